```python
import math
import jax, jax.numpy as jnp
from jax import lax
import numpy as np

D_MODEL = 2048
BATCH = 16
SEQ = 2048
DEPTH = 1

HEAD_DIM = 64
N_SWA_HEADS = 16
N_SWA_KV = 4
N_SB_HEADS = 16
WINDOW = 128
BLOCK = 128
D_FF = 5504
EPS = 1e-6

SWA_Q = N_SWA_HEADS * HEAD_DIM
SWA_KV = N_SWA_KV * HEAD_DIM
SB_W = N_SB_HEADS * HEAD_DIM
MIX_W = SWA_Q + SB_W
IN_W = SWA_Q + 2 * SWA_KV + 3 * SB_W

kernel_name = "hybrid_swa_sink_stickbreak_macaron"


def rmsnorm(x, g):
    xf = x.astype(jnp.float32)
    y = xf * lax.rsqrt(jnp.mean(xf * xf, axis=-1, keepdims=True) + EPS)
    return (y * g.astype(jnp.float32)).astype(x.dtype)


def swiglu(h, w_gate, w_up, w_down):
    return (jax.nn.silu(h @ w_gate) * (h @ w_up)) @ w_down


def alibi_slopes(n_heads):
    i = jnp.arange(1, n_heads + 1, dtype=jnp.float32)
    return jnp.exp2(-8.0 * i / n_heads)


def swa_sink_attention(q, k, v, sinks):
    B, S, H, D = q.shape
    Hkv = k.shape[2]
    G = H // Hkv
    nb = S // BLOCK
    qb = q.reshape(B, nb, BLOCK, Hkv, G, D)
    kb = k.reshape(B, nb, BLOCK, Hkv, D)
    vb = v.reshape(B, nb, BLOCK, Hkv, D)
    pad = ((0, 0), (1, 0), (0, 0), (0, 0), (0, 0))
    kk = jnp.concatenate([jnp.pad(kb[:, :-1], pad), kb], axis=2)
    vv = jnp.concatenate([jnp.pad(vb[:, :-1], pad), vb], axis=2)
    s = jnp.einsum('bnqhgd,bnkhd->bnhgqk', qb, kk).astype(jnp.float32) * (D ** -0.5)
    q_pos = jnp.arange(BLOCK)[:, None] + BLOCK
    k_pos = jnp.arange(2 * BLOCK)[None, :]
    dist = q_pos - k_pos
    valid = (dist >= 0) & (dist < WINDOW)
    blk = jnp.arange(nb)[:, None, None]
    valid = valid[None] & ((blk > 0) | (k_pos[None] >= BLOCK))
    slopes = alibi_slopes(H).reshape(Hkv, G)
    s = s - slopes[:, :, None, None] * dist.astype(jnp.float32)
    s = jnp.where(valid[None, :, None, None], s, -jnp.inf)
    sink = jnp.broadcast_to(sinks.astype(jnp.float32).reshape(1, 1, Hkv, G, 1, 1),
                            s.shape[:-1] + (1,))
    p = jax.nn.softmax(jnp.concatenate([s, sink], axis=-1), axis=-1)[..., :-1]
    o = jnp.einsum('bnhgqk,bnkhd->bnqhgd', p.astype(v.dtype), vv)
    return o.reshape(B, S, H * D)


def stick_breaking_attention(q, k, v):
    B, S, H, D = q.shape
    nb = S // BLOCK
    qb = q.reshape(B, nb, BLOCK, H, D).transpose(1, 0, 2, 3, 4)
    k_pos = jnp.arange(S)

    def one_block(args):
        q_blk, n = args
        z = jnp.einsum('bqhd,bkhd->bhqk', q_blk, k).astype(jnp.float32) * (D ** -0.5)
        q_pos = n * BLOCK + jnp.arange(BLOCK)
        mask = k_pos[None, :] < q_pos[:, None]
        log_beta = jax.nn.log_sigmoid(z)
        log_1m = jnp.where(mask, jax.nn.log_sigmoid(-z), 0.0)
        after = lax.cumsum(log_1m, axis=3, reverse=True) - log_1m
        a = jnp.where(mask, jnp.exp(log_beta + after), 0.0)
        return jnp.einsum('bhqk,bkhd->bqhd', a.astype(v.dtype), v)

    o = lax.map(one_block, (qb, jnp.arange(nb)))
    return o.transpose(1, 0, 2, 3, 4).reshape(B, S, H * D)


def setup_inputs(seed: int = 0) -> dict:
    key = jax.random.key(seed)
    ks = jax.random.split(key, 16)
    L = DEPTH

    def w(k, shape, fan_in):
        return jax.random.normal(k, shape, jnp.float32) * (fan_in ** -0.5)

    def gain(k, shape):
        return 1.0 + 0.02 * jax.random.normal(k, shape, jnp.float32)

    return {
        "x": jax.random.normal(ks[0], (BATCH, SEQ, D_MODEL), jnp.float32),
        "ffn1_norm": gain(ks[1], (L, D_MODEL)),
        "ffn1_w_gate": w(ks[2], (L, D_MODEL, D_FF), D_MODEL),
        "ffn1_w_up": w(ks[3], (L, D_MODEL, D_FF), D_MODEL),
        "ffn1_w_down": w(ks[4], (L, D_FF, D_MODEL), D_FF),
        "mix_norm": gain(ks[5], (L, D_MODEL)),
        "w_in": w(ks[6], (L, D_MODEL, IN_W), D_MODEL),
        "swa_sinks": jax.random.normal(ks[7], (L, N_SWA_HEADS), jnp.float32),
        "swa_out_norm": gain(ks[8], (L, SWA_Q)),
        "sb_out_norm": gain(ks[9], (L, SB_W)),
        "w_out": w(ks[10], (L, MIX_W, D_MODEL), MIX_W),
        "ffn2_norm": gain(ks[11], (L, D_MODEL)),
        "ffn2_w_gate": w(ks[12], (L, D_MODEL, D_FF), D_MODEL),
        "ffn2_w_up": w(ks[13], (L, D_MODEL, D_FF), D_MODEL),
        "ffn2_w_down": w(ks[14], (L, D_FF, D_MODEL), D_FF),
        "final_norm": gain(ks[15], (D_MODEL,)),
    }


def reference(x, ffn1_norm, ffn1_w_gate, ffn1_w_up, ffn1_w_down, mix_norm, w_in,
              swa_sinks, swa_out_norm, sb_out_norm, w_out, ffn2_norm, ffn2_w_gate,
              ffn2_w_up, ffn2_w_down, final_norm):
    B, S, _ = x.shape
    for l in range(DEPTH):
        x = x + 0.5 * swiglu(rmsnorm(x, ffn1_norm[l]), ffn1_w_gate[l], ffn1_w_up[l], ffn1_w_down[l])

        h = rmsnorm(x, mix_norm[l])
        proj = h @ w_in[l]
        o1 = SWA_Q
        o2 = o1 + SWA_KV
        o3 = o2 + SWA_KV
        o4 = o3 + SB_W
        o5 = o4 + SB_W
        qa = proj[..., :o1].reshape(B, S, N_SWA_HEADS, HEAD_DIM)
        ka = proj[..., o1:o2].reshape(B, S, N_SWA_KV, HEAD_DIM)
        va = proj[..., o2:o3].reshape(B, S, N_SWA_KV, HEAD_DIM)
        qb = proj[..., o3:o4].reshape(B, S, N_SB_HEADS, HEAD_DIM)
        kb = proj[..., o4:o5].reshape(B, S, N_SB_HEADS, HEAD_DIM)
        vb = proj[..., o5:].reshape(B, S, N_SB_HEADS, HEAD_DIM)

        ya = swa_sink_attention(qa, ka, va, swa_sinks[l])
        yb = stick_breaking_attention(qb, kb, vb)

        y = jnp.concatenate([rmsnorm(ya, swa_out_norm[l]), rmsnorm(yb, sb_out_norm[l])], axis=-1)
        x = x + y @ w_out[l]

        x = x + 0.5 * swiglu(rmsnorm(x, ffn2_norm[l]), ffn2_w_gate[l], ffn2_w_up[l], ffn2_w_down[l])
    return rmsnorm(x, final_norm)
```

```python
import functools

import jax
import jax.numpy as jnp
from jax import lax
from jax.experimental import pallas as pl
from jax.experimental.pallas import tpu as pltpu

D_MODEL = 2048
HEAD_DIM = 64
N_SWA_HEADS = 16
N_SWA_KV = 4
N_SB_HEADS = 16
WINDOW = 128
BLOCK = 128
D_FF = 5504
EPS = 1e-6

LANES = 128
D_FF_PAD = 5632
FFN_TM = 512
FFN_TF = 512
PROJ_TM = 512
PROJ_TN = 1024
SB_TQ = 256
SB_TK = 128
VMEM_LIMIT = 56 * 1024 * 1024

SWA_Q = N_SWA_HEADS * HEAD_DIM
SB_W = N_SB_HEADS * HEAD_DIM
COL_KA = SWA_Q
COL_VA = COL_KA + 2 * N_SWA_KV * HEAD_DIM
COL_QB = COL_VA + 2 * N_SWA_KV * HEAD_DIM
COL_KB = COL_QB + SB_W
COL_VB = COL_KB + SB_W
PROJ_W = COL_VB + SB_W

_NT = (((1,), (1,)), ((), ()))


def _rms(x, g):
    return x * lax.rsqrt(jnp.mean(x * x, axis=-1, keepdims=True) + EPS) * g


def _ffn_kernel(x_ref, g_ref, wg_ref, wu_ref, wd_ref, fg_ref, o_ref, h_ref, *, final_norm):
    j = pl.program_id(1)

    @pl.when(j == 0)
    def _():
        x = x_ref[...]
        h_ref[...] = _rms(x, g_ref[...]).astype(jnp.bfloat16)
        o_ref[...] = x

    h = h_ref[...]
    gate = jnp.dot(h, wg_ref[...], preferred_element_type=jnp.float32)
    up = jnp.dot(h, wu_ref[...], preferred_element_type=jnp.float32)
    act = (gate * jax.nn.sigmoid(gate) * up * 0.5).astype(jnp.bfloat16)
    o_ref[...] += jnp.dot(act, wd_ref[...], preferred_element_type=jnp.float32)

    if final_norm:
        @pl.when(j == pl.num_programs(1) - 1)
        def _():
            o_ref[...] = _rms(o_ref[...], fg_ref[...])


def _ffn(x, g, wg, wu, wd, fg, final_norm):
    n = x.shape[0]
    grid = (n // FFN_TM, D_FF_PAD // FFN_TF)
    return pl.pallas_call(
        functools.partial(_ffn_kernel, final_norm=final_norm),
        grid=grid,
        in_specs=[
            pl.BlockSpec((FFN_TM, D_MODEL), lambda i, j: (i, 0)),
            pl.BlockSpec((1, D_MODEL), lambda i, j: (0, 0)),
            pl.BlockSpec((D_MODEL, FFN_TF), lambda i, j: (0, j)),
            pl.BlockSpec((D_MODEL, FFN_TF), lambda i, j: (0, j)),
            pl.BlockSpec((FFN_TF, D_MODEL), lambda i, j: (j, 0)),
            pl.BlockSpec((1, D_MODEL), lambda i, j: (0, 0)),
        ],
        out_specs=pl.BlockSpec((FFN_TM, D_MODEL), lambda i, j: (i, 0)),
        out_shape=jax.ShapeDtypeStruct((n, D_MODEL), jnp.float32),
        scratch_shapes=[pltpu.VMEM((FFN_TM, D_MODEL), jnp.bfloat16)],
        compiler_params=pltpu.CompilerParams(
            dimension_semantics=("arbitrary", "arbitrary"),
            vmem_limit_bytes=VMEM_LIMIT),
        name="ffn_final" if final_norm else "ffn",
    )(x, g, wg, wu, wd, fg)


def _in_proj_kernel(x_ref, g_ref, w_ref, o_ref, h_ref):
    @pl.when(pl.program_id(1) == 0)
    def _():
        h_ref[...] = _rms(x_ref[...], g_ref[...]).astype(jnp.bfloat16)

    o_ref[...] = jnp.dot(h_ref[...], w_ref[...],
                         preferred_element_type=jnp.float32).astype(jnp.bfloat16)


def _in_proj(x, g, w):
    n = x.shape[0]
    grid = (n // PROJ_TM, PROJ_W // PROJ_TN)
    return pl.pallas_call(
        _in_proj_kernel,
        grid=grid,
        in_specs=[
            pl.BlockSpec((PROJ_TM, D_MODEL), lambda i, j: (i, 0)),
            pl.BlockSpec((1, D_MODEL), lambda i, j: (0, 0)),
            pl.BlockSpec((D_MODEL, PROJ_TN), lambda i, j: (0, j)),
        ],
        out_specs=pl.BlockSpec((PROJ_TM, PROJ_TN), lambda i, j: (i, j)),
        out_shape=jax.ShapeDtypeStruct((n, PROJ_W), jnp.bfloat16),
        scratch_shapes=[pltpu.VMEM((PROJ_TM, D_MODEL), jnp.bfloat16)],
        compiler_params=pltpu.CompilerParams(
            dimension_semantics=("arbitrary", "arbitrary"),
            vmem_limit_bytes=VMEM_LIMIT),
        name="in_proj",
    )(x, g, w)


def _swa_kernel(slopes_ref, sinks_ref, q_ref, kp_ref, kc_ref, vp_ref, vc_ref, o_ref):
    g = pl.program_id(1)
    qb = pl.program_id(2)
    q = q_ref[0] * jnp.bfloat16(HEAD_DIM ** -0.5)
    kp, kc, vp, vc = kp_ref[0], kc_ref[0], vp_ref[0], vc_ref[0]
    row = lax.broadcasted_iota(jnp.int32, (BLOCK, BLOCK), 0)
    col = lax.broadcasted_iota(jnp.int32, (BLOCK, BLOCK), 1)
    low_half = col < HEAD_DIM
    dist_c = (row - col).astype(jnp.float32)
    dist_p = (row + BLOCK - col).astype(jnp.float32)
    valid_c = col <= row
    valid_p = jnp.logical_and(col > row, qb > 0)
    outs = []
    for jh in range(N_SWA_HEADS // N_SWA_KV):
        head = g * (N_SWA_HEADS // N_SWA_KV) + jh
        slope = slopes_ref[head]
        sink = sinks_ref[head]
        qpair = q[:, (jh // 2) * LANES:(jh // 2 + 1) * LANES]
        keep = low_half if jh % 2 == 0 else jnp.logical_not(low_half)
        qh = jnp.where(keep, qpair, jnp.zeros_like(qpair))
        s_c = lax.dot_general(qh, kc, _NT, preferred_element_type=jnp.float32)
        s_p = lax.dot_general(qh, kp, _NT, preferred_element_type=jnp.float32)
        s_c = jnp.where(valid_c, s_c - slope * dist_c, -jnp.inf)
        s_p = jnp.where(valid_p, s_p - slope * dist_p, -jnp.inf)
        m = jnp.maximum(jnp.maximum(jnp.max(s_c, axis=-1, keepdims=True),
                                    jnp.max(s_p, axis=-1, keepdims=True)), sink)
        e_c = jnp.exp(s_c - m)
        e_p = jnp.exp(s_p - m)
        denom = (jnp.sum(e_c, axis=-1, keepdims=True) + jnp.sum(e_p, axis=-1, keepdims=True)
                 + jnp.exp(sink - m))
        pv = (jnp.dot(e_c.astype(jnp.bfloat16), vc, preferred_element_type=jnp.float32)
              + jnp.dot(e_p.astype(jnp.bfloat16), vp, preferred_element_type=jnp.float32))
        outs.append(pv / denom)
    o_ref[0] = jnp.concatenate(
        [jnp.where(low_half, outs[0], outs[1]), jnp.where(low_half, outs[2], outs[3])], axis=1)


def _swa(proj, slopes, sinks):
    b, s, _ = proj.shape
    nb = s // BLOCK
    kblk, vblk = COL_KA // LANES, COL_VA // LANES
    smem = pl.BlockSpec(memory_space=pltpu.SMEM)
    return pl.pallas_call(
        _swa_kernel,
        grid=(b, N_SWA_KV, nb),
        in_specs=[
            smem, smem,
            pl.BlockSpec((1, BLOCK, 2 * LANES), lambda i, g, n: (i, n, g)),
            pl.BlockSpec((1, BLOCK, LANES), lambda i, g, n: (i, jnp.maximum(n - 1, 0), kblk + g)),
            pl.BlockSpec((1, BLOCK, LANES), lambda i, g, n: (i, n, kblk + g)),
            pl.BlockSpec((1, BLOCK, LANES), lambda i, g, n: (i, jnp.maximum(n - 1, 0), vblk + g)),
            pl.BlockSpec((1, BLOCK, LANES), lambda i, g, n: (i, n, vblk + g)),
        ],
        out_specs=pl.BlockSpec((1, BLOCK, 2 * LANES), lambda i, g, n: (i, n, g)),
        out_shape=jax.ShapeDtypeStruct((b, s, SWA_Q), jnp.float32),
        compiler_params=pltpu.CompilerParams(
            dimension_semantics=("arbitrary", "arbitrary", "arbitrary")),
        name="swa",
    )(slopes, sinks, proj, proj, proj, proj, proj)


def _sb_kernel(q_ref, k_ref, v_ref, o_ref, carry_ref, acc_ref):
    s_len = q_ref.shape[1]
    nq = s_len // SB_TQ
    r = lax.broadcasted_iota(jnp.int32, (2 * SB_TK, 2 * SB_TK), 0)
    c = lax.broadcasted_iota(jnp.int32, (2 * SB_TK, 2 * SB_TK), 1)
    tri = jnp.where(jnp.logical_or(c >= SB_TK, (r % SB_TK) >= c), 1.0, 0.0).astype(jnp.bfloat16)
    lane = lax.broadcasted_iota(jnp.int32, (SB_TQ, LANES), 1)
    low_half = lane < HEAD_DIM
    row = lax.broadcasted_iota(jnp.int32, (SB_TQ, SB_TK), 0)
    col = lax.broadcasted_iota(jnp.int32, (SB_TQ, SB_TK), 1)

    def q_block(qi, _):
        q0 = pl.multiple_of(qi * SB_TQ, SB_TQ)
        q = q_ref[0, pl.ds(q0, SB_TQ), :] * jnp.bfloat16(HEAD_DIM ** -0.5)
        zero = jnp.zeros_like(q)
        qh = (jnp.where(low_half, q, zero), jnp.where(low_half, zero, q))
        carry_ref[...] = jnp.zeros_like(carry_ref)
        acc_ref[...] = jnp.zeros_like(acc_ref)

        def k_block(kb, masked):
            k0 = pl.multiple_of(kb * SB_TK, SB_TK)
            kblk = k_ref[0, pl.ds(k0, SB_TK), :]
            vblk = v_ref[0, pl.ds(k0, SB_TK), :]
            if masked:
                mask = (k0 + col) < (q0 + row)
            for hd in range(2):
                z = lax.dot_general(qh[hd], kblk, _NT, preferred_element_type=jnp.float32)
                sp = jnp.maximum(z, 0.0) + jnp.log(1.0 + jnp.exp(-jnp.abs(z)))
                if masked:
                    sp = jnp.where(mask, sp, 0.0)
                hi = sp.astype(jnp.bfloat16)
                lo = (sp - hi.astype(jnp.float32)).astype(jnp.bfloat16)
                sums = jnp.dot(jnp.concatenate([hi, lo], axis=1), tri,
                               preferred_element_type=jnp.float32)
                carry = carry_ref[hd]
                a = jnp.exp(z - (carry + sums[:, :SB_TK]))
                if masked:
                    a = jnp.where(mask, a, 0.0)
                carry_ref[hd] = carry + sums[:, SB_TK:]
                acc_ref[hd] += jnp.dot(a.astype(jnp.bfloat16), vblk,
                                       preferred_element_type=jnp.float32)

        n_diag = SB_TQ // SB_TK
        for d in range(n_diag):
            k_block(qi * n_diag + (n_diag - 1 - d), True)

        def body(i, _):
            k_block(qi * n_diag - 1 - i, False)
            return 0

        lax.fori_loop(0, qi * n_diag, body, 0)
        o_ref[0, pl.ds(q0, SB_TQ), :] = jnp.where(low_half, acc_ref[0], acc_ref[1])
        return 0

    lax.fori_loop(0, nq, q_block, 0)


def _sb(proj):
    b, s, _ = proj.shape
    npair = N_SB_HEADS // 2
    qblk, kblk, vblk = COL_QB // LANES, COL_KB // LANES, COL_VB // LANES
    return pl.pallas_call(
        _sb_kernel,
        grid=(b, npair),
        in_specs=[
            pl.BlockSpec((1, s, LANES), lambda i, p: (i, 0, qblk + p)),
            pl.BlockSpec((1, s, LANES), lambda i, p: (i, 0, kblk + p)),
            pl.BlockSpec((1, s, LANES), lambda i, p: (i, 0, vblk + p)),
        ],
        out_specs=pl.BlockSpec((1, s, LANES), lambda i, p: (i, 0, p)),
        out_shape=jax.ShapeDtypeStruct((b, s, SB_W), jnp.float32),
        scratch_shapes=[pltpu.VMEM((2, SB_TQ, SB_TK), jnp.float32),
                        pltpu.VMEM((2, SB_TQ, LANES), jnp.float32)],
        compiler_params=pltpu.CompilerParams(
            dimension_semantics=("arbitrary", "arbitrary")),
        name="sb",
    )(proj, proj, proj)


def _out_proj_kernel(x_ref, ya_ref, yb_ref, ga_ref, gb_ref, w_ref, o_ref):
    y = jnp.concatenate([_rms(ya_ref[...], ga_ref[...]), _rms(yb_ref[...], gb_ref[...])],
                        axis=1).astype(jnp.bfloat16)
    o_ref[...] = x_ref[...] + jnp.dot(y, w_ref[...], preferred_element_type=jnp.float32)


def _out_proj(x, ya, yb, ga, gb, w):
    n = x.shape[0]
    return pl.pallas_call(
        _out_proj_kernel,
        grid=(n // PROJ_TM,),
        in_specs=[
            pl.BlockSpec((PROJ_TM, D_MODEL), lambda i: (i, 0)),
            pl.BlockSpec((PROJ_TM, SWA_Q), lambda i: (i, 0)),
            pl.BlockSpec((PROJ_TM, SB_W), lambda i: (i, 0)),
            pl.BlockSpec((1, SWA_Q), lambda i: (0, 0)),
            pl.BlockSpec((1, SB_W), lambda i: (0, 0)),
            pl.BlockSpec((SWA_Q + SB_W, D_MODEL), lambda i: (0, 0)),
        ],
        out_specs=pl.BlockSpec((PROJ_TM, D_MODEL), lambda i: (i, 0)),
        out_shape=jax.ShapeDtypeStruct((n, D_MODEL), jnp.float32),
        compiler_params=pltpu.CompilerParams(
            dimension_semantics=("arbitrary",),
            vmem_limit_bytes=VMEM_LIMIT),
        name="out_proj",
    )(x, ya, yb, ga, gb, w)


def _prep_ffn_weights(w_gate, w_up, w_down):
    pad = D_FF_PAD - D_FF
    wg = jnp.pad(w_gate, ((0, 0), (0, pad))).astype(jnp.bfloat16)
    wu = jnp.pad(w_up, ((0, 0), (0, pad))).astype(jnp.bfloat16)
    wd = jnp.pad(w_down, ((0, pad), (0, 0))).astype(jnp.bfloat16)
    return wg, wu, wd


def _prep_w_in(w):
    def twice(cols):
        c = cols.reshape(D_MODEL, N_SWA_KV, 1, HEAD_DIM)
        return jnp.broadcast_to(c, (D_MODEL, N_SWA_KV, 2, HEAD_DIM)).reshape(D_MODEL, -1)
    o1 = SWA_Q
    o2 = o1 + N_SWA_KV * HEAD_DIM
    o3 = o2 + N_SWA_KV * HEAD_DIM
    return jnp.concatenate([w[:, :o1], twice(w[:, o1:o2]), twice(w[:, o2:o3]), w[:, o3:]],
                           axis=1).astype(jnp.bfloat16)


def kernel(x, ffn1_norm, ffn1_w_gate, ffn1_w_up, ffn1_w_down, mix_norm, w_in, swa_sinks,
           swa_out_norm, sb_out_norm, w_out, ffn2_norm, ffn2_w_gate, ffn2_w_up, ffn2_w_down,
           final_norm):
    b, s, d = x.shape
    depth = ffn1_norm.shape[0]
    xf = x.reshape(b * s, d)
    i = jnp.arange(1, N_SWA_HEADS + 1, dtype=jnp.float32)
    slopes = jnp.exp2(-8.0 * i / N_SWA_HEADS)
    fg = final_norm.reshape(1, d)
    for l in range(depth):
        wg, wu, wd = _prep_ffn_weights(ffn1_w_gate[l], ffn1_w_up[l], ffn1_w_down[l])
        xf = _ffn(xf, ffn1_norm[l].reshape(1, d), wg, wu, wd, fg, final_norm=False)

        proj = _in_proj(xf, mix_norm[l].reshape(1, d), _prep_w_in(w_in[l]))
        proj = proj.reshape(b, s, PROJ_W)
        ya = _swa(proj, slopes, swa_sinks[l])
        yb = _sb(proj)
        xf = _out_proj(xf, ya.reshape(b * s, SWA_Q), yb.reshape(b * s, SB_W),
                       swa_out_norm[l].reshape(1, SWA_Q), sb_out_norm[l].reshape(1, SB_W),
                       w_out[l].astype(jnp.bfloat16))

        wg, wu, wd = _prep_ffn_weights(ffn2_w_gate[l], ffn2_w_up[l], ffn2_w_down[l])
        xf = _ffn(xf, ffn2_norm[l].reshape(1, d), wg, wu, wd, fg,
                  final_norm=(l == depth - 1))
    if depth == 0:
        xf = xf * lax.rsqrt(jnp.mean(xf * xf, axis=-1, keepdims=True) + EPS) * fg
    return xf.reshape(b, s, d)
```

```python
import functools
import math

import jax
import jax.numpy as jnp
from jax import lax
from jax.experimental import pallas as pl
from jax.experimental.pallas import tpu as pltpu

D_MODEL = 2048
HEAD_DIM = 64
N_SWA_HEADS = 16
N_SWA_KV = 4
N_SB_HEADS = 16
WINDOW = 128
BLOCK = 128
D_FF = 5504
EPS = 1e-6
LOG2E = math.log2(math.e)

LANES = 128
D_FF_PAD = 5632
FFN_TM = 512
FFN_TF = 512
PROJ_TM = 512
PROJ_TN = 1024
SB_TQ = 512
SB_TK = 128
VMEM_LIMIT = 56 * 1024 * 1024

SWA_Q = N_SWA_HEADS * HEAD_DIM
SB_W = N_SB_HEADS * HEAD_DIM
COL_KA = SWA_Q
COL_VA = COL_KA + 2 * N_SWA_KV * HEAD_DIM
COL_QB = COL_VA + 2 * N_SWA_KV * HEAD_DIM
COL_KB = COL_QB + SB_W
COL_VB = COL_KB + SB_W
PROJ_W = COL_VB + SB_W

_NT = (((1,), (1,)), ((), ()))


def _rms(x, g):
    return x * lax.rsqrt(jnp.mean(x * x, axis=-1, keepdims=True) + EPS) * g


def _ffn_kernel(x_ref, g_ref, wg_ref, wu_ref, wd_ref, fg_ref, o_ref, h_ref, *, final_norm):
    j = pl.program_id(1)

    @pl.when(j == 0)
    def _():
        x = x_ref[...]
        h_ref[...] = _rms(x, g_ref[...]).astype(jnp.bfloat16)
        o_ref[...] = x

    h = h_ref[...]
    gate = jnp.dot(h, wg_ref[...], preferred_element_type=jnp.float32)
    up = jnp.dot(h, wu_ref[...], preferred_element_type=jnp.float32)
    act = (gate * jax.nn.sigmoid(gate) * up * 0.5).astype(jnp.bfloat16)
    o_ref[...] += jnp.dot(act, wd_ref[...], preferred_element_type=jnp.float32)

    if final_norm:
        @pl.when(j == pl.num_programs(1) - 1)
        def _():
            o_ref[...] = _rms(o_ref[...], fg_ref[...])


def _ffn(x, g, wg, wu, wd, fg, final_norm):
    n = x.shape[0]
    grid = (n // FFN_TM, D_FF_PAD // FFN_TF)
    return pl.pallas_call(
        functools.partial(_ffn_kernel, final_norm=final_norm),
        grid=grid,
        in_specs=[
            pl.BlockSpec((FFN_TM, D_MODEL), lambda i, j: (i, 0)),
            pl.BlockSpec((1, D_MODEL), lambda i, j: (0, 0)),
            pl.BlockSpec((D_MODEL, FFN_TF), lambda i, j: (0, j)),
            pl.BlockSpec((D_MODEL, FFN_TF), lambda i, j: (0, j)),
            pl.BlockSpec((FFN_TF, D_MODEL), lambda i, j: (j, 0)),
            pl.BlockSpec((1, D_MODEL), lambda i, j: (0, 0)),
        ],
        out_specs=pl.BlockSpec((FFN_TM, D_MODEL), lambda i, j: (i, 0)),
        out_shape=jax.ShapeDtypeStruct((n, D_MODEL), jnp.float32),
        scratch_shapes=[pltpu.VMEM((FFN_TM, D_MODEL), jnp.bfloat16)],
        compiler_params=pltpu.CompilerParams(
            dimension_semantics=("arbitrary", "arbitrary"),
            vmem_limit_bytes=VMEM_LIMIT),
        name="ffn_final" if final_norm else "ffn",
    )(x, g, wg, wu, wd, fg)


def _in_proj_kernel(x_ref, g_ref, w_ref, s_ref, o_ref, h_ref):
    @pl.when(pl.program_id(1) == 0)
    def _():
        h_ref[...] = _rms(x_ref[...], g_ref[...]).astype(jnp.bfloat16)

    acc = jnp.dot(h_ref[...], w_ref[...], preferred_element_type=jnp.float32)
    o_ref[...] = (acc * s_ref[...]).astype(jnp.bfloat16)


def _in_proj(x, g, w, col_scale):
    n = x.shape[0]
    grid = (n // PROJ_TM, PROJ_W // PROJ_TN)
    return pl.pallas_call(
        _in_proj_kernel,
        grid=grid,
        in_specs=[
            pl.BlockSpec((PROJ_TM, D_MODEL), lambda i, j: (i, 0)),
            pl.BlockSpec((1, D_MODEL), lambda i, j: (0, 0)),
            pl.BlockSpec((D_MODEL, PROJ_TN), lambda i, j: (0, j)),
            pl.BlockSpec((1, PROJ_TN), lambda i, j: (0, j)),
        ],
        out_specs=pl.BlockSpec((PROJ_TM, PROJ_TN), lambda i, j: (i, j)),
        out_shape=jax.ShapeDtypeStruct((n, PROJ_W), jnp.bfloat16),
        scratch_shapes=[pltpu.VMEM((PROJ_TM, D_MODEL), jnp.bfloat16)],
        compiler_params=pltpu.CompilerParams(
            dimension_semantics=("arbitrary", "arbitrary"),
            vmem_limit_bytes=VMEM_LIMIT),
        name="in_proj",
    )(x, g, w, col_scale)


def _swa_kernel(slopes_ref, sinks_ref, q_ref, kp_ref, kc_ref, vp_ref, vc_ref, o_ref):
    g = pl.program_id(1)
    qb = pl.program_id(2)
    q = q_ref[0]
    kp, kc, vp, vc = kp_ref[0], kc_ref[0], vp_ref[0], vc_ref[0]
    row = lax.broadcasted_iota(jnp.int32, (BLOCK, BLOCK), 0)
    col = lax.broadcasted_iota(jnp.int32, (BLOCK, BLOCK), 1)
    low_half = col < HEAD_DIM
    dist_c = (row - col).astype(jnp.float32)
    dist_p = (row + BLOCK - col).astype(jnp.float32)
    valid_c = col <= row
    valid_p = jnp.logical_and(col > row, qb > 0)
    outs = []
    for jh in range(N_SWA_HEADS // N_SWA_KV):
        head = g * (N_SWA_HEADS // N_SWA_KV) + jh
        slope = slopes_ref[head]
        sink = sinks_ref[head]
        qpair = q[:, (jh // 2) * LANES:(jh // 2 + 1) * LANES]
        keep = low_half if jh % 2 == 0 else jnp.logical_not(low_half)
        qh = jnp.where(keep, qpair, jnp.zeros_like(qpair))
        s_c = lax.dot_general(qh, kc, _NT, preferred_element_type=jnp.float32)
        s_p = lax.dot_general(qh, kp, _NT, preferred_element_type=jnp.float32)
        s_c = jnp.where(valid_c, s_c - slope * dist_c, -jnp.inf)
        s_p = jnp.where(valid_p, s_p - slope * dist_p, -jnp.inf)
        m = jnp.maximum(jnp.maximum(jnp.max(s_c, axis=-1, keepdims=True),
                                    jnp.max(s_p, axis=-1, keepdims=True)), sink)
        e_c = jnp.exp(s_c - m)
        e_p = jnp.exp(s_p - m)
        denom = (jnp.sum(e_c, axis=-1, keepdims=True) + jnp.sum(e_p, axis=-1, keepdims=True)
                 + jnp.exp(sink - m))
        pv = (jnp.dot(e_c.astype(jnp.bfloat16), vc, preferred_element_type=jnp.float32)
              + jnp.dot(e_p.astype(jnp.bfloat16), vp, preferred_element_type=jnp.float32))
        outs.append(pv / denom)
    o_ref[0] = jnp.concatenate(
        [jnp.where(low_half, outs[0], outs[1]), jnp.where(low_half, outs[2], outs[3])], axis=1)


def _swa(proj, slopes, sinks):
    b, s, _ = proj.shape
    nb = s // BLOCK
    kblk, vblk = COL_KA // LANES, COL_VA // LANES
    smem = pl.BlockSpec(memory_space=pltpu.SMEM)
    return pl.pallas_call(
        _swa_kernel,
        grid=(b, N_SWA_KV, nb),
        in_specs=[
            smem, smem,
            pl.BlockSpec((1, BLOCK, 2 * LANES), lambda i, g, n: (i, n, g)),
            pl.BlockSpec((1, BLOCK, LANES), lambda i, g, n: (i, jnp.maximum(n - 1, 0), kblk + g)),
            pl.BlockSpec((1, BLOCK, LANES), lambda i, g, n: (i, n, kblk + g)),
            pl.BlockSpec((1, BLOCK, LANES), lambda i, g, n: (i, jnp.maximum(n - 1, 0), vblk + g)),
            pl.BlockSpec((1, BLOCK, LANES), lambda i, g, n: (i, n, vblk + g)),
        ],
        out_specs=pl.BlockSpec((1, BLOCK, 2 * LANES), lambda i, g, n: (i, n, g)),
        out_shape=jax.ShapeDtypeStruct((b, s, SWA_Q), jnp.float32),
        compiler_params=pltpu.CompilerParams(
            dimension_semantics=("arbitrary", "arbitrary", "arbitrary")),
        name="swa",
    )(slopes, sinks, proj, proj, proj, proj, proj)


def _sb_kernel(q_ref, k_ref, v_ref, o_ref, carry_ref, acc_ref):
    s_len = q_ref.shape[1]
    nq = s_len // SB_TQ
    n_sub = SB_TQ // SB_TK
    r = lax.broadcasted_iota(jnp.int32, (2 * SB_TK, 2 * SB_TK), 0)
    c = lax.broadcasted_iota(jnp.int32, (2 * SB_TK, 2 * SB_TK), 1)
    tri = jnp.where(jnp.logical_or(c >= SB_TK, (r % SB_TK) >= c), 1.0, 0.0).astype(jnp.bfloat16)
    low_half = lax.broadcasted_iota(jnp.int32, (SB_TK, LANES), 1) < HEAD_DIM
    strict_lower = (lax.broadcasted_iota(jnp.int32, (SB_TK, SB_TK), 1)
                    < lax.broadcasted_iota(jnp.int32, (SB_TK, SB_TK), 0))

    def q_block(qi, _):
        q0 = pl.multiple_of(qi * SB_TQ, SB_TQ)
        carry_ref[...] = jnp.zeros_like(carry_ref)
        acc_ref[...] = jnp.zeros_like(acc_ref)

        def k_block(k0, r0, diag):
            kblk = k_ref[0, pl.ds(k0, SB_TK), :]
            vblk = v_ref[0, pl.ds(k0, SB_TK), :]
            zero = jnp.zeros_like(kblk)
            kk = jnp.concatenate([jnp.where(low_half, kblk, zero),
                                  jnp.where(low_half, zero, kblk)], axis=0)
            vv = jnp.concatenate([jnp.where(low_half, vblk, zero),
                                  jnp.where(low_half, zero, vblk)], axis=0)
            def mask_top(x):
                top = jnp.where(strict_lower, x[:SB_TK], 0.0)
                return top if r0 == SB_TQ - SB_TK else jnp.concatenate([top, x[SB_TK:]], axis=0)

            q = q_ref[0, pl.ds(q0 + r0, SB_TQ - r0), :]
            u2 = lax.dot_general(q, kk, _NT, preferred_element_type=jnp.float32)
            a_heads = []
            for hd in range(2):
                u = u2[:, hd * SB_TK:(hd + 1) * SB_TK]
                sp = jnp.maximum(u, 0.0) + jnp.log(1.0 + jnp.exp2(-jnp.abs(u))) * LOG2E
                if diag:
                    sp = mask_top(sp)
                hi = sp.astype(jnp.bfloat16)
                lo = (sp - hi.astype(jnp.float32)).astype(jnp.bfloat16)
                sums = jnp.dot(jnp.concatenate([hi, lo], axis=1), tri,
                               preferred_element_type=jnp.float32)
                carry = carry_ref[hd, r0:, :]
                a = jnp.exp2(u - (carry + sums[:, :SB_TK]))
                if diag:
                    a = mask_top(a)
                carry_ref[hd, r0:, :] = carry + sums[:, SB_TK:]
                a_heads.append(a.astype(jnp.bfloat16))
            acc_ref[r0:, :] += jnp.dot(jnp.concatenate(a_heads, axis=1), vv,
                                       preferred_element_type=jnp.float32)

        for d in range(n_sub - 1, -1, -1):
            k_block(pl.multiple_of(q0 + d * SB_TK, SB_TK), d * SB_TK, True)

        def chunk(i, _):
            base = q0 - (i + 1) * SB_TQ
            for d in range(n_sub - 1, -1, -1):
                k_block(pl.multiple_of(base + d * SB_TK, SB_TK), 0, False)
            return 0

        lax.fori_loop(0, qi, chunk, 0)
        o_ref[0, pl.ds(q0, SB_TQ), :] = acc_ref[...]
        return 0

    lax.fori_loop(0, nq, q_block, 0)


def _sb(proj):
    b, s, _ = proj.shape
    npair = N_SB_HEADS // 2
    qblk, kblk, vblk = COL_QB // LANES, COL_KB // LANES, COL_VB // LANES
    return pl.pallas_call(
        _sb_kernel,
        grid=(b, npair),
        in_specs=[
            pl.BlockSpec((1, s, LANES), lambda i, p: (i, 0, qblk + p)),
            pl.BlockSpec((1, s, LANES), lambda i, p: (i, 0, kblk + p)),
            pl.BlockSpec((1, s, LANES), lambda i, p: (i, 0, vblk + p)),
        ],
        out_specs=pl.BlockSpec((1, s, LANES), lambda i, p: (i, 0, p)),
        out_shape=jax.ShapeDtypeStruct((b, s, SB_W), jnp.float32),
        scratch_shapes=[pltpu.VMEM((2, SB_TQ, SB_TK), jnp.float32),
                        pltpu.VMEM((SB_TQ, LANES), jnp.float32)],
        compiler_params=pltpu.CompilerParams(
            dimension_semantics=("arbitrary", "arbitrary")),
        name="sb",
    )(proj, proj, proj)


def _out_proj_kernel(x_ref, ya_ref, yb_ref, ga_ref, gb_ref, w_ref, o_ref):
    y = jnp.concatenate([_rms(ya_ref[...], ga_ref[...]), _rms(yb_ref[...], gb_ref[...])],
                        axis=1).astype(jnp.bfloat16)
    o_ref[...] = x_ref[...] + jnp.dot(y, w_ref[...], preferred_element_type=jnp.float32)


def _out_proj(x, ya, yb, ga, gb, w):
    n = x.shape[0]
    return pl.pallas_call(
        _out_proj_kernel,
        grid=(n // PROJ_TM,),
        in_specs=[
            pl.BlockSpec((PROJ_TM, D_MODEL), lambda i: (i, 0)),
            pl.BlockSpec((PROJ_TM, SWA_Q), lambda i: (i, 0)),
            pl.BlockSpec((PROJ_TM, SB_W), lambda i: (i, 0)),
            pl.BlockSpec((1, SWA_Q), lambda i: (0, 0)),
            pl.BlockSpec((1, SB_W), lambda i: (0, 0)),
            pl.BlockSpec((SWA_Q + SB_W, D_MODEL), lambda i: (0, 0)),
        ],
        out_specs=pl.BlockSpec((PROJ_TM, D_MODEL), lambda i: (i, 0)),
        out_shape=jax.ShapeDtypeStruct((n, D_MODEL), jnp.float32),
        compiler_params=pltpu.CompilerParams(
            dimension_semantics=("arbitrary",),
            vmem_limit_bytes=VMEM_LIMIT),
        name="out_proj",
    )(x, ya, yb, ga, gb, w)


def _prep_ffn_weights(w_gate, w_up, w_down):
    pad = D_FF_PAD - D_FF
    wg = jnp.pad(w_gate, ((0, 0), (0, pad))).astype(jnp.bfloat16)
    wu = jnp.pad(w_up, ((0, 0), (0, pad))).astype(jnp.bfloat16)
    wd = jnp.pad(w_down, ((0, pad), (0, 0))).astype(jnp.bfloat16)
    return wg, wu, wd


def _prep_w_in(w):
    def twice(cols):
        c = cols.reshape(D_MODEL, N_SWA_KV, 1, HEAD_DIM)
        return jnp.broadcast_to(c, (D_MODEL, N_SWA_KV, 2, HEAD_DIM)).reshape(D_MODEL, -1)
    o1 = SWA_Q
    o2 = o1 + N_SWA_KV * HEAD_DIM
    o3 = o2 + N_SWA_KV * HEAD_DIM
    return jnp.concatenate([w[:, :o1], twice(w[:, o1:o2]), twice(w[:, o2:o3]), w[:, o3:]],
                           axis=1).astype(jnp.bfloat16)


def _proj_col_scale():
    s = jnp.ones((PROJ_W,), jnp.float32)
    s = s.at[:SWA_Q].set(HEAD_DIM ** -0.5)
    s = s.at[COL_QB:COL_KB].set(HEAD_DIM ** -0.5 * LOG2E)
    return s.reshape(1, PROJ_W)


def kernel(x, ffn1_norm, ffn1_w_gate, ffn1_w_up, ffn1_w_down, mix_norm, w_in, swa_sinks,
           swa_out_norm, sb_out_norm, w_out, ffn2_norm, ffn2_w_gate, ffn2_w_up, ffn2_w_down,
           final_norm):
    b, s, d = x.shape
    depth = ffn1_norm.shape[0]
    xf = x.reshape(b * s, d)
    i = jnp.arange(1, N_SWA_HEADS + 1, dtype=jnp.float32)
    slopes = jnp.exp2(-8.0 * i / N_SWA_HEADS)
    fg = final_norm.reshape(1, d)
    col_scale = _proj_col_scale()
    for l in range(depth):
        wg, wu, wd = _prep_ffn_weights(ffn1_w_gate[l], ffn1_w_up[l], ffn1_w_down[l])
        xf = _ffn(xf, ffn1_norm[l].reshape(1, d), wg, wu, wd, fg, final_norm=False)

        proj = _in_proj(xf, mix_norm[l].reshape(1, d), _prep_w_in(w_in[l]), col_scale)
        proj = proj.reshape(b, s, PROJ_W)
        ya = _swa(proj, slopes, swa_sinks[l])
        yb = _sb(proj)
        xf = _out_proj(xf, ya.reshape(b * s, SWA_Q), yb.reshape(b * s, SB_W),
                       swa_out_norm[l].reshape(1, SWA_Q), sb_out_norm[l].reshape(1, SB_W),
                       w_out[l].astype(jnp.bfloat16))

        wg, wu, wd = _prep_ffn_weights(ffn2_w_gate[l], ffn2_w_up[l], ffn2_w_down[l])
        xf = _ffn(xf, ffn2_norm[l].reshape(1, d), wg, wu, wd, fg,
                  final_norm=(l == depth - 1))
    return xf.reshape(b, s, d)
```

```python
import functools
import math

import jax
import jax.numpy as jnp
from jax import lax
from jax.experimental import pallas as pl
from jax.experimental.pallas import tpu as pltpu

D_MODEL = 2048
HEAD_DIM = 64
N_SWA_HEADS = 16
N_SWA_KV = 4
N_SB_HEADS = 16
WINDOW = 128
BLOCK = 128
D_FF = 5504
EPS = 1e-6
LOG2E = math.log2(math.e)

LANES = 128
D_FF_PAD = 5632
FFN_TM = 512
FFN_TF = 512
PROJ_TM = 512
PROJ_TN = 1024
SWA_QB = 4
SB_TQ = 512
SB_TK = 128
VMEM_LIMIT = 56 * 1024 * 1024

SWA_Q = N_SWA_HEADS * HEAD_DIM
SB_W = N_SB_HEADS * HEAD_DIM
COL_KA = SWA_Q
COL_VA = COL_KA + 2 * N_SWA_KV * HEAD_DIM
COL_QB = COL_VA + 2 * N_SWA_KV * HEAD_DIM
COL_KB = COL_QB + SB_W
COL_VB = COL_KB + SB_W
PROJ_W = COL_VB + SB_W

_NT = (((1,), (1,)), ((), ()))


def _rms(x, g):
    return x * lax.rsqrt(jnp.mean(x * x, axis=-1, keepdims=True) + EPS) * g


def _ffn_kernel(x_ref, g_ref, wg_ref, wu_ref, wd_ref, fg_ref, o_ref, h_ref, *, final_norm):
    j = pl.program_id(1)

    @pl.when(j == 0)
    def _():
        x = x_ref[...]
        h_ref[...] = _rms(x, g_ref[...]).astype(jnp.bfloat16)
        o_ref[...] = x

    h = h_ref[...]
    gate = jnp.dot(h, wg_ref[...], preferred_element_type=jnp.float32)
    up = jnp.dot(h, wu_ref[...], preferred_element_type=jnp.float32)
    act = (gate * jax.nn.sigmoid(gate) * up * 0.5).astype(jnp.bfloat16)
    o_ref[...] += jnp.dot(act, wd_ref[...], preferred_element_type=jnp.float32)

    if final_norm:
        @pl.when(j == pl.num_programs(1) - 1)
        def _():
            o_ref[...] = _rms(o_ref[...], fg_ref[...])


def _ffn(x, g, wg, wu, wd, fg, final_norm):
    n = x.shape[0]
    grid = (n // FFN_TM, D_FF_PAD // FFN_TF)
    return pl.pallas_call(
        functools.partial(_ffn_kernel, final_norm=final_norm),
        grid=grid,
        in_specs=[
            pl.BlockSpec((FFN_TM, D_MODEL), lambda i, j: (i, 0)),
            pl.BlockSpec((1, D_MODEL), lambda i, j: (0, 0)),
            pl.BlockSpec((D_MODEL, FFN_TF), lambda i, j: (0, j)),
            pl.BlockSpec((D_MODEL, FFN_TF), lambda i, j: (0, j)),
            pl.BlockSpec((FFN_TF, D_MODEL), lambda i, j: (j, 0)),
            pl.BlockSpec((1, D_MODEL), lambda i, j: (0, 0)),
        ],
        out_specs=pl.BlockSpec((FFN_TM, D_MODEL), lambda i, j: (i, 0)),
        out_shape=jax.ShapeDtypeStruct((n, D_MODEL), jnp.float32),
        scratch_shapes=[pltpu.VMEM((FFN_TM, D_MODEL), jnp.bfloat16)],
        compiler_params=pltpu.CompilerParams(
            dimension_semantics=("arbitrary", "arbitrary"),
            vmem_limit_bytes=VMEM_LIMIT),
        name="ffn_final" if final_norm else "ffn",
    )(x, g, wg, wu, wd, fg)


def _in_proj_kernel(x_ref, g_ref, w_ref, s_ref, o_ref, h_ref):
    @pl.when(pl.program_id(1) == 0)
    def _():
        h_ref[...] = _rms(x_ref[...], g_ref[...]).astype(jnp.bfloat16)

    acc = jnp.dot(h_ref[...], w_ref[...], preferred_element_type=jnp.float32)
    o_ref[...] = (acc * s_ref[...]).astype(jnp.bfloat16)


def _in_proj(x, g, w, col_scale):
    n = x.shape[0]
    grid = (n // PROJ_TM, PROJ_W // PROJ_TN)
    return pl.pallas_call(
        _in_proj_kernel,
        grid=grid,
        in_specs=[
            pl.BlockSpec((PROJ_TM, D_MODEL), lambda i, j: (i, 0)),
            pl.BlockSpec((1, D_MODEL), lambda i, j: (0, 0)),
            pl.BlockSpec((D_MODEL, PROJ_TN), lambda i, j: (0, j)),
            pl.BlockSpec((1, PROJ_TN), lambda i, j: (0, j)),
        ],
        out_specs=pl.BlockSpec((PROJ_TM, PROJ_TN), lambda i, j: (i, j)),
        out_shape=jax.ShapeDtypeStruct((n, PROJ_W), jnp.bfloat16),
        scratch_shapes=[pltpu.VMEM((PROJ_TM, D_MODEL), jnp.bfloat16)],
        compiler_params=pltpu.CompilerParams(
            dimension_semantics=("arbitrary", "arbitrary"),
            vmem_limit_bytes=VMEM_LIMIT),
        name="in_proj",
    )(x, g, w, col_scale)


def _swa_kernel(slopes_ref, sinks_ref, q_ref, k_ref, v_ref, o_ref, bias_ref):
    g = pl.program_id(1)
    n_heads = N_SWA_HEADS // N_SWA_KV
    nb = q_ref.shape[1] // BLOCK
    row = lax.broadcasted_iota(jnp.int32, (BLOCK, BLOCK), 0)
    col = lax.broadcasted_iota(jnp.int32, (BLOCK, BLOCK), 1)
    low_half = col < HEAD_DIM
    dist_p = (row + BLOCK - col).astype(jnp.float32)
    dist_c = (row - col).astype(jnp.float32)
    for jh in range(n_heads):
        slope2 = slopes_ref[g * n_heads + jh] * LOG2E
        bias_ref[jh, :, :BLOCK] = jnp.where(col > row, -slope2 * dist_p, -jnp.inf)
        bias_ref[jh, :, BLOCK:] = jnp.where(col <= row, -slope2 * dist_c, -jnp.inf)
    ones = jnp.ones((BLOCK, LANES), jnp.bfloat16)
    prev_cols = lax.broadcasted_iota(jnp.int32, (BLOCK, 2 * BLOCK), 1) < BLOCK

    def group(gi, _):
        for t in range(SWA_QB):
            qb = gi * SWA_QB + t
            q0 = pl.multiple_of(qb * BLOCK, BLOCK)
            p0 = pl.multiple_of(jnp.maximum(qb - 1, 0) * BLOCK, BLOCK)
            kk = jnp.concatenate([k_ref[0, pl.ds(p0, BLOCK), :],
                                  k_ref[0, pl.ds(q0, BLOCK), :]], axis=0)
            vv = jnp.concatenate(
                [jnp.concatenate([v_ref[0, pl.ds(p0, BLOCK), :], ones], axis=1),
                 jnp.concatenate([v_ref[0, pl.ds(q0, BLOCK), :], ones], axis=1)], axis=0)
            q = q_ref[0, pl.ds(q0, BLOCK), :]
            outs = []
            for jh in range(n_heads):
                sink2 = sinks_ref[g * n_heads + jh] * LOG2E
                qpair = q[:, (jh // 2) * LANES:(jh // 2 + 1) * LANES]
                keep = low_half if jh % 2 == 0 else jnp.logical_not(low_half)
                qh = jnp.where(keep, qpair, jnp.zeros_like(qpair))
                bias = bias_ref[jh]
                if t == 0:
                    bias = jnp.where(jnp.logical_and(prev_cols, qb == 0), -jnp.inf, bias)
                s = lax.dot_general(qh, kk, _NT, preferred_element_type=jnp.float32) + bias
                m = jnp.maximum(jnp.max(s, axis=-1, keepdims=True), sink2)
                e = jnp.exp2(s - m).astype(jnp.bfloat16)
                pv = jnp.dot(e, vv, preferred_element_type=jnp.float32)
                outs.append(pv[:, :LANES] / (pv[:, LANES:] + jnp.exp2(sink2 - m)))
            o_ref[0, pl.ds(q0, BLOCK), :] = jnp.concatenate(
                [jnp.where(low_half, outs[0], outs[1]), jnp.where(low_half, outs[2], outs[3])],
                axis=1)
        return 0

    lax.fori_loop(0, nb // SWA_QB, group, 0)


def _swa(proj, slopes, sinks):
    b, s, _ = proj.shape
    kblk, vblk = COL_KA // LANES, COL_VA // LANES
    smem = pl.BlockSpec(memory_space=pltpu.SMEM)
    return pl.pallas_call(
        _swa_kernel,
        grid=(b, N_SWA_KV),
        in_specs=[
            smem, smem,
            pl.BlockSpec((1, s, 2 * LANES), lambda i, g: (i, 0, g)),
            pl.BlockSpec((1, s, LANES), lambda i, g: (i, 0, kblk + g)),
            pl.BlockSpec((1, s, LANES), lambda i, g: (i, 0, vblk + g)),
        ],
        out_specs=pl.BlockSpec((1, s, 2 * LANES), lambda i, g: (i, 0, g)),
        out_shape=jax.ShapeDtypeStruct((b, s, SWA_Q), jnp.float32),
        scratch_shapes=[pltpu.VMEM((N_SWA_HEADS // N_SWA_KV, BLOCK, 2 * BLOCK), jnp.float32)],
        compiler_params=pltpu.CompilerParams(
            dimension_semantics=("arbitrary", "arbitrary")),
        name="swa",
    )(slopes, sinks, proj, proj, proj)


def _sb_kernel(q_ref, k_ref, v_ref, o_ref, carry_ref, acc_ref):
    s_len = q_ref.shape[1]
    nq = s_len // SB_TQ
    n_sub = SB_TQ // SB_TK
    r = lax.broadcasted_iota(jnp.int32, (2 * SB_TK, 2 * SB_TK), 0)
    c = lax.broadcasted_iota(jnp.int32, (2 * SB_TK, 2 * SB_TK), 1)
    tri = jnp.where(jnp.logical_or(c >= SB_TK, (r % SB_TK) >= c), 1.0, 0.0).astype(jnp.bfloat16)
    low_half = lax.broadcasted_iota(jnp.int32, (SB_TK, LANES), 1) < HEAD_DIM
    strict_lower = (lax.broadcasted_iota(jnp.int32, (SB_TK, SB_TK), 1)
                    < lax.broadcasted_iota(jnp.int32, (SB_TK, SB_TK), 0))

    def q_block(qi, _):
        q0 = pl.multiple_of(qi * SB_TQ, SB_TQ)
        carry_ref[...] = jnp.zeros_like(carry_ref)
        acc_ref[...] = jnp.zeros_like(acc_ref)

        def k_block(k0, r0, diag):
            kblk = k_ref[0, pl.ds(k0, SB_TK), :]
            vblk = v_ref[0, pl.ds(k0, SB_TK), :]
            zero = jnp.zeros_like(kblk)
            kk = jnp.concatenate([jnp.where(low_half, kblk, zero),
                                  jnp.where(low_half, zero, kblk)], axis=0)
            vv = jnp.concatenate([jnp.where(low_half, vblk, zero),
                                  jnp.where(low_half, zero, vblk)], axis=0)
            def mask_top(x):
                top = jnp.where(strict_lower, x[:SB_TK], 0.0)
                return top if r0 == SB_TQ - SB_TK else jnp.concatenate([top, x[SB_TK:]], axis=0)

            q = q_ref[0, pl.ds(q0 + r0, SB_TQ - r0), :]
            u2 = lax.dot_general(q, kk, _NT, preferred_element_type=jnp.float32)
            a_heads = []
            for hd in range(2):
                u = u2[:, hd * SB_TK:(hd + 1) * SB_TK]
                sp = jnp.maximum(u, 0.0) + jnp.log(1.0 + jnp.exp2(-jnp.abs(u))) * LOG2E
                if diag:
                    sp = mask_top(sp)
                hi = sp.astype(jnp.bfloat16)
                lo = (sp - hi.astype(jnp.float32)).astype(jnp.bfloat16)
                sums = jnp.dot(jnp.concatenate([hi, lo], axis=1), tri,
                               preferred_element_type=jnp.float32)
                carry = carry_ref[hd, r0:, :]
                a = jnp.exp2(u - (carry + sums[:, :SB_TK]))
                if diag:
                    a = mask_top(a)
                carry_ref[hd, r0:, :] = carry + sums[:, SB_TK:]
                a_heads.append(a.astype(jnp.bfloat16))
            acc_ref[r0:, :] += jnp.dot(jnp.concatenate(a_heads, axis=1), vv,
                                       preferred_element_type=jnp.float32)

        for d in range(n_sub - 1, -1, -1):
            k_block(pl.multiple_of(q0 + d * SB_TK, SB_TK), d * SB_TK, True)

        def chunk(i, _):
            base = q0 - (i + 1) * SB_TQ
            for d in range(n_sub - 1, -1, -1):
                k_block(pl.multiple_of(base + d * SB_TK, SB_TK), 0, False)
            return 0

        lax.fori_loop(0, qi, chunk, 0)
        o_ref[0, pl.ds(q0, SB_TQ), :] = acc_ref[...]
        return 0

    lax.fori_loop(0, nq, q_block, 0)


def _sb(proj):
    b, s, _ = proj.shape
    npair = N_SB_HEADS // 2
    qblk, kblk, vblk = COL_QB // LANES, COL_KB // LANES, COL_VB // LANES
    return pl.pallas_call(
        _sb_kernel,
        grid=(b, npair),
        in_specs=[
            pl.BlockSpec((1, s, LANES), lambda i, p: (i, 0, qblk + p)),
            pl.BlockSpec((1, s, LANES), lambda i, p: (i, 0, kblk + p)),
            pl.BlockSpec((1, s, LANES), lambda i, p: (i, 0, vblk + p)),
        ],
        out_specs=pl.BlockSpec((1, s, LANES), lambda i, p: (i, 0, p)),
        out_shape=jax.ShapeDtypeStruct((b, s, SB_W), jnp.float32),
        scratch_shapes=[pltpu.VMEM((2, SB_TQ, SB_TK), jnp.float32),
                        pltpu.VMEM((SB_TQ, LANES), jnp.float32)],
        compiler_params=pltpu.CompilerParams(
            dimension_semantics=("arbitrary", "arbitrary")),
        name="sb",
    )(proj, proj, proj)


def _out_proj_kernel(x_ref, ya_ref, yb_ref, ga_ref, gb_ref, w_ref, o_ref):
    y = jnp.concatenate([_rms(ya_ref[...], ga_ref[...]), _rms(yb_ref[...], gb_ref[...])],
                        axis=1).astype(jnp.bfloat16)
    o_ref[...] = x_ref[...] + jnp.dot(y, w_ref[...], preferred_element_type=jnp.float32)


def _out_proj(x, ya, yb, ga, gb, w):
    n = x.shape[0]
    return pl.pallas_call(
        _out_proj_kernel,
        grid=(n // PROJ_TM,),
        in_specs=[
            pl.BlockSpec((PROJ_TM, D_MODEL), lambda i: (i, 0)),
            pl.BlockSpec((PROJ_TM, SWA_Q), lambda i: (i, 0)),
            pl.BlockSpec((PROJ_TM, SB_W), lambda i: (i, 0)),
            pl.BlockSpec((1, SWA_Q), lambda i: (0, 0)),
            pl.BlockSpec((1, SB_W), lambda i: (0, 0)),
            pl.BlockSpec((SWA_Q + SB_W, D_MODEL), lambda i: (0, 0)),
        ],
        out_specs=pl.BlockSpec((PROJ_TM, D_MODEL), lambda i: (i, 0)),
        out_shape=jax.ShapeDtypeStruct((n, D_MODEL), jnp.float32),
        compiler_params=pltpu.CompilerParams(
            dimension_semantics=("arbitrary",),
            vmem_limit_bytes=VMEM_LIMIT),
        name="out_proj",
    )(x, ya, yb, ga, gb, w)


def _prep_ffn_weights(w_gate, w_up, w_down):
    pad = D_FF_PAD - D_FF
    wg = jnp.pad(w_gate, ((0, 0), (0, pad))).astype(jnp.bfloat16)
    wu = jnp.pad(w_up, ((0, 0), (0, pad))).astype(jnp.bfloat16)
    wd = jnp.pad(w_down, ((0, pad), (0, 0))).astype(jnp.bfloat16)
    return wg, wu, wd


def _prep_w_in(w):
    def twice(cols):
        c = cols.reshape(D_MODEL, N_SWA_KV, 1, HEAD_DIM)
        return jnp.broadcast_to(c, (D_MODEL, N_SWA_KV, 2, HEAD_DIM)).reshape(D_MODEL, -1)
    o1 = SWA_Q
    o2 = o1 + N_SWA_KV * HEAD_DIM
    o3 = o2 + N_SWA_KV * HEAD_DIM
    return jnp.concatenate([w[:, :o1], twice(w[:, o1:o2]), twice(w[:, o2:o3]), w[:, o3:]],
                           axis=1).astype(jnp.bfloat16)


def _proj_col_scale():
    s = jnp.ones((PROJ_W,), jnp.float32)
    s = s.at[:SWA_Q].set(HEAD_DIM ** -0.5 * LOG2E)
    s = s.at[COL_QB:COL_KB].set(HEAD_DIM ** -0.5 * LOG2E)
    return s.reshape(1, PROJ_W)


def kernel(x, ffn1_norm, ffn1_w_gate, ffn1_w_up, ffn1_w_down, mix_norm, w_in, swa_sinks,
           swa_out_norm, sb_out_norm, w_out, ffn2_norm, ffn2_w_gate, ffn2_w_up, ffn2_w_down,
           final_norm):
    b, s, d = x.shape
    depth = ffn1_norm.shape[0]
    xf = x.reshape(b * s, d)
    i = jnp.arange(1, N_SWA_HEADS + 1, dtype=jnp.float32)
    slopes = jnp.exp2(-8.0 * i / N_SWA_HEADS)
    fg = final_norm.reshape(1, d)
    col_scale = _proj_col_scale()
    for l in range(depth):
        wg, wu, wd = _prep_ffn_weights(ffn1_w_gate[l], ffn1_w_up[l], ffn1_w_down[l])
        xf = _ffn(xf, ffn1_norm[l].reshape(1, d), wg, wu, wd, fg, final_norm=False)

        proj = _in_proj(xf, mix_norm[l].reshape(1, d), _prep_w_in(w_in[l]), col_scale)
        proj = proj.reshape(b, s, PROJ_W)
        ya = _swa(proj, slopes, swa_sinks[l])
        yb = _sb(proj)
        xf = _out_proj(xf, ya.reshape(b * s, SWA_Q), yb.reshape(b * s, SB_W),
                       swa_out_norm[l].reshape(1, SWA_Q), sb_out_norm[l].reshape(1, SB_W),
                       w_out[l].astype(jnp.bfloat16))

        wg, wu, wd = _prep_ffn_weights(ffn2_w_gate[l], ffn2_w_up[l], ffn2_w_down[l])
        xf = _ffn(xf, ffn2_norm[l].reshape(1, d), wg, wu, wd, fg,
                  final_norm=(l == depth - 1))
    return xf.reshape(b, s, d)
```

```python
import functools
import math

import jax
import jax.numpy as jnp
from jax import lax
from jax.experimental import pallas as pl
from jax.experimental.pallas import tpu as pltpu

D_MODEL = 2048
HEAD_DIM = 64
N_SWA_HEADS = 16
N_SWA_KV = 4
N_SB_HEADS = 16
WINDOW = 128
BLOCK = 128
D_FF = 5504
EPS = 1e-6
LOG2E = math.log2(math.e)

LANES = 128
D_FF_PAD = 5632
FFN_TM = 1024
FFN_TF = 512
PROJ_TM = 512
PROJ_TN = 1024
SWA_QB = 4
SB_TQ = 512
SB_TK = 128
VMEM_LIMIT = 56 * 1024 * 1024

SWA_Q = N_SWA_HEADS * HEAD_DIM
SB_W = N_SB_HEADS * HEAD_DIM
COL_KA = SWA_Q
COL_VA = COL_KA + 2 * N_SWA_KV * HEAD_DIM
COL_QB = COL_VA + 2 * N_SWA_KV * HEAD_DIM
COL_KB = COL_QB + SB_W
COL_VB = COL_KB + SB_W
PROJ_W = COL_VB + SB_W

_NT = (((1,), (1,)), ((), ()))


def _rms(x, g):
    return x * lax.rsqrt(jnp.mean(x * x, axis=-1, keepdims=True) + EPS) * g


def _ffn_kernel(x_ref, g_ref, wg_ref, wu_ref, wd_ref, fg_ref, o_ref, h_ref, *, final_norm):
    j = pl.program_id(1)

    @pl.when(j == 0)
    def _():
        x = x_ref[...]
        h_ref[...] = _rms(x, g_ref[...]).astype(jnp.bfloat16)
        o_ref[...] = x

    h = h_ref[...]
    gate = jnp.dot(h, wg_ref[...], preferred_element_type=jnp.float32)
    up = jnp.dot(h, wu_ref[...], preferred_element_type=jnp.float32)
    act = (gate * jax.nn.sigmoid(gate) * up * 0.5).astype(jnp.bfloat16)
    o_ref[...] += jnp.dot(act, wd_ref[...], preferred_element_type=jnp.float32)

    if final_norm:
        @pl.when(j == pl.num_programs(1) - 1)
        def _():
            o_ref[...] = _rms(o_ref[...], fg_ref[...])


def _ffn(x, g, wg, wu, wd, fg, final_norm):
    n = x.shape[0]
    grid = (n // FFN_TM, D_FF_PAD // FFN_TF)
    return pl.pallas_call(
        functools.partial(_ffn_kernel, final_norm=final_norm),
        grid=grid,
        in_specs=[
            pl.BlockSpec((FFN_TM, D_MODEL), lambda i, j: (i, 0), pipeline_mode=pl.Buffered(1)),
            pl.BlockSpec((1, D_MODEL), lambda i, j: (0, 0)),
            pl.BlockSpec((D_MODEL, FFN_TF), lambda i, j: (0, j)),
            pl.BlockSpec((D_MODEL, FFN_TF), lambda i, j: (0, j)),
            pl.BlockSpec((FFN_TF, D_MODEL), lambda i, j: (j, 0)),
            pl.BlockSpec((1, D_MODEL), lambda i, j: (0, 0)),
        ],
        out_specs=pl.BlockSpec((FFN_TM, D_MODEL), lambda i, j: (i, 0)),
        out_shape=jax.ShapeDtypeStruct((n, D_MODEL), jnp.float32),
        scratch_shapes=[pltpu.VMEM((FFN_TM, D_MODEL), jnp.bfloat16)],
        compiler_params=pltpu.CompilerParams(
            dimension_semantics=("arbitrary", "arbitrary"),
            vmem_limit_bytes=VMEM_LIMIT),
        name="ffn_final" if final_norm else "ffn",
    )(x, g, wg, wu, wd, fg)


def _in_proj_kernel(x_ref, g_ref, w_ref, s_ref, o_ref, h_ref):
    @pl.when(pl.program_id(1) == 0)
    def _():
        h_ref[...] = _rms(x_ref[...], g_ref[...]).astype(jnp.bfloat16)

    acc = jnp.dot(h_ref[...], w_ref[...], preferred_element_type=jnp.float32)
    o_ref[...] = (acc * s_ref[...]).astype(jnp.bfloat16)


def _in_proj(x, g, w, col_scale):
    n = x.shape[0]
    grid = (n // PROJ_TM, PROJ_W // PROJ_TN)
    return pl.pallas_call(
        _in_proj_kernel,
        grid=grid,
        in_specs=[
            pl.BlockSpec((PROJ_TM, D_MODEL), lambda i, j: (i, 0)),
            pl.BlockSpec((1, D_MODEL), lambda i, j: (0, 0)),
            pl.BlockSpec((D_MODEL, PROJ_TN), lambda i, j: (0, j)),
            pl.BlockSpec((1, PROJ_TN), lambda i, j: (0, j)),
        ],
        out_specs=pl.BlockSpec((PROJ_TM, PROJ_TN), lambda i, j: (i, j)),
        out_shape=jax.ShapeDtypeStruct((n, PROJ_W), jnp.bfloat16),
        scratch_shapes=[pltpu.VMEM((PROJ_TM, D_MODEL), jnp.bfloat16)],
        compiler_params=pltpu.CompilerParams(
            dimension_semantics=("arbitrary", "arbitrary"),
            vmem_limit_bytes=VMEM_LIMIT),
        name="in_proj",
    )(x, g, w, col_scale)


def _swa_kernel(slopes_ref, sinks_ref, q_ref, k_ref, v_ref, o_ref, bias_ref):
    g = pl.program_id(1)
    n_heads = N_SWA_HEADS // N_SWA_KV
    nb = q_ref.shape[1] // BLOCK
    row = lax.broadcasted_iota(jnp.int32, (BLOCK, BLOCK), 0)
    col = lax.broadcasted_iota(jnp.int32, (BLOCK, BLOCK), 1)
    low_half = col < HEAD_DIM
    dist_p = (row + BLOCK - col).astype(jnp.float32)
    dist_c = (row - col).astype(jnp.float32)
    for jh in range(n_heads):
        slope2 = slopes_ref[g * n_heads + jh] * LOG2E
        bias_ref[jh, :, :BLOCK] = jnp.where(col > row, -slope2 * dist_p, -jnp.inf)
        bias_ref[jh, :, BLOCK:] = jnp.where(col <= row, -slope2 * dist_c, -jnp.inf)
    ones = jnp.ones((BLOCK, LANES), jnp.bfloat16)
    prev_cols = lax.broadcasted_iota(jnp.int32, (BLOCK, 2 * BLOCK), 1) < BLOCK

    def group(gi, _):
        for t in range(SWA_QB):
            qb = gi * SWA_QB + t
            q0 = pl.multiple_of(qb * BLOCK, BLOCK)
            p0 = pl.multiple_of(jnp.maximum(qb - 1, 0) * BLOCK, BLOCK)
            kk = jnp.concatenate([k_ref[0, pl.ds(p0, BLOCK), :],
                                  k_ref[0, pl.ds(q0, BLOCK), :]], axis=0)
            vv = jnp.concatenate(
                [jnp.concatenate([v_ref[0, pl.ds(p0, BLOCK), :], ones], axis=1),
                 jnp.concatenate([v_ref[0, pl.ds(q0, BLOCK), :], ones], axis=1)], axis=0)
            q = q_ref[0, pl.ds(q0, BLOCK), :]
            outs = []
            for jh in range(n_heads):
                sink2 = sinks_ref[g * n_heads + jh] * LOG2E
                qpair = q[:, (jh // 2) * LANES:(jh // 2 + 1) * LANES]
                keep = low_half if jh % 2 == 0 else jnp.logical_not(low_half)
                qh = jnp.where(keep, qpair, jnp.zeros_like(qpair))
                bias = bias_ref[jh]
                if t == 0:
                    bias = jnp.where(jnp.logical_and(prev_cols, qb == 0), -jnp.inf, bias)
                s = lax.dot_general(qh, kk, _NT, preferred_element_type=jnp.float32) + bias
                m = jnp.maximum(jnp.max(s, axis=-1, keepdims=True), sink2)
                e = jnp.exp2(s - m).astype(jnp.bfloat16)
                pv = jnp.dot(e, vv, preferred_element_type=jnp.float32)
                outs.append(pv[:, :LANES] / (pv[:, LANES:] + jnp.exp2(sink2 - m)))
            o_ref[0, pl.ds(q0, BLOCK), :] = jnp.concatenate(
                [jnp.where(low_half, outs[0], outs[1]), jnp.where(low_half, outs[2], outs[3])],
                axis=1)
        return 0

    lax.fori_loop(0, nb // SWA_QB, group, 0)


def _swa(proj, slopes, sinks):
    b, s, _ = proj.shape
    kblk, vblk = COL_KA // LANES, COL_VA // LANES
    smem = pl.BlockSpec(memory_space=pltpu.SMEM)
    return pl.pallas_call(
        _swa_kernel,
        grid=(b, N_SWA_KV),
        in_specs=[
            smem, smem,
            pl.BlockSpec((1, s, 2 * LANES), lambda i, g: (i, 0, g)),
            pl.BlockSpec((1, s, LANES), lambda i, g: (i, 0, kblk + g)),
            pl.BlockSpec((1, s, LANES), lambda i, g: (i, 0, vblk + g)),
        ],
        out_specs=pl.BlockSpec((1, s, 2 * LANES), lambda i, g: (i, 0, g)),
        out_shape=jax.ShapeDtypeStruct((b, s, SWA_Q), jnp.float32),
        scratch_shapes=[pltpu.VMEM((N_SWA_HEADS // N_SWA_KV, BLOCK, 2 * BLOCK), jnp.float32)],
        compiler_params=pltpu.CompilerParams(
            dimension_semantics=("arbitrary", "arbitrary")),
        name="swa",
    )(slopes, sinks, proj, proj, proj)


def _sb_kernel(q_ref, k_ref, v_ref, o_ref, carry_ref, acc_ref):
    s_len = q_ref.shape[1]
    nq = s_len // SB_TQ
    n_sub = SB_TQ // SB_TK
    r = lax.broadcasted_iota(jnp.int32, (2 * SB_TK, 2 * SB_TK), 0)
    c = lax.broadcasted_iota(jnp.int32, (2 * SB_TK, 2 * SB_TK), 1)
    tri = jnp.where(jnp.logical_or(c >= SB_TK, (r % SB_TK) >= c), 1.0, 0.0).astype(jnp.bfloat16)
    low_half = lax.broadcasted_iota(jnp.int32, (SB_TK, LANES), 1) < HEAD_DIM
    strict_lower = (lax.broadcasted_iota(jnp.int32, (SB_TK, SB_TK), 1)
                    < lax.broadcasted_iota(jnp.int32, (SB_TK, SB_TK), 0))

    def q_block(qi, _):
        q0 = pl.multiple_of(qi * SB_TQ, SB_TQ)
        carry_ref[...] = jnp.zeros_like(carry_ref)
        acc_ref[...] = jnp.zeros_like(acc_ref)

        def k_block(k0, r0, diag):
            kblk = k_ref[0, pl.ds(k0, SB_TK), :]
            vblk = v_ref[0, pl.ds(k0, SB_TK), :]
            zero = jnp.zeros_like(kblk)
            kk = jnp.concatenate([jnp.where(low_half, kblk, zero),
                                  jnp.where(low_half, zero, kblk)], axis=0)
            vv = jnp.concatenate([jnp.where(low_half, vblk, zero),
                                  jnp.where(low_half, zero, vblk)], axis=0)
            def mask_top(x):
                top = jnp.where(strict_lower, x[:SB_TK], 0.0)
                return top if r0 == SB_TQ - SB_TK else jnp.concatenate([top, x[SB_TK:]], axis=0)

            q = q_ref[0, pl.ds(q0 + r0, SB_TQ - r0), :]
            u2 = lax.dot_general(q, kk, _NT, preferred_element_type=jnp.float32)
            a_heads = []
            for hd in range(2):
                u = u2[:, hd * SB_TK:(hd + 1) * SB_TK]
                sp = jnp.maximum(u, 0.0) + jnp.log(1.0 + jnp.exp2(-jnp.abs(u))) * LOG2E
                if diag:
                    sp = mask_top(sp)
                hi = sp.astype(jnp.bfloat16)
                lo = (sp - hi.astype(jnp.float32)).astype(jnp.bfloat16)
                sums = jnp.dot(jnp.concatenate([hi, lo], axis=1), tri,
                               preferred_element_type=jnp.float32)
                carry = carry_ref[hd, r0:, :]
                a = jnp.exp2(u - (carry + sums[:, :SB_TK]))
                if diag:
                    a = mask_top(a)
                carry_ref[hd, r0:, :] = carry + sums[:, SB_TK:]
                a_heads.append(a.astype(jnp.bfloat16))
            acc_ref[r0:, :] += jnp.dot(jnp.concatenate(a_heads, axis=1), vv,
                                       preferred_element_type=jnp.float32)

        for d in range(n_sub - 1, -1, -1):
            k_block(pl.multiple_of(q0 + d * SB_TK, SB_TK), d * SB_TK, True)

        def chunk(i, _):
            base = q0 - (i + 1) * SB_TQ
            for d in range(n_sub - 1, -1, -1):
                k_block(pl.multiple_of(base + d * SB_TK, SB_TK), 0, False)
            return 0

        lax.fori_loop(0, qi, chunk, 0)
        o_ref[0, pl.ds(q0, SB_TQ), :] = acc_ref[...]
        return 0

    lax.fori_loop(0, nq, q_block, 0)


def _sb(proj):
    b, s, _ = proj.shape
    npair = N_SB_HEADS // 2
    qblk, kblk, vblk = COL_QB // LANES, COL_KB // LANES, COL_VB // LANES
    return pl.pallas_call(
        _sb_kernel,
        grid=(b, npair),
        in_specs=[
            pl.BlockSpec((1, s, LANES), lambda i, p: (i, 0, qblk + p)),
            pl.BlockSpec((1, s, LANES), lambda i, p: (i, 0, kblk + p)),
            pl.BlockSpec((1, s, LANES), lambda i, p: (i, 0, vblk + p)),
        ],
        out_specs=pl.BlockSpec((1, s, LANES), lambda i, p: (i, 0, p)),
        out_shape=jax.ShapeDtypeStruct((b, s, SB_W), jnp.float32),
        scratch_shapes=[pltpu.VMEM((2, SB_TQ, SB_TK), jnp.float32),
                        pltpu.VMEM((SB_TQ, LANES), jnp.float32)],
        compiler_params=pltpu.CompilerParams(
            dimension_semantics=("arbitrary", "arbitrary")),
        name="sb",
    )(proj, proj, proj)


def _out_proj_kernel(x_ref, ya_ref, yb_ref, ga_ref, gb_ref, w_ref, o_ref):
    y = jnp.concatenate([_rms(ya_ref[...], ga_ref[...]), _rms(yb_ref[...], gb_ref[...])],
                        axis=1).astype(jnp.bfloat16)
    o_ref[...] = x_ref[...] + jnp.dot(y, w_ref[...], preferred_element_type=jnp.float32)


def _out_proj(x, ya, yb, ga, gb, w):
    n = x.shape[0]
    return pl.pallas_call(
        _out_proj_kernel,
        grid=(n // PROJ_TM,),
        in_specs=[
            pl.BlockSpec((PROJ_TM, D_MODEL), lambda i: (i, 0)),
            pl.BlockSpec((PROJ_TM, SWA_Q), lambda i: (i, 0)),
            pl.BlockSpec((PROJ_TM, SB_W), lambda i: (i, 0)),
            pl.BlockSpec((1, SWA_Q), lambda i: (0, 0)),
            pl.BlockSpec((1, SB_W), lambda i: (0, 0)),
            pl.BlockSpec((SWA_Q + SB_W, D_MODEL), lambda i: (0, 0)),
        ],
        out_specs=pl.BlockSpec((PROJ_TM, D_MODEL), lambda i: (i, 0)),
        out_shape=jax.ShapeDtypeStruct((n, D_MODEL), jnp.float32),
        compiler_params=pltpu.CompilerParams(
            dimension_semantics=("arbitrary",),
            vmem_limit_bytes=VMEM_LIMIT),
        name="out_proj",
    )(x, ya, yb, ga, gb, w)


def _prep_ffn_weights(w_gate, w_up, w_down):
    pad = D_FF_PAD - D_FF
    wg = jnp.pad(w_gate, ((0, 0), (0, pad))).astype(jnp.bfloat16)
    wu = jnp.pad(w_up, ((0, 0), (0, pad))).astype(jnp.bfloat16)
    wd = jnp.pad(w_down, ((0, pad), (0, 0))).astype(jnp.bfloat16)
    return wg, wu, wd


def _prep_w_in(w):
    def twice(cols):
        c = cols.reshape(D_MODEL, N_SWA_KV, 1, HEAD_DIM)
        return jnp.broadcast_to(c, (D_MODEL, N_SWA_KV, 2, HEAD_DIM)).reshape(D_MODEL, -1)
    o1 = SWA_Q
    o2 = o1 + N_SWA_KV * HEAD_DIM
    o3 = o2 + N_SWA_KV * HEAD_DIM
    return jnp.concatenate([w[:, :o1], twice(w[:, o1:o2]), twice(w[:, o2:o3]), w[:, o3:]],
                           axis=1).astype(jnp.bfloat16)


def _proj_col_scale():
    s = jnp.ones((PROJ_W,), jnp.float32)
    s = s.at[:SWA_Q].set(HEAD_DIM ** -0.5 * LOG2E)
    s = s.at[COL_QB:COL_KB].set(HEAD_DIM ** -0.5 * LOG2E)
    return s.reshape(1, PROJ_W)


def kernel(x, ffn1_norm, ffn1_w_gate, ffn1_w_up, ffn1_w_down, mix_norm, w_in, swa_sinks,
           swa_out_norm, sb_out_norm, w_out, ffn2_norm, ffn2_w_gate, ffn2_w_up, ffn2_w_down,
           final_norm):
    b, s, d = x.shape
    depth = ffn1_norm.shape[0]
    xf = x.reshape(b * s, d)
    i = jnp.arange(1, N_SWA_HEADS + 1, dtype=jnp.float32)
    slopes = jnp.exp2(-8.0 * i / N_SWA_HEADS)
    fg = final_norm.reshape(1, d)
    col_scale = _proj_col_scale()
    for l in range(depth):
        wg, wu, wd = _prep_ffn_weights(ffn1_w_gate[l], ffn1_w_up[l], ffn1_w_down[l])
        xf = _ffn(xf, ffn1_norm[l].reshape(1, d), wg, wu, wd, fg, final_norm=False)

        proj = _in_proj(xf, mix_norm[l].reshape(1, d), _prep_w_in(w_in[l]), col_scale)
        proj = proj.reshape(b, s, PROJ_W)
        ya = _swa(proj, slopes, swa_sinks[l])
        yb = _sb(proj)
        xf = _out_proj(xf, ya.reshape(b * s, SWA_Q), yb.reshape(b * s, SB_W),
                       swa_out_norm[l].reshape(1, SWA_Q), sb_out_norm[l].reshape(1, SB_W),
                       w_out[l].astype(jnp.bfloat16))

        wg, wu, wd = _prep_ffn_weights(ffn2_w_gate[l], ffn2_w_up[l], ffn2_w_down[l])
        xf = _ffn(xf, ffn2_norm[l].reshape(1, d), wg, wu, wd, fg,
                  final_norm=(l == depth - 1))
    return xf.reshape(b, s, d)
```

```python
import functools
import math

import jax
import jax.numpy as jnp
from jax import lax
from jax.experimental import pallas as pl
from jax.experimental.pallas import tpu as pltpu

D_MODEL = 2048
HEAD_DIM = 64
N_SWA_HEADS = 16
N_SWA_KV = 4
N_SB_HEADS = 16
WINDOW = 128
BLOCK = 128
D_FF = 5504
EPS = 1e-6
LOG2E = math.log2(math.e)

LANES = 128
D_FF_PAD = 5632
FFN_TM = 1024
FFN_TF = 512
PROJ_TM = 512
PROJ_TN = 1024
SWA_QB = 4
SB_TQ = 512
SB_TK = 128
VMEM_LIMIT = 56 * 1024 * 1024

SWA_Q = N_SWA_HEADS * HEAD_DIM
SB_W = N_SB_HEADS * HEAD_DIM
COL_KA = SWA_Q
COL_VA = COL_KA + 2 * N_SWA_KV * HEAD_DIM
COL_QB = COL_VA + 2 * N_SWA_KV * HEAD_DIM
COL_KB = COL_QB + SB_W
COL_VB = COL_KB + SB_W
PROJ_W = COL_VB + SB_W

_NT = (((1,), (1,)), ((), ()))


def _rms(x, g):
    return x * lax.rsqrt(jnp.mean(x * x, axis=-1, keepdims=True) + EPS) * g


def _ffn_kernel(x_ref, g_ref, wg_ref, wu_ref, wd_ref, fg_ref, o_ref, h_ref, *, final_norm):
    j = pl.program_id(1)

    @pl.when(j == 0)
    def _():
        x = x_ref[...]
        h_ref[...] = _rms(x, g_ref[...]).astype(jnp.bfloat16)
        o_ref[...] = x

    h = h_ref[...]
    gate = jnp.dot(h, wg_ref[...], preferred_element_type=jnp.float32)
    up = jnp.dot(h, wu_ref[...], preferred_element_type=jnp.float32)
    act = (gate * jax.nn.sigmoid(gate) * up * 0.5).astype(jnp.bfloat16)
    o_ref[...] += jnp.dot(act, wd_ref[...], preferred_element_type=jnp.float32)

    if final_norm:
        @pl.when(j == pl.num_programs(1) - 1)
        def _():
            o_ref[...] = _rms(o_ref[...], fg_ref[...])


def _ffn(x, g, wg, wu, wd, fg, final_norm):
    n = x.shape[0]
    grid = (n // FFN_TM, D_FF_PAD // FFN_TF)
    return pl.pallas_call(
        functools.partial(_ffn_kernel, final_norm=final_norm),
        grid=grid,
        in_specs=[
            pl.BlockSpec((FFN_TM, D_MODEL), lambda i, j: (i, 0)),
            pl.BlockSpec((1, D_MODEL), lambda i, j: (0, 0)),
            pl.BlockSpec((D_MODEL, FFN_TF), lambda i, j: (0, j)),
            pl.BlockSpec((D_MODEL, FFN_TF), lambda i, j: (0, j)),
            pl.BlockSpec((FFN_TF, D_MODEL), lambda i, j: (j, 0)),
            pl.BlockSpec((1, D_MODEL), lambda i, j: (0, 0)),
        ],
        out_specs=pl.BlockSpec((FFN_TM, D_MODEL), lambda i, j: (i, 0)),
        out_shape=jax.ShapeDtypeStruct((n, D_MODEL), jnp.float32),
        scratch_shapes=[pltpu.VMEM((FFN_TM, D_MODEL), jnp.bfloat16)],
        compiler_params=pltpu.CompilerParams(
            dimension_semantics=("arbitrary", "arbitrary"),
            vmem_limit_bytes=VMEM_LIMIT),
        name="ffn_final" if final_norm else "ffn",
    )(x, g, wg, wu, wd, fg)


def _in_proj_kernel(x_ref, g_ref, w_ref, s_ref, o_ref, h_ref):
    @pl.when(pl.program_id(1) == 0)
    def _():
        h_ref[...] = _rms(x_ref[...], g_ref[...]).astype(jnp.bfloat16)

    acc = jnp.dot(h_ref[...], w_ref[...], preferred_element_type=jnp.float32)
    o_ref[...] = (acc * s_ref[...]).astype(jnp.bfloat16)


def _in_proj(x, g, w, col_scale):
    n = x.shape[0]
    grid = (n // PROJ_TM, PROJ_W // PROJ_TN)
    return pl.pallas_call(
        _in_proj_kernel,
        grid=grid,
        in_specs=[
            pl.BlockSpec((PROJ_TM, D_MODEL), lambda i, j: (i, 0)),
            pl.BlockSpec((1, D_MODEL), lambda i, j: (0, 0)),
            pl.BlockSpec((D_MODEL, PROJ_TN), lambda i, j: (0, j)),
            pl.BlockSpec((1, PROJ_TN), lambda i, j: (0, j)),
        ],
        out_specs=pl.BlockSpec((PROJ_TM, PROJ_TN), lambda i, j: (i, j)),
        out_shape=jax.ShapeDtypeStruct((n, PROJ_W), jnp.bfloat16),
        scratch_shapes=[pltpu.VMEM((PROJ_TM, D_MODEL), jnp.bfloat16)],
        compiler_params=pltpu.CompilerParams(
            dimension_semantics=("arbitrary", "arbitrary"),
            vmem_limit_bytes=VMEM_LIMIT),
        name="in_proj",
    )(x, g, w, col_scale)


def _swa_kernel(slopes_ref, sinks_ref, q_ref, k_ref, v_ref, o_ref, bias_ref):
    g = pl.program_id(1)
    n_heads = N_SWA_HEADS // N_SWA_KV
    nb = q_ref.shape[1] // BLOCK
    row = lax.broadcasted_iota(jnp.int32, (BLOCK, BLOCK), 0)
    col = lax.broadcasted_iota(jnp.int32, (BLOCK, BLOCK), 1)
    low_half = col < HEAD_DIM
    dist_p = (row + BLOCK - col).astype(jnp.float32)
    dist_c = (row - col).astype(jnp.float32)
    for jh in range(n_heads):
        slope2 = slopes_ref[g * n_heads + jh] * LOG2E
        bias_ref[jh, :, :BLOCK] = jnp.where(col > row, -slope2 * dist_p, -jnp.inf)
        bias_ref[jh, :, BLOCK:] = jnp.where(col <= row, -slope2 * dist_c, -jnp.inf)
    ones = jnp.ones((BLOCK, LANES), jnp.bfloat16)
    prev_cols = lax.broadcasted_iota(jnp.int32, (BLOCK, 2 * BLOCK), 1) < BLOCK

    def group(gi, _):
        for t in range(SWA_QB):
            qb = gi * SWA_QB + t
            q0 = pl.multiple_of(qb * BLOCK, BLOCK)
            p0 = pl.multiple_of(jnp.maximum(qb - 1, 0) * BLOCK, BLOCK)
            kk = jnp.concatenate([k_ref[0, pl.ds(p0, BLOCK), :],
                                  k_ref[0, pl.ds(q0, BLOCK), :]], axis=0)
            vv = jnp.concatenate(
                [jnp.concatenate([v_ref[0, pl.ds(p0, BLOCK), :], ones], axis=1),
                 jnp.concatenate([v_ref[0, pl.ds(q0, BLOCK), :], ones], axis=1)], axis=0)
            q = q_ref[0, pl.ds(q0, BLOCK), :]
            outs = []
            for jh in range(n_heads):
                sink2 = sinks_ref[g * n_heads + jh] * LOG2E
                qpair = q[:, (jh // 2) * LANES:(jh // 2 + 1) * LANES]
                keep = low_half if jh % 2 == 0 else jnp.logical_not(low_half)
                qh = jnp.where(keep, qpair, jnp.zeros_like(qpair))
                bias = bias_ref[jh]
                if t == 0:
                    bias = jnp.where(jnp.logical_and(prev_cols, qb == 0), -jnp.inf, bias)
                s = lax.dot_general(qh, kk, _NT, preferred_element_type=jnp.float32) + bias
                m = jnp.maximum(jnp.max(s, axis=-1, keepdims=True), sink2)
                e = jnp.exp2(s - m).astype(jnp.bfloat16)
                pv = jnp.dot(e, vv, preferred_element_type=jnp.float32)
                outs.append(pv[:, :LANES] / (pv[:, LANES:] + jnp.exp2(sink2 - m)))
            o_ref[0, pl.ds(q0, BLOCK), :] = jnp.concatenate(
                [jnp.where(low_half, outs[0], outs[1]), jnp.where(low_half, outs[2], outs[3])],
                axis=1)
        return 0

    lax.fori_loop(0, nb // SWA_QB, group, 0)


def _swa(proj, slopes, sinks):
    b, s, _ = proj.shape
    kblk, vblk = COL_KA // LANES, COL_VA // LANES
    smem = pl.BlockSpec(memory_space=pltpu.SMEM)
    return pl.pallas_call(
        _swa_kernel,
        grid=(b, N_SWA_KV),
        in_specs=[
            smem, smem,
            pl.BlockSpec((1, s, 2 * LANES), lambda i, g: (i, 0, g)),
            pl.BlockSpec((1, s, LANES), lambda i, g: (i, 0, kblk + g)),
            pl.BlockSpec((1, s, LANES), lambda i, g: (i, 0, vblk + g)),
        ],
        out_specs=pl.BlockSpec((1, s, 2 * LANES), lambda i, g: (i, 0, g)),
        out_shape=jax.ShapeDtypeStruct((b, s, SWA_Q), jnp.float32),
        scratch_shapes=[pltpu.VMEM((N_SWA_HEADS // N_SWA_KV, BLOCK, 2 * BLOCK), jnp.float32)],
        compiler_params=pltpu.CompilerParams(
            dimension_semantics=("arbitrary", "arbitrary")),
        name="swa",
    )(slopes, sinks, proj, proj, proj)


def _sb_kernel(q_ref, k_ref, v_ref, o_ref, carry_ref, acc_ref):
    s_len = q_ref.shape[1]
    nq = s_len // SB_TQ
    n_sub = SB_TQ // SB_TK
    r = lax.broadcasted_iota(jnp.int32, (2 * SB_TK, 2 * SB_TK), 0)
    c = lax.broadcasted_iota(jnp.int32, (2 * SB_TK, 2 * SB_TK), 1)
    tri = jnp.where(jnp.logical_or(c >= SB_TK, (r % SB_TK) >= c), 1.0, 0.0).astype(jnp.bfloat16)
    low_half = lax.broadcasted_iota(jnp.int32, (SB_TK, LANES), 1) < HEAD_DIM
    strict_lower = (lax.broadcasted_iota(jnp.int32, (SB_TK, SB_TK), 1)
                    < lax.broadcasted_iota(jnp.int32, (SB_TK, SB_TK), 0))

    def q_block(qi, _):
        q0 = pl.multiple_of(qi * SB_TQ, SB_TQ)
        carry_ref[...] = jnp.zeros_like(carry_ref)
        acc_ref[...] = jnp.zeros_like(acc_ref)

        def k_block(k0, r0, diag):
            kblk = k_ref[0, pl.ds(k0, SB_TK), :]
            vblk = v_ref[0, pl.ds(k0, SB_TK), :]
            zero = jnp.zeros_like(kblk)
            kk = jnp.concatenate([jnp.where(low_half, kblk, zero),
                                  jnp.where(low_half, zero, kblk)], axis=0)
            vv = jnp.concatenate([jnp.where(low_half, vblk, zero),
                                  jnp.where(low_half, zero, vblk)], axis=0)
            def mask_top(x):
                top = jnp.where(strict_lower, x[:SB_TK], 0.0)
                return top if r0 == SB_TQ - SB_TK else jnp.concatenate([top, x[SB_TK:]], axis=0)

            q = q_ref[0, pl.ds(q0 + r0, SB_TQ - r0), :]
            u2 = lax.dot_general(q, kk, _NT, preferred_element_type=jnp.float32)
            a_heads = []
            for hd in range(2):
                u = u2[:, hd * SB_TK:(hd + 1) * SB_TK]
                sp = jnp.maximum(u, 0.0) + jnp.log(1.0 + jnp.exp2(-jnp.abs(u))) * LOG2E
                if diag:
                    sp = mask_top(sp)
                hi = sp.astype(jnp.bfloat16)
                lo = (sp - hi.astype(jnp.float32)).astype(jnp.bfloat16)
                sums = jnp.dot(jnp.concatenate([hi, lo], axis=1), tri,
                               preferred_element_type=jnp.float32)
                carry = carry_ref[hd, r0:, :]
                a = jnp.exp2(u - (carry + sums[:, :SB_TK]))
                if diag:
                    a = mask_top(a)
                carry_ref[hd, r0:, :] = carry + sums[:, SB_TK:]
                a_heads.append(a.astype(jnp.bfloat16))
            acc_ref[r0:, :] += jnp.dot(jnp.concatenate(a_heads, axis=1), vv,
                                       preferred_element_type=jnp.float32)

        for d in range(n_sub - 1, -1, -1):
            k_block(pl.multiple_of(q0 + d * SB_TK, SB_TK), d * SB_TK, True)

        def chunk(i, _):
            base = q0 - (i + 1) * SB_TQ
            for d in range(n_sub - 1, -1, -1):
                k_block(pl.multiple_of(base + d * SB_TK, SB_TK), 0, False)
            return 0

        lax.fori_loop(0, qi, chunk, 0)
        o_ref[0, pl.ds(q0, SB_TQ), :] = acc_ref[...]
        return 0

    lax.fori_loop(0, nq, q_block, 0)


def _sb(proj):
    b, s, _ = proj.shape
    npair = N_SB_HEADS // 2
    qblk, kblk, vblk = COL_QB // LANES, COL_KB // LANES, COL_VB // LANES
    return pl.pallas_call(
        _sb_kernel,
        grid=(b, npair),
        in_specs=[
            pl.BlockSpec((1, s, LANES), lambda i, p: (i, 0, qblk + p)),
            pl.BlockSpec((1, s, LANES), lambda i, p: (i, 0, kblk + p)),
            pl.BlockSpec((1, s, LANES), lambda i, p: (i, 0, vblk + p)),
        ],
        out_specs=pl.BlockSpec((1, s, LANES), lambda i, p: (i, 0, p)),
        out_shape=jax.ShapeDtypeStruct((b, s, SB_W), jnp.float32),
        scratch_shapes=[pltpu.VMEM((2, SB_TQ, SB_TK), jnp.float32),
                        pltpu.VMEM((SB_TQ, LANES), jnp.float32)],
        compiler_params=pltpu.CompilerParams(
            dimension_semantics=("arbitrary", "arbitrary")),
        name="sb",
    )(proj, proj, proj)


def _out_proj_kernel(x_ref, ya_ref, yb_ref, ga_ref, gb_ref, w_ref, o_ref):
    y = jnp.concatenate([_rms(ya_ref[...], ga_ref[...]), _rms(yb_ref[...], gb_ref[...])],
                        axis=1).astype(jnp.bfloat16)
    o_ref[...] = x_ref[...] + jnp.dot(y, w_ref[...], preferred_element_type=jnp.float32)


def _out_proj(x, ya, yb, ga, gb, w):
    n = x.shape[0]
    return pl.pallas_call(
        _out_proj_kernel,
        grid=(n // PROJ_TM,),
        in_specs=[
            pl.BlockSpec((PROJ_TM, D_MODEL), lambda i: (i, 0)),
            pl.BlockSpec((PROJ_TM, SWA_Q), lambda i: (i, 0)),
            pl.BlockSpec((PROJ_TM, SB_W), lambda i: (i, 0)),
            pl.BlockSpec((1, SWA_Q), lambda i: (0, 0)),
            pl.BlockSpec((1, SB_W), lambda i: (0, 0)),
            pl.BlockSpec((SWA_Q + SB_W, D_MODEL), lambda i: (0, 0)),
        ],
        out_specs=pl.BlockSpec((PROJ_TM, D_MODEL), lambda i: (i, 0)),
        out_shape=jax.ShapeDtypeStruct((n, D_MODEL), jnp.float32),
        compiler_params=pltpu.CompilerParams(
            dimension_semantics=("arbitrary",),
            vmem_limit_bytes=VMEM_LIMIT),
        name="out_proj",
    )(x, ya, yb, ga, gb, w)


def _prep_ffn_weights(w_gate, w_up, w_down):
    pad = D_FF_PAD - D_FF
    wg = jnp.pad(w_gate, ((0, 0), (0, pad))).astype(jnp.bfloat16)
    wu = jnp.pad(w_up, ((0, 0), (0, pad))).astype(jnp.bfloat16)
    wd = jnp.pad(w_down, ((0, pad), (0, 0))).astype(jnp.bfloat16)
    return wg, wu, wd


def _prep_w_in(w):
    def twice(cols):
        c = cols.reshape(D_MODEL, N_SWA_KV, 1, HEAD_DIM)
        return jnp.broadcast_to(c, (D_MODEL, N_SWA_KV, 2, HEAD_DIM)).reshape(D_MODEL, -1)
    o1 = SWA_Q
    o2 = o1 + N_SWA_KV * HEAD_DIM
    o3 = o2 + N_SWA_KV * HEAD_DIM
    return jnp.concatenate([w[:, :o1], twice(w[:, o1:o2]), twice(w[:, o2:o3]), w[:, o3:]],
                           axis=1).astype(jnp.bfloat16)


def _proj_col_scale():
    s = jnp.ones((PROJ_W,), jnp.float32)
    s = s.at[:SWA_Q].set(HEAD_DIM ** -0.5 * LOG2E)
    s = s.at[COL_QB:COL_KB].set(HEAD_DIM ** -0.5 * LOG2E)
    return s.reshape(1, PROJ_W)


def kernel(x, ffn1_norm, ffn1_w_gate, ffn1_w_up, ffn1_w_down, mix_norm, w_in, swa_sinks,
           swa_out_norm, sb_out_norm, w_out, ffn2_norm, ffn2_w_gate, ffn2_w_up, ffn2_w_down,
           final_norm):
    b, s, d = x.shape
    depth = ffn1_norm.shape[0]
    xf = x.reshape(b * s, d)
    i = jnp.arange(1, N_SWA_HEADS + 1, dtype=jnp.float32)
    slopes = jnp.exp2(-8.0 * i / N_SWA_HEADS)
    fg = final_norm.reshape(1, d)
    col_scale = _proj_col_scale()
    for l in range(depth):
        wg, wu, wd = _prep_ffn_weights(ffn1_w_gate[l], ffn1_w_up[l], ffn1_w_down[l])
        xf = _ffn(xf, ffn1_norm[l].reshape(1, d), wg, wu, wd, fg, final_norm=False)

        proj = _in_proj(xf, mix_norm[l].reshape(1, d), _prep_w_in(w_in[l]), col_scale)
        proj = proj.reshape(b, s, PROJ_W)
        ya = _swa(proj, slopes, swa_sinks[l])
        yb = _sb(proj)
        xf = _out_proj(xf, ya.reshape(b * s, SWA_Q), yb.reshape(b * s, SB_W),
                       swa_out_norm[l].reshape(1, SWA_Q), sb_out_norm[l].reshape(1, SB_W),
                       w_out[l].astype(jnp.bfloat16))

        wg, wu, wd = _prep_ffn_weights(ffn2_w_gate[l], ffn2_w_up[l], ffn2_w_down[l])
        xf = _ffn(xf, ffn2_norm[l].reshape(1, d), wg, wu, wd, fg,
                  final_norm=(l == depth - 1))
    return xf.reshape(b, s, d)
```

```python
import functools
import math

import jax
import jax.numpy as jnp
from jax import lax
from jax.experimental import pallas as pl
from jax.experimental.pallas import tpu as pltpu

D_MODEL = 2048
HEAD_DIM = 64
N_SWA_HEADS = 16
N_SWA_KV = 4
N_SB_HEADS = 16
WINDOW = 128
BLOCK = 128
D_FF = 5504
EPS = 1e-6
LOG2E = math.log2(math.e)

LANES = 128
D_FF_PAD = 5632
FFN_TM = 1024
FFN_TF = 512
PROJ_TM = 1024
PROJ_TN = 1024
OUT_TM = 512
SWA_QB = 4
SB_TQ = 1024
SB_TK = 128
SB_CK = 512
VMEM_LIMIT = 56 * 1024 * 1024

SWA_Q = N_SWA_HEADS * HEAD_DIM
SB_W = N_SB_HEADS * HEAD_DIM
COL_KA = SWA_Q
COL_VA = COL_KA + 2 * N_SWA_KV * HEAD_DIM
COL_QB = COL_VA + 2 * N_SWA_KV * HEAD_DIM
COL_KB = COL_QB + SB_W
COL_VB = COL_KB + SB_W
PROJ_W = COL_VB + SB_W

_NT = (((1,), (1,)), ((), ()))


def _rms(x, g):
    return x * lax.rsqrt(jnp.mean(x * x, axis=-1, keepdims=True) + EPS) * g


def _ffn_kernel(x_ref, g_ref, wg_ref, wu_ref, wd_ref, fg_ref, o_ref, h_ref, *, final_norm):
    j = pl.program_id(1)

    @pl.when(j == 0)
    def _():
        x = x_ref[...]
        h_ref[...] = _rms(x, g_ref[...]).astype(jnp.bfloat16)
        o_ref[...] = x

    h = h_ref[...]
    gate = jnp.dot(h, wg_ref[...], preferred_element_type=jnp.float32)
    up = jnp.dot(h, wu_ref[...], preferred_element_type=jnp.float32)
    act = (gate * jax.nn.sigmoid(gate) * up * 0.5).astype(jnp.bfloat16)
    o_ref[...] += jnp.dot(act, wd_ref[...], preferred_element_type=jnp.float32)

    if final_norm:
        @pl.when(j == pl.num_programs(1) - 1)
        def _():
            o_ref[...] = _rms(o_ref[...], fg_ref[...])


def _ffn(x, g, wg, wu, wd, fg, final_norm):
    n = x.shape[0]
    grid = (n // FFN_TM, D_FF_PAD // FFN_TF)
    return pl.pallas_call(
        functools.partial(_ffn_kernel, final_norm=final_norm),
        grid=grid,
        in_specs=[
            pl.BlockSpec((FFN_TM, D_MODEL), lambda i, j: (i, 0)),
            pl.BlockSpec((1, D_MODEL), lambda i, j: (0, 0)),
            pl.BlockSpec((D_MODEL, FFN_TF), lambda i, j: (0, j)),
            pl.BlockSpec((D_MODEL, FFN_TF), lambda i, j: (0, j)),
            pl.BlockSpec((FFN_TF, D_MODEL), lambda i, j: (j, 0)),
            pl.BlockSpec((1, D_MODEL), lambda i, j: (0, 0)),
        ],
        out_specs=pl.BlockSpec((FFN_TM, D_MODEL), lambda i, j: (i, 0)),
        out_shape=jax.ShapeDtypeStruct((n, D_MODEL), jnp.float32),
        scratch_shapes=[pltpu.VMEM((FFN_TM, D_MODEL), jnp.bfloat16)],
        compiler_params=pltpu.CompilerParams(
            dimension_semantics=("arbitrary", "arbitrary"),
            vmem_limit_bytes=VMEM_LIMIT),
        name="ffn_final" if final_norm else "ffn",
    )(x, g, wg, wu, wd, fg)


def _in_proj_kernel(x_ref, g_ref, w_ref, s_ref, o_ref, h_ref):
    @pl.when(pl.program_id(1) == 0)
    def _():
        h_ref[...] = _rms(x_ref[...], g_ref[...]).astype(jnp.bfloat16)

    acc = jnp.dot(h_ref[...], w_ref[...], preferred_element_type=jnp.float32)
    o_ref[...] = (acc * s_ref[...]).astype(jnp.bfloat16)


def _in_proj(x, g, w, col_scale):
    n = x.shape[0]
    grid = (n // PROJ_TM, PROJ_W // PROJ_TN)
    return pl.pallas_call(
        _in_proj_kernel,
        grid=grid,
        in_specs=[
            pl.BlockSpec((PROJ_TM, D_MODEL), lambda i, j: (i, 0)),
            pl.BlockSpec((1, D_MODEL), lambda i, j: (0, 0)),
            pl.BlockSpec((D_MODEL, PROJ_TN), lambda i, j: (0, j)),
            pl.BlockSpec((1, PROJ_TN), lambda i, j: (0, j)),
        ],
        out_specs=pl.BlockSpec((PROJ_TM, PROJ_TN), lambda i, j: (i, j)),
        out_shape=jax.ShapeDtypeStruct((n, PROJ_W), jnp.bfloat16),
        scratch_shapes=[pltpu.VMEM((PROJ_TM, D_MODEL), jnp.bfloat16)],
        compiler_params=pltpu.CompilerParams(
            dimension_semantics=("arbitrary", "arbitrary"),
            vmem_limit_bytes=VMEM_LIMIT),
        name="in_proj",
    )(x, g, w, col_scale)


def _swa_kernel(slopes_ref, sinks_ref, q_ref, k_ref, v_ref, o_ref, bias_ref):
    g = pl.program_id(1)
    n_heads = N_SWA_HEADS // N_SWA_KV
    nb = q_ref.shape[1] // BLOCK
    row = lax.broadcasted_iota(jnp.int32, (BLOCK, BLOCK), 0)
    col = lax.broadcasted_iota(jnp.int32, (BLOCK, BLOCK), 1)
    low_half = col < HEAD_DIM
    dist_p = (row + BLOCK - col).astype(jnp.float32)
    dist_c = (row - col).astype(jnp.float32)
    for jh in range(n_heads):
        slope2 = slopes_ref[g * n_heads + jh] * LOG2E
        bias_ref[jh, :, :BLOCK] = jnp.where(col > row, -slope2 * dist_p, -jnp.inf)
        bias_ref[jh, :, BLOCK:] = jnp.where(col <= row, -slope2 * dist_c, -jnp.inf)
    ones = jnp.ones((BLOCK, LANES), jnp.bfloat16)
    prev_cols = lax.broadcasted_iota(jnp.int32, (BLOCK, 2 * BLOCK), 1) < BLOCK

    def group(gi, _):
        for t in range(SWA_QB):
            qb = gi * SWA_QB + t
            q0 = pl.multiple_of(qb * BLOCK, BLOCK)
            p0 = pl.multiple_of(jnp.maximum(qb - 1, 0) * BLOCK, BLOCK)
            kk = jnp.concatenate([k_ref[0, pl.ds(p0, BLOCK), :],
                                  k_ref[0, pl.ds(q0, BLOCK), :]], axis=0)
            vv = jnp.concatenate(
                [jnp.concatenate([v_ref[0, pl.ds(p0, BLOCK), :], ones], axis=1),
                 jnp.concatenate([v_ref[0, pl.ds(q0, BLOCK), :], ones], axis=1)], axis=0)
            q = q_ref[0, pl.ds(q0, BLOCK), :]
            outs = []
            for jh in range(n_heads):
                sink2 = sinks_ref[g * n_heads + jh] * LOG2E
                qpair = q[:, (jh // 2) * LANES:(jh // 2 + 1) * LANES]
                keep = low_half if jh % 2 == 0 else jnp.logical_not(low_half)
                qh = jnp.where(keep, qpair, jnp.zeros_like(qpair))
                bias = bias_ref[jh]
                if t == 0:
                    bias = jnp.where(jnp.logical_and(prev_cols, qb == 0), -jnp.inf, bias)
                s = lax.dot_general(qh, kk, _NT, preferred_element_type=jnp.float32) + bias
                m = jnp.maximum(jnp.max(s, axis=-1, keepdims=True), sink2)
                e = jnp.exp2(s - m).astype(jnp.bfloat16)
                pv = jnp.dot(e, vv, preferred_element_type=jnp.float32)
                outs.append(pv[:, :LANES] / (pv[:, LANES:] + jnp.exp2(sink2 - m)))
            o_ref[0, pl.ds(q0, BLOCK), :] = jnp.concatenate(
                [jnp.where(low_half, outs[0], outs[1]), jnp.where(low_half, outs[2], outs[3])],
                axis=1)
        return 0

    lax.fori_loop(0, nb // SWA_QB, group, 0)


def _swa(proj, slopes, sinks):
    b, s, _ = proj.shape
    kblk, vblk = COL_KA // LANES, COL_VA // LANES
    smem = pl.BlockSpec(memory_space=pltpu.SMEM)
    return pl.pallas_call(
        _swa_kernel,
        grid=(b, N_SWA_KV),
        in_specs=[
            smem, smem,
            pl.BlockSpec((1, s, 2 * LANES), lambda i, g: (i, 0, g)),
            pl.BlockSpec((1, s, LANES), lambda i, g: (i, 0, kblk + g)),
            pl.BlockSpec((1, s, LANES), lambda i, g: (i, 0, vblk + g)),
        ],
        out_specs=pl.BlockSpec((1, s, 2 * LANES), lambda i, g: (i, 0, g)),
        out_shape=jax.ShapeDtypeStruct((b, s, SWA_Q), jnp.float32),
        scratch_shapes=[pltpu.VMEM((N_SWA_HEADS // N_SWA_KV, BLOCK, 2 * BLOCK), jnp.float32)],
        compiler_params=pltpu.CompilerParams(
            dimension_semantics=("arbitrary", "arbitrary")),
        name="swa",
    )(slopes, sinks, proj, proj, proj)


def _sb_kernel(q_ref, k_ref, v_ref, o_ref, carry_ref, acc_ref):
    s_len = q_ref.shape[1]
    nq = s_len // SB_TQ
    n_sub = SB_TQ // SB_TK
    r = lax.broadcasted_iota(jnp.int32, (2 * SB_TK, 2 * SB_TK), 0)
    c = lax.broadcasted_iota(jnp.int32, (2 * SB_TK, 2 * SB_TK), 1)
    tri = jnp.where(jnp.logical_and(r >= c, (r < SB_TK) == (c < SB_TK)), 1.0, 0.0
                    ).astype(jnp.bfloat16)
    low_half = lax.broadcasted_iota(jnp.int32, (SB_TK, LANES), 1) < HEAD_DIM
    strict_lower = (lax.broadcasted_iota(jnp.int32, (SB_TK, SB_TK), 1)
                    < lax.broadcasted_iota(jnp.int32, (SB_TK, SB_TK), 0))

    def q_block(qi, _):
        q0 = pl.multiple_of(qi * SB_TQ, SB_TQ)
        carry_ref[...] = jnp.zeros_like(carry_ref)
        acc_ref[...] = jnp.zeros_like(acc_ref)

        def k_block(k0, r0, diag):
            kblk = k_ref[0, pl.ds(k0, SB_TK), :]
            vblk = v_ref[0, pl.ds(k0, SB_TK), :]
            zero = jnp.zeros_like(kblk)
            kk = jnp.concatenate([jnp.where(low_half, kblk, zero),
                                  jnp.where(low_half, zero, kblk)], axis=0)
            vv = jnp.concatenate([jnp.where(low_half, vblk, zero),
                                  jnp.where(low_half, zero, vblk)], axis=0)
            def mask_top(x):
                top = jnp.where(strict_lower, x[:SB_TK], 0.0)
                return top if r0 == SB_TQ - SB_TK else jnp.concatenate([top, x[SB_TK:]], axis=0)

            q = q_ref[0, pl.ds(q0 + r0, SB_TQ - r0), :]
            u2 = lax.dot_general(q, kk, _NT, preferred_element_type=jnp.float32)
            sps = []
            for hd in range(2):
                u = u2[:, hd * SB_TK:(hd + 1) * SB_TK]
                neg_abs = lax.bitcast_convert_type(
                    lax.bitcast_convert_type(u, jnp.int32) | jnp.int32(-2 ** 31), jnp.float32)
                sp = jnp.maximum(u, 0.0) + jnp.log(1.0 + jnp.exp2(neg_abs)) * LOG2E
                if diag:
                    sp = mask_top(sp)
                sps.append(sp.astype(jnp.bfloat16))
            sums = jnp.dot(jnp.concatenate(sps, axis=1), tri,
                           preferred_element_type=jnp.float32)
            a_heads = []
            for hd in range(2):
                u = u2[:, hd * SB_TK:(hd + 1) * SB_TK]
                s_in = sums[:, hd * SB_TK:(hd + 1) * SB_TK]
                carry = carry_ref[hd, r0:, :]
                a = jnp.exp2(u - (carry + s_in))
                if diag:
                    a = mask_top(a)
                carry_ref[hd, r0:, :] = carry + jnp.broadcast_to(s_in[:, 0:1], s_in.shape)
                a_heads.append(a.astype(jnp.bfloat16))
            acc_ref[r0:, :] += jnp.dot(jnp.concatenate(a_heads, axis=1), vv,
                                       preferred_element_type=jnp.float32)

        for d in range(n_sub - 1, -1, -1):
            k_block(pl.multiple_of(q0 + d * SB_TK, SB_TK), d * SB_TK, True)

        def chunk(i, _):
            base = q0 - (i + 1) * SB_CK
            for d in range(SB_CK // SB_TK - 1, -1, -1):
                k_block(pl.multiple_of(base + d * SB_TK, SB_TK), 0, False)
            return 0

        lax.fori_loop(0, qi * (SB_TQ // SB_CK), chunk, 0)
        o_ref[0, pl.ds(q0, SB_TQ), :] = acc_ref[...]
        return 0

    lax.fori_loop(0, nq, q_block, 0)


def _sb(proj):
    b, s, _ = proj.shape
    npair = N_SB_HEADS // 2
    qblk, kblk, vblk = COL_QB // LANES, COL_KB // LANES, COL_VB // LANES
    return pl.pallas_call(
        _sb_kernel,
        grid=(b, npair),
        in_specs=[
            pl.BlockSpec((1, s, LANES), lambda i, p: (i, 0, qblk + p)),
            pl.BlockSpec((1, s, LANES), lambda i, p: (i, 0, kblk + p)),
            pl.BlockSpec((1, s, LANES), lambda i, p: (i, 0, vblk + p)),
        ],
        out_specs=pl.BlockSpec((1, s, LANES), lambda i, p: (i, 0, p)),
        out_shape=jax.ShapeDtypeStruct((b, s, SB_W), jnp.float32),
        scratch_shapes=[pltpu.VMEM((2, SB_TQ, SB_TK), jnp.float32),
                        pltpu.VMEM((SB_TQ, LANES), jnp.float32)],
        compiler_params=pltpu.CompilerParams(
            dimension_semantics=("arbitrary", "arbitrary")),
        name="sb",
    )(proj, proj, proj)


def _out_proj_kernel(x_ref, ya_ref, yb_ref, ga_ref, gb_ref, w_ref, o_ref):
    y = jnp.concatenate([_rms(ya_ref[...], ga_ref[...]), _rms(yb_ref[...], gb_ref[...])],
                        axis=1).astype(jnp.bfloat16)
    o_ref[...] = x_ref[...] + jnp.dot(y, w_ref[...], preferred_element_type=jnp.float32)


def _out_proj(x, ya, yb, ga, gb, w):
    n = x.shape[0]
    return pl.pallas_call(
        _out_proj_kernel,
        grid=(n // OUT_TM,),
        in_specs=[
            pl.BlockSpec((OUT_TM, D_MODEL), lambda i: (i, 0)),
            pl.BlockSpec((OUT_TM, SWA_Q), lambda i: (i, 0)),
            pl.BlockSpec((OUT_TM, SB_W), lambda i: (i, 0)),
            pl.BlockSpec((1, SWA_Q), lambda i: (0, 0)),
            pl.BlockSpec((1, SB_W), lambda i: (0, 0)),
            pl.BlockSpec((SWA_Q + SB_W, D_MODEL), lambda i: (0, 0)),
        ],
        out_specs=pl.BlockSpec((OUT_TM, D_MODEL), lambda i: (i, 0)),
        out_shape=jax.ShapeDtypeStruct((n, D_MODEL), jnp.float32),
        compiler_params=pltpu.CompilerParams(
            dimension_semantics=("arbitrary",),
            vmem_limit_bytes=VMEM_LIMIT),
        name="out_proj",
    )(x, ya, yb, ga, gb, w)


def _prep_ffn_weights(w_gate, w_up, w_down):
    pad = D_FF_PAD - D_FF
    wg = jnp.pad(w_gate, ((0, 0), (0, pad))).astype(jnp.bfloat16)
    wu = jnp.pad(w_up, ((0, 0), (0, pad))).astype(jnp.bfloat16)
    wd = jnp.pad(w_down, ((0, pad), (0, 0))).astype(jnp.bfloat16)
    return wg, wu, wd


def _prep_w_in(w):
    def twice(cols):
        c = cols.reshape(D_MODEL, N_SWA_KV, 1, HEAD_DIM)
        return jnp.broadcast_to(c, (D_MODEL, N_SWA_KV, 2, HEAD_DIM)).reshape(D_MODEL, -1)
    o1 = SWA_Q
    o2 = o1 + N_SWA_KV * HEAD_DIM
    o3 = o2 + N_SWA_KV * HEAD_DIM
    return jnp.concatenate([w[:, :o1], twice(w[:, o1:o2]), twice(w[:, o2:o3]), w[:, o3:]],
                           axis=1).astype(jnp.bfloat16)


def _proj_col_scale():
    s = jnp.ones((PROJ_W,), jnp.float32)
    s = s.at[:SWA_Q].set(HEAD_DIM ** -0.5 * LOG2E)
    s = s.at[COL_QB:COL_KB].set(HEAD_DIM ** -0.5 * LOG2E)
    return s.reshape(1, PROJ_W)


def kernel(x, ffn1_norm, ffn1_w_gate, ffn1_w_up, ffn1_w_down, mix_norm, w_in, swa_sinks,
           swa_out_norm, sb_out_norm, w_out, ffn2_norm, ffn2_w_gate, ffn2_w_up, ffn2_w_down,
           final_norm):
    b, s, d = x.shape
    depth = ffn1_norm.shape[0]
    xf = x.reshape(b * s, d)
    i = jnp.arange(1, N_SWA_HEADS + 1, dtype=jnp.float32)
    slopes = jnp.exp2(-8.0 * i / N_SWA_HEADS)
    fg = final_norm.reshape(1, d)
    col_scale = _proj_col_scale()
    for l in range(depth):
        wg, wu, wd = _prep_ffn_weights(ffn1_w_gate[l], ffn1_w_up[l], ffn1_w_down[l])
        xf = _ffn(xf, ffn1_norm[l].reshape(1, d), wg, wu, wd, fg, final_norm=False)

        proj = _in_proj(xf, mix_norm[l].reshape(1, d), _prep_w_in(w_in[l]), col_scale)
        proj = proj.reshape(b, s, PROJ_W)
        ya = _swa(proj, slopes, swa_sinks[l])
        yb = _sb(proj)
        xf = _out_proj(xf, ya.reshape(b * s, SWA_Q), yb.reshape(b * s, SB_W),
                       swa_out_norm[l].reshape(1, SWA_Q), sb_out_norm[l].reshape(1, SB_W),
                       w_out[l].astype(jnp.bfloat16))

        wg, wu, wd = _prep_ffn_weights(ffn2_w_gate[l], ffn2_w_up[l], ffn2_w_down[l])
        xf = _ffn(xf, ffn2_norm[l].reshape(1, d), wg, wu, wd, fg,
                  final_norm=(l == depth - 1))
    return xf.reshape(b, s, d)
```

```python
import functools
import math

import jax
import jax.numpy as jnp
from jax import lax
from jax.experimental import pallas as pl
from jax.experimental.pallas import tpu as pltpu

D_MODEL = 2048
HEAD_DIM = 64
N_SWA_HEADS = 16
N_SWA_KV = 4
N_SB_HEADS = 16
WINDOW = 128
BLOCK = 128
D_FF = 5504
EPS = 1e-6
LOG2E = math.log2(math.e)

LANES = 128
FFN_TM = 1024
FFN_TF = 512
PROJ_TM = 1024
PROJ_TN = 1024
OUT_TM = 512
SWA_QB = 4
SB_TQ = 1024
SB_TK = 128
SB_CK = 512
VMEM_LIMIT = 56 * 1024 * 1024

SWA_Q = N_SWA_HEADS * HEAD_DIM
SB_W = N_SB_HEADS * HEAD_DIM
COL_KA = SWA_Q
COL_VA = COL_KA + 2 * N_SWA_KV * HEAD_DIM
COL_QB = COL_VA + 2 * N_SWA_KV * HEAD_DIM
COL_KB = COL_QB + SB_W
COL_VB = COL_KB + SB_W
PROJ_W = COL_VB + SB_W

_NT = (((1,), (1,)), ((), ()))


def _rms(x, g):
    return x * lax.rsqrt(jnp.mean(x * x, axis=-1, keepdims=True) + EPS) * g


def _ffn_kernel(x_ref, g_ref, wg_ref, wu_ref, wd_ref, fg_ref, o_ref, h_ref, *, final_norm):
    j = pl.program_id(1)

    @pl.when(j == 0)
    def _():
        x = x_ref[...]
        h_ref[...] = _rms(x, g_ref[...]).astype(jnp.bfloat16)
        o_ref[...] = x

    valid = D_FF - j * FFN_TF
    h = h_ref[...]
    gate = jnp.dot(h, wg_ref[...], preferred_element_type=jnp.float32)
    up = jnp.dot(h, wu_ref[...], preferred_element_type=jnp.float32)
    act = gate * jax.nn.sigmoid(gate) * up * 0.5
    col = lax.broadcasted_iota(jnp.int32, act.shape, 1)
    act = jnp.where(col < valid, act, 0.0).astype(jnp.bfloat16)
    wd32 = pltpu.bitcast(wd_ref[...], jnp.uint32)
    row = lax.broadcasted_iota(jnp.int32, wd32.shape, 0)
    wd = pltpu.bitcast(jnp.where(row < valid // 2, wd32, jnp.uint32(0)), jnp.bfloat16)
    o_ref[...] += jnp.dot(act, wd, preferred_element_type=jnp.float32)

    if final_norm:
        @pl.when(j == pl.num_programs(1) - 1)
        def _():
            o_ref[...] = _rms(o_ref[...], fg_ref[...])


def _ffn(x, g, wg, wu, wd, fg, final_norm):
    n = x.shape[0]
    grid = (n // FFN_TM, pl.cdiv(D_FF, FFN_TF))
    return pl.pallas_call(
        functools.partial(_ffn_kernel, final_norm=final_norm),
        grid=grid,
        in_specs=[
            pl.BlockSpec((FFN_TM, D_MODEL), lambda i, j: (i, 0)),
            pl.BlockSpec((1, D_MODEL), lambda i, j: (0, 0)),
            pl.BlockSpec((D_MODEL, FFN_TF), lambda i, j: (0, j)),
            pl.BlockSpec((D_MODEL, FFN_TF), lambda i, j: (0, j)),
            pl.BlockSpec((FFN_TF, D_MODEL), lambda i, j: (j, 0)),
            pl.BlockSpec((1, D_MODEL), lambda i, j: (0, 0)),
        ],
        out_specs=pl.BlockSpec((FFN_TM, D_MODEL), lambda i, j: (i, 0)),
        out_shape=jax.ShapeDtypeStruct((n, D_MODEL), jnp.float32),
        scratch_shapes=[pltpu.VMEM((FFN_TM, D_MODEL), jnp.bfloat16)],
        compiler_params=pltpu.CompilerParams(
            dimension_semantics=("arbitrary", "arbitrary"),
            vmem_limit_bytes=VMEM_LIMIT),
        name="ffn_final" if final_norm else "ffn",
    )(x, g, wg, wu, wd, fg)


def _in_proj_kernel(x_ref, g_ref, w_ref, s_ref, o_ref, h_ref):
    @pl.when(pl.program_id(1) == 0)
    def _():
        h_ref[...] = _rms(x_ref[...], g_ref[...]).astype(jnp.bfloat16)

    acc = jnp.dot(h_ref[...], w_ref[...], preferred_element_type=jnp.float32)
    o_ref[...] = (acc * s_ref[...]).astype(jnp.bfloat16)


def _in_proj(x, g, w, col_scale):
    n = x.shape[0]
    grid = (n // PROJ_TM, PROJ_W // PROJ_TN)
    return pl.pallas_call(
        _in_proj_kernel,
        grid=grid,
        in_specs=[
            pl.BlockSpec((PROJ_TM, D_MODEL), lambda i, j: (i, 0)),
            pl.BlockSpec((1, D_MODEL), lambda i, j: (0, 0)),
            pl.BlockSpec((D_MODEL, PROJ_TN), lambda i, j: (0, j)),
            pl.BlockSpec((1, PROJ_TN), lambda i, j: (0, j)),
        ],
        out_specs=pl.BlockSpec((PROJ_TM, PROJ_TN), lambda i, j: (i, j)),
        out_shape=jax.ShapeDtypeStruct((n, PROJ_W), jnp.bfloat16),
        scratch_shapes=[pltpu.VMEM((PROJ_TM, D_MODEL), jnp.bfloat16)],
        compiler_params=pltpu.CompilerParams(
            dimension_semantics=("arbitrary", "arbitrary"),
            vmem_limit_bytes=VMEM_LIMIT),
        name="in_proj",
    )(x, g, w, col_scale)


def _swa_kernel(slopes_ref, sinks_ref, q_ref, k_ref, v_ref, o_ref, bias_ref):
    g = pl.program_id(1)
    n_heads = N_SWA_HEADS // N_SWA_KV
    nb = q_ref.shape[1] // BLOCK
    row = lax.broadcasted_iota(jnp.int32, (BLOCK, BLOCK), 0)
    col = lax.broadcasted_iota(jnp.int32, (BLOCK, BLOCK), 1)
    low_half = col < HEAD_DIM
    dist_p = (row + BLOCK - col).astype(jnp.float32)
    dist_c = (row - col).astype(jnp.float32)
    for jh in range(n_heads):
        slope2 = slopes_ref[g * n_heads + jh] * LOG2E
        bias_ref[jh, :, :BLOCK] = jnp.where(col > row, -slope2 * dist_p, -jnp.inf)
        bias_ref[jh, :, BLOCK:] = jnp.where(col <= row, -slope2 * dist_c, -jnp.inf)
    ones = jnp.ones((BLOCK, LANES), jnp.bfloat16)
    prev_cols = lax.broadcasted_iota(jnp.int32, (BLOCK, 2 * BLOCK), 1) < BLOCK

    def group(gi, _):
        for t in range(SWA_QB):
            qb = gi * SWA_QB + t
            q0 = pl.multiple_of(qb * BLOCK, BLOCK)
            p0 = pl.multiple_of(jnp.maximum(qb - 1, 0) * BLOCK, BLOCK)
            kk = jnp.concatenate([k_ref[0, pl.ds(p0, BLOCK), :],
                                  k_ref[0, pl.ds(q0, BLOCK), :]], axis=0)
            vv = jnp.concatenate(
                [jnp.concatenate([v_ref[0, pl.ds(p0, BLOCK), :], ones], axis=1),
                 jnp.concatenate([v_ref[0, pl.ds(q0, BLOCK), :], ones], axis=1)], axis=0)
            q = q_ref[0, pl.ds(q0, BLOCK), :]
            outs = []
            for jh in range(n_heads):
                sink2 = sinks_ref[g * n_heads + jh] * LOG2E
                qpair = q[:, (jh // 2) * LANES:(jh // 2 + 1) * LANES]
                keep = low_half if jh % 2 == 0 else jnp.logical_not(low_half)
                qh = jnp.where(keep, qpair, jnp.zeros_like(qpair))
                bias = bias_ref[jh]
                if t == 0:
                    bias = jnp.where(jnp.logical_and(prev_cols, qb == 0), -jnp.inf, bias)
                s = lax.dot_general(qh, kk, _NT, preferred_element_type=jnp.float32) + bias
                m = jnp.maximum(jnp.max(s, axis=-1, keepdims=True), sink2)
                e = jnp.exp2(s - m).astype(jnp.bfloat16)
                pv = jnp.dot(e, vv, preferred_element_type=jnp.float32)
                outs.append(pv[:, :LANES] / (pv[:, LANES:] + jnp.exp2(sink2 - m)))
            o_ref[0, pl.ds(q0, BLOCK), :] = jnp.concatenate(
                [jnp.where(low_half, outs[0], outs[1]), jnp.where(low_half, outs[2], outs[3])],
                axis=1)
        return 0

    lax.fori_loop(0, nb // SWA_QB, group, 0)


def _swa(proj, slopes, sinks):
    b, s, _ = proj.shape
    kblk, vblk = COL_KA // LANES, COL_VA // LANES
    smem = pl.BlockSpec(memory_space=pltpu.SMEM)
    return pl.pallas_call(
        _swa_kernel,
        grid=(b, N_SWA_KV),
        in_specs=[
            smem, smem,
            pl.BlockSpec((1, s, 2 * LANES), lambda i, g: (i, 0, g)),
            pl.BlockSpec((1, s, LANES), lambda i, g: (i, 0, kblk + g)),
            pl.BlockSpec((1, s, LANES), lambda i, g: (i, 0, vblk + g)),
        ],
        out_specs=pl.BlockSpec((1, s, 2 * LANES), lambda i, g: (i, 0, g)),
        out_shape=jax.ShapeDtypeStruct((b, s, SWA_Q), jnp.float32),
        scratch_shapes=[pltpu.VMEM((N_SWA_HEADS // N_SWA_KV, BLOCK, 2 * BLOCK), jnp.float32)],
        compiler_params=pltpu.CompilerParams(
            dimension_semantics=("arbitrary", "arbitrary")),
        name="swa",
    )(slopes, sinks, proj, proj, proj)


def _sb_kernel(q_ref, k_ref, v_ref, o_ref, carry_ref, acc_ref):
    s_len = q_ref.shape[1]
    nq = s_len // SB_TQ
    n_sub = SB_TQ // SB_TK
    r = lax.broadcasted_iota(jnp.int32, (2 * SB_TK, 2 * SB_TK), 0)
    c = lax.broadcasted_iota(jnp.int32, (2 * SB_TK, 2 * SB_TK), 1)
    tri = jnp.where(jnp.logical_and(r >= c, (r < SB_TK) == (c < SB_TK)), 1.0, 0.0
                    ).astype(jnp.bfloat16)
    low_half = lax.broadcasted_iota(jnp.int32, (SB_TK, LANES), 1) < HEAD_DIM
    strict_lower = (lax.broadcasted_iota(jnp.int32, (SB_TK, SB_TK), 1)
                    < lax.broadcasted_iota(jnp.int32, (SB_TK, SB_TK), 0))

    def q_block(qi, _):
        q0 = pl.multiple_of(qi * SB_TQ, SB_TQ)
        carry_ref[...] = jnp.zeros_like(carry_ref)
        acc_ref[...] = jnp.zeros_like(acc_ref)

        def k_block(k0, r0, diag):
            kblk = k_ref[0, pl.ds(k0, SB_TK), :]
            vblk = v_ref[0, pl.ds(k0, SB_TK), :]
            zero = jnp.zeros_like(kblk)
            kk = jnp.concatenate([jnp.where(low_half, kblk, zero),
                                  jnp.where(low_half, zero, kblk)], axis=0)
            vv = jnp.concatenate([jnp.where(low_half, vblk, zero),
                                  jnp.where(low_half, zero, vblk)], axis=0)
            def mask_top(x):
                top = jnp.where(strict_lower, x[:SB_TK], 0.0)
                return top if r0 == SB_TQ - SB_TK else jnp.concatenate([top, x[SB_TK:]], axis=0)

            q = q_ref[0, pl.ds(q0 + r0, SB_TQ - r0), :]
            u2 = lax.dot_general(q, kk, _NT, preferred_element_type=jnp.float32)
            sps = []
            for hd in range(2):
                u = u2[:, hd * SB_TK:(hd + 1) * SB_TK]
                neg_abs = lax.bitcast_convert_type(
                    lax.bitcast_convert_type(u, jnp.int32) | jnp.int32(-2 ** 31), jnp.float32)
                sp = jnp.maximum(u, 0.0) + jnp.log(1.0 + jnp.exp2(neg_abs)) * LOG2E
                if diag:
                    sp = mask_top(sp)
                sps.append(sp.astype(jnp.bfloat16))
            sums = jnp.dot(jnp.concatenate(sps, axis=1), tri,
                           preferred_element_type=jnp.float32)
            a_heads = []
            for hd in range(2):
                u = u2[:, hd * SB_TK:(hd + 1) * SB_TK]
                s_in = sums[:, hd * SB_TK:(hd + 1) * SB_TK]
                carry = carry_ref[hd, r0:, :]
                a = jnp.exp2(u - (carry + s_in))
                if diag:
                    a = mask_top(a)
                carry_ref[hd, r0:, :] = carry + jnp.broadcast_to(s_in[:, 0:1], s_in.shape)
                a_heads.append(a.astype(jnp.bfloat16))
            acc_ref[r0:, :] += jnp.dot(jnp.concatenate(a_heads, axis=1), vv,
                                       preferred_element_type=jnp.float32)

        for d in range(n_sub - 1, -1, -1):
            k_block(pl.multiple_of(q0 + d * SB_TK, SB_TK), d * SB_TK, True)

        def chunk(i, _):
            base = q0 - (i + 1) * SB_CK
            for d in range(SB_CK // SB_TK - 1, -1, -1):
                k_block(pl.multiple_of(base + d * SB_TK, SB_TK), 0, False)
            return 0

        lax.fori_loop(0, qi * (SB_TQ // SB_CK), chunk, 0)
        o_ref[0, pl.ds(q0, SB_TQ), :] = acc_ref[...]
        return 0

    lax.fori_loop(0, nq, q_block, 0)


def _sb(proj):
    b, s, _ = proj.shape
    npair = N_SB_HEADS // 2
    qblk, kblk, vblk = COL_QB // LANES, COL_KB // LANES, COL_VB // LANES
    return pl.pallas_call(
        _sb_kernel,
        grid=(b, npair),
        in_specs=[
            pl.BlockSpec((1, s, LANES), lambda i, p: (i, 0, qblk + p)),
            pl.BlockSpec((1, s, LANES), lambda i, p: (i, 0, kblk + p)),
            pl.BlockSpec((1, s, LANES), lambda i, p: (i, 0, vblk + p)),
        ],
        out_specs=pl.BlockSpec((1, s, LANES), lambda i, p: (i, 0, p)),
        out_shape=jax.ShapeDtypeStruct((b, s, SB_W), jnp.float32),
        scratch_shapes=[pltpu.VMEM((2, SB_TQ, SB_TK), jnp.float32),
                        pltpu.VMEM((SB_TQ, LANES), jnp.float32)],
        compiler_params=pltpu.CompilerParams(
            dimension_semantics=("arbitrary", "arbitrary")),
        name="sb",
    )(proj, proj, proj)


def _out_proj_kernel(x_ref, ya_ref, yb_ref, ga_ref, gb_ref, w_ref, o_ref):
    y = jnp.concatenate([_rms(ya_ref[...], ga_ref[...]), _rms(yb_ref[...], gb_ref[...])],
                        axis=1).astype(jnp.bfloat16)
    o_ref[...] = x_ref[...] + jnp.dot(y, w_ref[...], preferred_element_type=jnp.float32)


def _out_proj(x, ya, yb, ga, gb, w):
    n = x.shape[0]
    return pl.pallas_call(
        _out_proj_kernel,
        grid=(n // OUT_TM,),
        in_specs=[
            pl.BlockSpec((OUT_TM, D_MODEL), lambda i: (i, 0)),
            pl.BlockSpec((OUT_TM, SWA_Q), lambda i: (i, 0)),
            pl.BlockSpec((OUT_TM, SB_W), lambda i: (i, 0)),
            pl.BlockSpec((1, SWA_Q), lambda i: (0, 0)),
            pl.BlockSpec((1, SB_W), lambda i: (0, 0)),
            pl.BlockSpec((SWA_Q + SB_W, D_MODEL), lambda i: (0, 0)),
        ],
        out_specs=pl.BlockSpec((OUT_TM, D_MODEL), lambda i: (i, 0)),
        out_shape=jax.ShapeDtypeStruct((n, D_MODEL), jnp.float32),
        compiler_params=pltpu.CompilerParams(
            dimension_semantics=("arbitrary",),
            vmem_limit_bytes=VMEM_LIMIT),
        name="out_proj",
    )(x, ya, yb, ga, gb, w)


def _prep_ffn_weights(w_gate, w_up, w_down):
    return (w_gate.astype(jnp.bfloat16), w_up.astype(jnp.bfloat16),
            w_down.astype(jnp.bfloat16))


def _prep_w_in(w):
    def twice(cols):
        c = cols.reshape(D_MODEL, N_SWA_KV, 1, HEAD_DIM)
        return jnp.broadcast_to(c, (D_MODEL, N_SWA_KV, 2, HEAD_DIM)).reshape(D_MODEL, -1)
    o1 = SWA_Q
    o2 = o1 + N_SWA_KV * HEAD_DIM
    o3 = o2 + N_SWA_KV * HEAD_DIM
    return jnp.concatenate([w[:, :o1], twice(w[:, o1:o2]), twice(w[:, o2:o3]), w[:, o3:]],
                           axis=1).astype(jnp.bfloat16)


def _proj_col_scale():
    s = jnp.ones((PROJ_W,), jnp.float32)
    s = s.at[:SWA_Q].set(HEAD_DIM ** -0.5 * LOG2E)
    s = s.at[COL_QB:COL_KB].set(HEAD_DIM ** -0.5 * LOG2E)
    return s.reshape(1, PROJ_W)


def kernel(x, ffn1_norm, ffn1_w_gate, ffn1_w_up, ffn1_w_down, mix_norm, w_in, swa_sinks,
           swa_out_norm, sb_out_norm, w_out, ffn2_norm, ffn2_w_gate, ffn2_w_up, ffn2_w_down,
           final_norm):
    b, s, d = x.shape
    depth = ffn1_norm.shape[0]
    xf = x.reshape(b * s, d)
    i = jnp.arange(1, N_SWA_HEADS + 1, dtype=jnp.float32)
    slopes = jnp.exp2(-8.0 * i / N_SWA_HEADS)
    fg = final_norm.reshape(1, d)
    col_scale = _proj_col_scale()
    for l in range(depth):
        wg, wu, wd = _prep_ffn_weights(ffn1_w_gate[l], ffn1_w_up[l], ffn1_w_down[l])
        xf = _ffn(xf, ffn1_norm[l].reshape(1, d), wg, wu, wd, fg, final_norm=False)

        proj = _in_proj(xf, mix_norm[l].reshape(1, d), _prep_w_in(w_in[l]), col_scale)
        proj = proj.reshape(b, s, PROJ_W)
        ya = _swa(proj, slopes, swa_sinks[l])
        yb = _sb(proj)
        xf = _out_proj(xf, ya.reshape(b * s, SWA_Q), yb.reshape(b * s, SB_W),
                       swa_out_norm[l].reshape(1, SWA_Q), sb_out_norm[l].reshape(1, SB_W),
                       w_out[l].astype(jnp.bfloat16))

        wg, wu, wd = _prep_ffn_weights(ffn2_w_gate[l], ffn2_w_up[l], ffn2_w_down[l])
        xf = _ffn(xf, ffn2_norm[l].reshape(1, d), wg, wu, wd, fg,
                  final_norm=(l == depth - 1))
    return xf.reshape(b, s, d)
```

```python
import functools
import math

import jax
import jax.numpy as jnp
from jax import lax
from jax.experimental import pallas as pl
from jax.experimental.pallas import tpu as pltpu

D_MODEL = 2048
HEAD_DIM = 64
N_SWA_HEADS = 16
N_SWA_KV = 4
N_SB_HEADS = 16
WINDOW = 128
BLOCK = 128
D_FF = 5504
EPS = 1e-6
LOG2E = math.log2(math.e)

LANES = 128
FFN_TM = 1024
FFN_TF = 512
PROJ_TM = 1024
PROJ_TN = 1024
OUT_TM = 512
SWA_QB = 4
SB_TQ = 2048
SB_TK = 128
SB_CK = 512
VMEM_LIMIT = 56 * 1024 * 1024

SWA_Q = N_SWA_HEADS * HEAD_DIM
SB_W = N_SB_HEADS * HEAD_DIM
COL_KA = SWA_Q
COL_VA = COL_KA + 2 * N_SWA_KV * HEAD_DIM
COL_QB = COL_VA + 2 * N_SWA_KV * HEAD_DIM
COL_KB = COL_QB + SB_W
COL_VB = COL_KB + SB_W
PROJ_W = COL_VB + SB_W

_NT = (((1,), (1,)), ((), ()))


def _rms(x, g):
    return x * lax.rsqrt(jnp.mean(x * x, axis=-1, keepdims=True) + EPS) * g


def _ffn_kernel(x_ref, g_ref, wg_ref, wu_ref, wd_ref, fg_ref, o_ref, h_ref, *, final_norm):
    j = pl.program_id(1)

    @pl.when(j == 0)
    def _():
        x = x_ref[...]
        h_ref[...] = _rms(x, g_ref[...]).astype(jnp.bfloat16)
        o_ref[...] = x

    valid = D_FF - j * FFN_TF
    h = h_ref[...]
    gate = jnp.dot(h, wg_ref[...], preferred_element_type=jnp.float32)
    up = jnp.dot(h, wu_ref[...], preferred_element_type=jnp.float32)
    act = gate * jax.nn.sigmoid(gate) * up * 0.5
    col = lax.broadcasted_iota(jnp.int32, act.shape, 1)
    act = jnp.where(col < valid, act, 0.0).astype(jnp.bfloat16)
    wd32 = pltpu.bitcast(wd_ref[...], jnp.uint32)
    row = lax.broadcasted_iota(jnp.int32, wd32.shape, 0)
    wd = pltpu.bitcast(jnp.where(row < valid // 2, wd32, jnp.uint32(0)), jnp.bfloat16)
    o_ref[...] += jnp.dot(act, wd, preferred_element_type=jnp.float32)

    if final_norm:
        @pl.when(j == pl.num_programs(1) - 1)
        def _():
            o_ref[...] = _rms(o_ref[...], fg_ref[...])


def _ffn(x, g, wg, wu, wd, fg, final_norm):
    n = x.shape[0]
    grid = (n // FFN_TM, pl.cdiv(D_FF, FFN_TF))
    return pl.pallas_call(
        functools.partial(_ffn_kernel, final_norm=final_norm),
        grid=grid,
        in_specs=[
            pl.BlockSpec((FFN_TM, D_MODEL), lambda i, j: (i, 0)),
            pl.BlockSpec((1, D_MODEL), lambda i, j: (0, 0)),
            pl.BlockSpec((D_MODEL, FFN_TF), lambda i, j: (0, j)),
            pl.BlockSpec((D_MODEL, FFN_TF), lambda i, j: (0, j)),
            pl.BlockSpec((FFN_TF, D_MODEL), lambda i, j: (j, 0)),
            pl.BlockSpec((1, D_MODEL), lambda i, j: (0, 0)),
        ],
        out_specs=pl.BlockSpec((FFN_TM, D_MODEL), lambda i, j: (i, 0)),
        out_shape=jax.ShapeDtypeStruct((n, D_MODEL), jnp.float32),
        scratch_shapes=[pltpu.VMEM((FFN_TM, D_MODEL), jnp.bfloat16)],
        compiler_params=pltpu.CompilerParams(
            dimension_semantics=("arbitrary", "arbitrary"),
            vmem_limit_bytes=VMEM_LIMIT),
        name="ffn_final" if final_norm else "ffn",
    )(x, g, wg, wu, wd, fg)


def _in_proj_kernel(x_ref, g_ref, w_ref, s_ref, o_ref, h_ref):
    @pl.when(pl.program_id(1) == 0)
    def _():
        h_ref[...] = _rms(x_ref[...], g_ref[...]).astype(jnp.bfloat16)

    acc = jnp.dot(h_ref[...], w_ref[...], preferred_element_type=jnp.float32)
    o_ref[...] = (acc * s_ref[...]).astype(jnp.bfloat16)


def _in_proj(x, g, w, col_scale):
    n = x.shape[0]
    grid = (n // PROJ_TM, PROJ_W // PROJ_TN)
    return pl.pallas_call(
        _in_proj_kernel,
        grid=grid,
        in_specs=[
            pl.BlockSpec((PROJ_TM, D_MODEL), lambda i, j: (i, 0)),
            pl.BlockSpec((1, D_MODEL), lambda i, j: (0, 0)),
            pl.BlockSpec((D_MODEL, PROJ_TN), lambda i, j: (0, j)),
            pl.BlockSpec((1, PROJ_TN), lambda i, j: (0, j)),
        ],
        out_specs=pl.BlockSpec((PROJ_TM, PROJ_TN), lambda i, j: (i, j)),
        out_shape=jax.ShapeDtypeStruct((n, PROJ_W), jnp.bfloat16),
        scratch_shapes=[pltpu.VMEM((PROJ_TM, D_MODEL), jnp.bfloat16)],
        compiler_params=pltpu.CompilerParams(
            dimension_semantics=("arbitrary", "arbitrary"),
            vmem_limit_bytes=VMEM_LIMIT),
        name="in_proj",
    )(x, g, w, col_scale)


def _swa_kernel(slopes_ref, sinks_ref, q_ref, k_ref, v_ref, o_ref, bias_ref):
    g = pl.program_id(1)
    n_heads = N_SWA_HEADS // N_SWA_KV
    nb = q_ref.shape[1] // BLOCK
    row = lax.broadcasted_iota(jnp.int32, (BLOCK, BLOCK), 0)
    col = lax.broadcasted_iota(jnp.int32, (BLOCK, BLOCK), 1)
    low_half = col < HEAD_DIM
    dist_p = (row + BLOCK - col).astype(jnp.float32)
    dist_c = (row - col).astype(jnp.float32)
    for jh in range(n_heads):
        slope2 = slopes_ref[g * n_heads + jh] * LOG2E
        bias_ref[jh, :, :BLOCK] = jnp.where(col > row, -slope2 * dist_p, -jnp.inf)
        bias_ref[jh, :, BLOCK:] = jnp.where(col <= row, -slope2 * dist_c, -jnp.inf)
    ones = jnp.ones((BLOCK, LANES), jnp.bfloat16)
    prev_cols = lax.broadcasted_iota(jnp.int32, (BLOCK, 2 * BLOCK), 1) < BLOCK

    def group(gi, _):
        for t in range(SWA_QB):
            qb = gi * SWA_QB + t
            q0 = pl.multiple_of(qb * BLOCK, BLOCK)
            p0 = pl.multiple_of(jnp.maximum(qb - 1, 0) * BLOCK, BLOCK)
            kk = jnp.concatenate([k_ref[0, pl.ds(p0, BLOCK), :],
                                  k_ref[0, pl.ds(q0, BLOCK), :]], axis=0)
            vv = jnp.concatenate(
                [jnp.concatenate([v_ref[0, pl.ds(p0, BLOCK), :], ones], axis=1),
                 jnp.concatenate([v_ref[0, pl.ds(q0, BLOCK), :], ones], axis=1)], axis=0)
            q = q_ref[0, pl.ds(q0, BLOCK), :]
            outs = []
            for jh in range(n_heads):
                sink2 = sinks_ref[g * n_heads + jh] * LOG2E
                qpair = q[:, (jh // 2) * LANES:(jh // 2 + 1) * LANES]
                keep = low_half if jh % 2 == 0 else jnp.logical_not(low_half)
                qh = jnp.where(keep, qpair, jnp.zeros_like(qpair))
                bias = bias_ref[jh]
                if t == 0:
                    bias = jnp.where(jnp.logical_and(prev_cols, qb == 0), -jnp.inf, bias)
                s = lax.dot_general(qh, kk, _NT, preferred_element_type=jnp.float32) + bias
                m = jnp.maximum(jnp.max(s, axis=-1, keepdims=True), sink2)
                e = jnp.exp2(s - m).astype(jnp.bfloat16)
                pv = jnp.dot(e, vv, preferred_element_type=jnp.float32)
                outs.append(pv[:, :LANES] / (pv[:, LANES:] + jnp.exp2(sink2 - m)))
            o_ref[0, pl.ds(q0, BLOCK), :] = jnp.concatenate(
                [jnp.where(low_half, outs[0], outs[1]), jnp.where(low_half, outs[2], outs[3])],
                axis=1)
        return 0

    lax.fori_loop(0, nb // SWA_QB, group, 0)


def _swa(proj, slopes, sinks):
    b, s, _ = proj.shape
    kblk, vblk = COL_KA // LANES, COL_VA // LANES
    smem = pl.BlockSpec(memory_space=pltpu.SMEM)
    return pl.pallas_call(
        _swa_kernel,
        grid=(b, N_SWA_KV),
        in_specs=[
            smem, smem,
            pl.BlockSpec((1, s, 2 * LANES), lambda i, g: (i, 0, g)),
            pl.BlockSpec((1, s, LANES), lambda i, g: (i, 0, kblk + g)),
            pl.BlockSpec((1, s, LANES), lambda i, g: (i, 0, vblk + g)),
        ],
        out_specs=pl.BlockSpec((1, s, 2 * LANES), lambda i, g: (i, 0, g)),
        out_shape=jax.ShapeDtypeStruct((b, s, SWA_Q), jnp.float32),
        scratch_shapes=[pltpu.VMEM((N_SWA_HEADS // N_SWA_KV, BLOCK, 2 * BLOCK), jnp.float32)],
        compiler_params=pltpu.CompilerParams(
            dimension_semantics=("arbitrary", "arbitrary")),
        name="swa",
    )(slopes, sinks, proj, proj, proj)


def _sb_kernel(q_ref, k_ref, v_ref, o_ref, carry_ref, acc_ref):
    s_len = q_ref.shape[1]
    nq = s_len // SB_TQ
    n_sub = SB_TQ // SB_TK
    r = lax.broadcasted_iota(jnp.int32, (2 * SB_TK, 2 * SB_TK), 0)
    c = lax.broadcasted_iota(jnp.int32, (2 * SB_TK, 2 * SB_TK), 1)
    tri = jnp.where(jnp.logical_and(r >= c, (r < SB_TK) == (c < SB_TK)), 1.0, 0.0
                    ).astype(jnp.bfloat16)
    low_half = lax.broadcasted_iota(jnp.int32, (SB_TK, LANES), 1) < HEAD_DIM
    strict_lower = (lax.broadcasted_iota(jnp.int32, (SB_TK, SB_TK), 1)
                    < lax.broadcasted_iota(jnp.int32, (SB_TK, SB_TK), 0))

    def q_block(qi, _):
        q0 = pl.multiple_of(qi * SB_TQ, SB_TQ)
        carry_ref[...] = jnp.zeros_like(carry_ref)
        acc_ref[...] = jnp.zeros_like(acc_ref)

        def k_block(k0, r0, diag):
            kblk = k_ref[0, pl.ds(k0, SB_TK), :]
            vblk = v_ref[0, pl.ds(k0, SB_TK), :]
            zero = jnp.zeros_like(kblk)
            kk = jnp.concatenate([jnp.where(low_half, kblk, zero),
                                  jnp.where(low_half, zero, kblk)], axis=0)
            vv = jnp.concatenate([jnp.where(low_half, vblk, zero),
                                  jnp.where(low_half, zero, vblk)], axis=0)
            def mask_top(x):
                top = jnp.where(strict_lower, x[:SB_TK], 0.0)
                return top if r0 == SB_TQ - SB_TK else jnp.concatenate([top, x[SB_TK:]], axis=0)

            q = q_ref[0, pl.ds(q0 + r0, SB_TQ - r0), :]
            u2 = lax.dot_general(q, kk, _NT, preferred_element_type=jnp.float32)
            sps = []
            for hd in range(2):
                u = u2[:, hd * SB_TK:(hd + 1) * SB_TK]
                neg_abs = lax.bitcast_convert_type(
                    lax.bitcast_convert_type(u, jnp.int32) | jnp.int32(-2 ** 31), jnp.float32)
                sp = jnp.maximum(u, 0.0) + jnp.log(1.0 + jnp.exp2(neg_abs)) * LOG2E
                if diag:
                    sp = mask_top(sp)
                sps.append(sp.astype(jnp.bfloat16))
            sums = jnp.dot(jnp.concatenate(sps, axis=1), tri,
                           preferred_element_type=jnp.float32)
            a_heads = []
            for hd in range(2):
                u = u2[:, hd * SB_TK:(hd + 1) * SB_TK]
                s_in = sums[:, hd * SB_TK:(hd + 1) * SB_TK]
                carry = carry_ref[hd, r0:, :]
                a = jnp.exp2(u - (carry + s_in))
                if diag:
                    a = mask_top(a)
                carry_ref[hd, r0:, :] = carry + jnp.broadcast_to(s_in[:, 0:1], s_in.shape)
                a_heads.append(a.astype(jnp.bfloat16))
            acc_ref[r0:, :] += jnp.dot(jnp.concatenate(a_heads, axis=1), vv,
                                       preferred_element_type=jnp.float32)

        for d in range(n_sub - 1, -1, -1):
            k_block(pl.multiple_of(q0 + d * SB_TK, SB_TK), d * SB_TK, True)

        def chunk(i, _):
            base = q0 - (i + 1) * SB_CK
            for d in range(SB_CK // SB_TK - 1, -1, -1):
                k_block(pl.multiple_of(base + d * SB_TK, SB_TK), 0, False)
            return 0

        lax.fori_loop(0, qi * (SB_TQ // SB_CK), chunk, 0)
        o_ref[0, pl.ds(q0, SB_TQ), :] = acc_ref[...]
        return 0

    lax.fori_loop(0, nq, q_block, 0)


def _sb(proj):
    b, s, _ = proj.shape
    npair = N_SB_HEADS // 2
    qblk, kblk, vblk = COL_QB // LANES, COL_KB // LANES, COL_VB // LANES
    return pl.pallas_call(
        _sb_kernel,
        grid=(b, npair),
        in_specs=[
            pl.BlockSpec((1, s, LANES), lambda i, p: (i, 0, qblk + p)),
            pl.BlockSpec((1, s, LANES), lambda i, p: (i, 0, kblk + p)),
            pl.BlockSpec((1, s, LANES), lambda i, p: (i, 0, vblk + p)),
        ],
        out_specs=pl.BlockSpec((1, s, LANES), lambda i, p: (i, 0, p)),
        out_shape=jax.ShapeDtypeStruct((b, s, SB_W), jnp.float32),
        scratch_shapes=[pltpu.VMEM((2, SB_TQ, SB_TK), jnp.float32),
                        pltpu.VMEM((SB_TQ, LANES), jnp.float32)],
        compiler_params=pltpu.CompilerParams(
            dimension_semantics=("arbitrary", "arbitrary")),
        name="sb",
    )(proj, proj, proj)


def _out_proj_kernel(x_ref, ya_ref, yb_ref, ga_ref, gb_ref, w_ref, o_ref):
    y = jnp.concatenate([_rms(ya_ref[...], ga_ref[...]), _rms(yb_ref[...], gb_ref[...])],
                        axis=1).astype(jnp.bfloat16)
    o_ref[...] = x_ref[...] + jnp.dot(y, w_ref[...], preferred_element_type=jnp.float32)


def _out_proj(x, ya, yb, ga, gb, w):
    n = x.shape[0]
    return pl.pallas_call(
        _out_proj_kernel,
        grid=(n // OUT_TM,),
        in_specs=[
            pl.BlockSpec((OUT_TM, D_MODEL), lambda i: (i, 0)),
            pl.BlockSpec((OUT_TM, SWA_Q), lambda i: (i, 0)),
            pl.BlockSpec((OUT_TM, SB_W), lambda i: (i, 0)),
            pl.BlockSpec((1, SWA_Q), lambda i: (0, 0)),
            pl.BlockSpec((1, SB_W), lambda i: (0, 0)),
            pl.BlockSpec((SWA_Q + SB_W, D_MODEL), lambda i: (0, 0)),
        ],
        out_specs=pl.BlockSpec((OUT_TM, D_MODEL), lambda i: (i, 0)),
        out_shape=jax.ShapeDtypeStruct((n, D_MODEL), jnp.float32),
        compiler_params=pltpu.CompilerParams(
            dimension_semantics=("arbitrary",),
            vmem_limit_bytes=VMEM_LIMIT),
        name="out_proj",
    )(x, ya, yb, ga, gb, w)


def _prep_ffn_weights(w_gate, w_up, w_down):
    return (w_gate.astype(jnp.bfloat16), w_up.astype(jnp.bfloat16),
            w_down.astype(jnp.bfloat16))


def _prep_w_in(w):
    def twice(cols):
        c = cols.reshape(D_MODEL, N_SWA_KV, 1, HEAD_DIM)
        return jnp.broadcast_to(c, (D_MODEL, N_SWA_KV, 2, HEAD_DIM)).reshape(D_MODEL, -1)
    o1 = SWA_Q
    o2 = o1 + N_SWA_KV * HEAD_DIM
    o3 = o2 + N_SWA_KV * HEAD_DIM
    return jnp.concatenate([w[:, :o1], twice(w[:, o1:o2]), twice(w[:, o2:o3]), w[:, o3:]],
                           axis=1).astype(jnp.bfloat16)


def _proj_col_scale():
    s = jnp.ones((PROJ_W,), jnp.float32)
    s = s.at[:SWA_Q].set(HEAD_DIM ** -0.5 * LOG2E)
    s = s.at[COL_QB:COL_KB].set(HEAD_DIM ** -0.5 * LOG2E)
    return s.reshape(1, PROJ_W)


def kernel(x, ffn1_norm, ffn1_w_gate, ffn1_w_up, ffn1_w_down, mix_norm, w_in, swa_sinks,
           swa_out_norm, sb_out_norm, w_out, ffn2_norm, ffn2_w_gate, ffn2_w_up, ffn2_w_down,
           final_norm):
    b, s, d = x.shape
    depth = ffn1_norm.shape[0]
    xf = x.reshape(b * s, d)
    i = jnp.arange(1, N_SWA_HEADS + 1, dtype=jnp.float32)
    slopes = jnp.exp2(-8.0 * i / N_SWA_HEADS)
    fg = final_norm.reshape(1, d)
    col_scale = _proj_col_scale()
    for l in range(depth):
        wg, wu, wd = _prep_ffn_weights(ffn1_w_gate[l], ffn1_w_up[l], ffn1_w_down[l])
        xf = _ffn(xf, ffn1_norm[l].reshape(1, d), wg, wu, wd, fg, final_norm=False)

        proj = _in_proj(xf, mix_norm[l].reshape(1, d), _prep_w_in(w_in[l]), col_scale)
        proj = proj.reshape(b, s, PROJ_W)
        ya = _swa(proj, slopes, swa_sinks[l])
        yb = _sb(proj)
        xf = _out_proj(xf, ya.reshape(b * s, SWA_Q), yb.reshape(b * s, SB_W),
                       swa_out_norm[l].reshape(1, SWA_Q), sb_out_norm[l].reshape(1, SB_W),
                       w_out[l].astype(jnp.bfloat16))

        wg, wu, wd = _prep_ffn_weights(ffn2_w_gate[l], ffn2_w_up[l], ffn2_w_down[l])
        xf = _ffn(xf, ffn2_norm[l].reshape(1, d), wg, wu, wd, fg,
                  final_norm=(l == depth - 1))
    return xf.reshape(b, s, d)
```

```python
import functools
import math

import jax
import jax.numpy as jnp
from jax import lax
from jax.experimental import pallas as pl
from jax.experimental.pallas import tpu as pltpu

D_MODEL = 2048
HEAD_DIM = 64
N_SWA_HEADS = 16
N_SWA_KV = 4
N_SB_HEADS = 16
WINDOW = 128
BLOCK = 128
D_FF = 5504
EPS = 1e-6
LOG2E = math.log2(math.e)

LANES = 128
FFN_TM = 1024
FFN_TF = 512
PROJ_TM = 1024
PROJ_TN = 1024
OUT_TM = 512
SWA_QB = 4
SB_TQ = 2048
SB_TK = 128
SB_CK = 512
VMEM_LIMIT = 56 * 1024 * 1024

SWA_Q = N_SWA_HEADS * HEAD_DIM
SB_W = N_SB_HEADS * HEAD_DIM
COL_KA = SWA_Q
COL_VA = COL_KA + 2 * N_SWA_KV * HEAD_DIM
COL_QB = COL_VA + 2 * N_SWA_KV * HEAD_DIM
COL_KB = COL_QB + SB_W
COL_VB = COL_KB + SB_W
PROJ_W = COL_VB + SB_W

_NT = (((1,), (1,)), ((), ()))


def _rms(x, g):
    return x * lax.rsqrt(jnp.mean(x * x, axis=-1, keepdims=True) + EPS) * g


def _ffn_kernel(x_ref, g_ref, wg_ref, wu_ref, wd_ref, fg_ref, o_ref, h_ref, *, final_norm):
    j = pl.program_id(1)
    last = pl.cdiv(D_FF, FFN_TF) - 1
    tail = D_FF - last * FFN_TF

    def half_swiglu(h, width):
        gate = jnp.dot(h, wg_ref[:, :width], preferred_element_type=jnp.float32)
        up = jnp.dot(h, wu_ref[:, :width], preferred_element_type=jnp.float32)
        act = (gate * jax.nn.sigmoid(gate) * up * 0.5).astype(jnp.bfloat16)
        return jnp.dot(act, wd_ref[:width, :], preferred_element_type=jnp.float32)

    @pl.when(j == 0)
    def _():
        x = x_ref[...]
        h = _rms(x, g_ref[...]).astype(jnp.bfloat16)
        h_ref[...] = h
        o_ref[...] = x + half_swiglu(h, FFN_TF)

    @pl.when(jnp.logical_and(j > 0, j < last))
    def _():
        o_ref[...] += half_swiglu(h_ref[...], FFN_TF)

    @pl.when(j == last)
    def _():
        y = o_ref[...] + half_swiglu(h_ref[...], tail)
        o_ref[...] = _rms(y, fg_ref[...]) if final_norm else y


def _ffn(x, g, wg, wu, wd, fg, final_norm):
    n = x.shape[0]
    grid = (n // FFN_TM, pl.cdiv(D_FF, FFN_TF))
    return pl.pallas_call(
        functools.partial(_ffn_kernel, final_norm=final_norm),
        grid=grid,
        in_specs=[
            pl.BlockSpec((FFN_TM, D_MODEL), lambda i, j: (i, 0)),
            pl.BlockSpec((1, D_MODEL), lambda i, j: (0, 0)),
            pl.BlockSpec((D_MODEL, FFN_TF), lambda i, j: (0, j)),
            pl.BlockSpec((D_MODEL, FFN_TF), lambda i, j: (0, j)),
            pl.BlockSpec((FFN_TF, D_MODEL), lambda i, j: (j, 0)),
            pl.BlockSpec((1, D_MODEL), lambda i, j: (0, 0)),
        ],
        out_specs=pl.BlockSpec((FFN_TM, D_MODEL), lambda i, j: (i, 0)),
        out_shape=jax.ShapeDtypeStruct((n, D_MODEL), jnp.float32),
        scratch_shapes=[pltpu.VMEM((FFN_TM, D_MODEL), jnp.bfloat16)],
        compiler_params=pltpu.CompilerParams(
            dimension_semantics=("arbitrary", "arbitrary"),
            vmem_limit_bytes=VMEM_LIMIT),
        name="ffn_final" if final_norm else "ffn",
    )(x, g, wg, wu, wd, fg)


def _in_proj_kernel(x_ref, g_ref, w_ref, s_ref, o_ref, h_ref):
    @pl.when(pl.program_id(1) == 0)
    def _():
        h_ref[...] = _rms(x_ref[...], g_ref[...]).astype(jnp.bfloat16)

    acc = jnp.dot(h_ref[...], w_ref[...], preferred_element_type=jnp.float32)
    o_ref[...] = (acc * s_ref[...]).astype(jnp.bfloat16)


def _in_proj(x, g, w, col_scale):
    n = x.shape[0]
    grid = (n // PROJ_TM, PROJ_W // PROJ_TN)
    return pl.pallas_call(
        _in_proj_kernel,
        grid=grid,
        in_specs=[
            pl.BlockSpec((PROJ_TM, D_MODEL), lambda i, j: (i, 0)),
            pl.BlockSpec((1, D_MODEL), lambda i, j: (0, 0)),
            pl.BlockSpec((D_MODEL, PROJ_TN), lambda i, j: (0, j)),
            pl.BlockSpec((1, PROJ_TN), lambda i, j: (0, j)),
        ],
        out_specs=pl.BlockSpec((PROJ_TM, PROJ_TN), lambda i, j: (i, j)),
        out_shape=jax.ShapeDtypeStruct((n, PROJ_W), jnp.bfloat16),
        scratch_shapes=[pltpu.VMEM((PROJ_TM, D_MODEL), jnp.bfloat16)],
        compiler_params=pltpu.CompilerParams(
            dimension_semantics=("arbitrary", "arbitrary"),
            vmem_limit_bytes=VMEM_LIMIT),
        name="in_proj",
    )(x, g, w, col_scale)


def _swa_kernel(slopes_ref, sinks_ref, q_ref, k_ref, v_ref, o_ref, bias_ref):
    g = pl.program_id(1)
    n_heads = N_SWA_HEADS // N_SWA_KV
    nb = q_ref.shape[1] // BLOCK
    row = lax.broadcasted_iota(jnp.int32, (BLOCK, BLOCK), 0)
    col = lax.broadcasted_iota(jnp.int32, (BLOCK, BLOCK), 1)
    low_half = col < HEAD_DIM
    dist_p = (row + BLOCK - col).astype(jnp.float32)
    dist_c = (row - col).astype(jnp.float32)
    for jh in range(n_heads):
        slope2 = slopes_ref[g * n_heads + jh] * LOG2E
        bias_ref[jh, :, :BLOCK] = jnp.where(col > row, -slope2 * dist_p, -jnp.inf)
        bias_ref[jh, :, BLOCK:] = jnp.where(col <= row, -slope2 * dist_c, -jnp.inf)
    ones = jnp.ones((BLOCK, LANES), jnp.bfloat16)
    prev_cols = lax.broadcasted_iota(jnp.int32, (BLOCK, 2 * BLOCK), 1) < BLOCK

    def group(gi, _):
        for t in range(SWA_QB):
            qb = gi * SWA_QB + t
            q0 = pl.multiple_of(qb * BLOCK, BLOCK)
            p0 = pl.multiple_of(jnp.maximum(qb - 1, 0) * BLOCK, BLOCK)
            kk = jnp.concatenate([k_ref[0, pl.ds(p0, BLOCK), :],
                                  k_ref[0, pl.ds(q0, BLOCK), :]], axis=0)
            vv = jnp.concatenate(
                [jnp.concatenate([v_ref[0, pl.ds(p0, BLOCK), :], ones], axis=1),
                 jnp.concatenate([v_ref[0, pl.ds(q0, BLOCK), :], ones], axis=1)], axis=0)
            q = q_ref[0, pl.ds(q0, BLOCK), :]
            outs = []
            for jh in range(n_heads):
                sink2 = sinks_ref[g * n_heads + jh] * LOG2E
                qpair = q[:, (jh // 2) * LANES:(jh // 2 + 1) * LANES]
                keep = low_half if jh % 2 == 0 else jnp.logical_not(low_half)
                qh = jnp.where(keep, qpair, jnp.zeros_like(qpair))
                bias = bias_ref[jh]
                if t == 0:
                    bias = jnp.where(jnp.logical_and(prev_cols, qb == 0), -jnp.inf, bias)
                s = lax.dot_general(qh, kk, _NT, preferred_element_type=jnp.float32) + bias
                m = jnp.maximum(jnp.max(s, axis=-1, keepdims=True), sink2)
                e = jnp.exp2(s - m).astype(jnp.bfloat16)
                pv = jnp.dot(e, vv, preferred_element_type=jnp.float32)
                outs.append(pv[:, :LANES] / (pv[:, LANES:] + jnp.exp2(sink2 - m)))
            o_ref[0, pl.ds(q0, BLOCK), :] = jnp.concatenate(
                [jnp.where(low_half, outs[0], outs[1]), jnp.where(low_half, outs[2], outs[3])],
                axis=1)
        return 0

    lax.fori_loop(0, nb // SWA_QB, group, 0)


def _swa(proj, slopes, sinks):
    b, s, _ = proj.shape
    kblk, vblk = COL_KA // LANES, COL_VA // LANES
    smem = pl.BlockSpec(memory_space=pltpu.SMEM)
    return pl.pallas_call(
        _swa_kernel,
        grid=(b, N_SWA_KV),
        in_specs=[
            smem, smem,
            pl.BlockSpec((1, s, 2 * LANES), lambda i, g: (i, 0, g)),
            pl.BlockSpec((1, s, LANES), lambda i, g: (i, 0, kblk + g)),
            pl.BlockSpec((1, s, LANES), lambda i, g: (i, 0, vblk + g)),
        ],
        out_specs=pl.BlockSpec((1, s, 2 * LANES), lambda i, g: (i, 0, g)),
        out_shape=jax.ShapeDtypeStruct((b, s, SWA_Q), jnp.float32),
        scratch_shapes=[pltpu.VMEM((N_SWA_HEADS // N_SWA_KV, BLOCK, 2 * BLOCK), jnp.float32)],
        compiler_params=pltpu.CompilerParams(
            dimension_semantics=("arbitrary", "arbitrary")),
        name="swa",
    )(slopes, sinks, proj, proj, proj)


def _sb_kernel(q_ref, k_ref, v_ref, o_ref, carry_ref, acc_ref):
    s_len = q_ref.shape[1]
    nq = s_len // SB_TQ
    n_sub = SB_TQ // SB_TK
    r = lax.broadcasted_iota(jnp.int32, (2 * SB_TK, 2 * SB_TK), 0)
    c = lax.broadcasted_iota(jnp.int32, (2 * SB_TK, 2 * SB_TK), 1)
    tri = jnp.where(jnp.logical_and(r >= c, (r < SB_TK) == (c < SB_TK)), 1.0, 0.0
                    ).astype(jnp.bfloat16)
    low_half = lax.broadcasted_iota(jnp.int32, (SB_TK, LANES), 1) < HEAD_DIM
    strict_lower = (lax.broadcasted_iota(jnp.int32, (SB_TK, SB_TK), 1)
                    < lax.broadcasted_iota(jnp.int32, (SB_TK, SB_TK), 0))

    def q_block(qi, _):
        q0 = pl.multiple_of(qi * SB_TQ, SB_TQ)
        carry_ref[...] = jnp.zeros_like(carry_ref)
        acc_ref[...] = jnp.zeros_like(acc_ref)

        def k_block(k0, r0, diag):
            kblk = k_ref[0, pl.ds(k0, SB_TK), :]
            vblk = v_ref[0, pl.ds(k0, SB_TK), :]
            zero = jnp.zeros_like(kblk)
            kk = jnp.concatenate([jnp.where(low_half, kblk, zero),
                                  jnp.where(low_half, zero, kblk)], axis=0)
            vv = jnp.concatenate([jnp.where(low_half, vblk, zero),
                                  jnp.where(low_half, zero, vblk)], axis=0)
            def mask_top(x):
                top = jnp.where(strict_lower, x[:SB_TK], 0.0)
                return top if r0 == SB_TQ - SB_TK else jnp.concatenate([top, x[SB_TK:]], axis=0)

            q = q_ref[0, pl.ds(q0 + r0, SB_TQ - r0), :]
            u2 = lax.dot_general(q, kk, _NT, preferred_element_type=jnp.float32)
            sps = []
            for hd in range(2):
                u = u2[:, hd * SB_TK:(hd + 1) * SB_TK]
                neg_abs = lax.bitcast_convert_type(
                    lax.bitcast_convert_type(u, jnp.int32) | jnp.int32(-2 ** 31), jnp.float32)
                sp = jnp.maximum(u, 0.0) + jnp.log(1.0 + jnp.exp2(neg_abs)) * LOG2E
                if diag:
                    sp = mask_top(sp)
                sps.append(sp.astype(jnp.bfloat16))
            sums = jnp.dot(jnp.concatenate(sps, axis=1), tri,
                           preferred_element_type=jnp.float32)
            a_heads = []
            for hd in range(2):
                u = u2[:, hd * SB_TK:(hd + 1) * SB_TK]
                s_in = sums[:, hd * SB_TK:(hd + 1) * SB_TK]
                carry = carry_ref[hd, r0:, :]
                a = jnp.exp2(u - (carry + s_in))
                if diag:
                    a = mask_top(a)
                carry_ref[hd, r0:, :] = carry + jnp.broadcast_to(s_in[:, 0:1], s_in.shape)
                a_heads.append(a.astype(jnp.bfloat16))
            acc_ref[r0:, :] += jnp.dot(jnp.concatenate(a_heads, axis=1), vv,
                                       preferred_element_type=jnp.float32)

        for d in range(n_sub - 1, -1, -1):
            k_block(pl.multiple_of(q0 + d * SB_TK, SB_TK), d * SB_TK, True)

        def chunk(i, _):
            base = q0 - (i + 1) * SB_CK
            for d in range(SB_CK // SB_TK - 1, -1, -1):
                k_block(pl.multiple_of(base + d * SB_TK, SB_TK), 0, False)
            return 0

        lax.fori_loop(0, qi * (SB_TQ // SB_CK), chunk, 0)
        o_ref[0, pl.ds(q0, SB_TQ), :] = acc_ref[...]
        return 0

    lax.fori_loop(0, nq, q_block, 0)


def _sb(proj):
    b, s, _ = proj.shape
    npair = N_SB_HEADS // 2
    qblk, kblk, vblk = COL_QB // LANES, COL_KB // LANES, COL_VB // LANES
    return pl.pallas_call(
        _sb_kernel,
        grid=(b, npair),
        in_specs=[
            pl.BlockSpec((1, s, LANES), lambda i, p: (i, 0, qblk + p)),
            pl.BlockSpec((1, s, LANES), lambda i, p: (i, 0, kblk + p)),
            pl.BlockSpec((1, s, LANES), lambda i, p: (i, 0, vblk + p)),
        ],
        out_specs=pl.BlockSpec((1, s, LANES), lambda i, p: (i, 0, p)),
        out_shape=jax.ShapeDtypeStruct((b, s, SB_W), jnp.float32),
        scratch_shapes=[pltpu.VMEM((2, SB_TQ, SB_TK), jnp.float32),
                        pltpu.VMEM((SB_TQ, LANES), jnp.float32)],
        compiler_params=pltpu.CompilerParams(
            dimension_semantics=("arbitrary", "arbitrary")),
        name="sb",
    )(proj, proj, proj)


def _out_proj_kernel(x_ref, ya_ref, yb_ref, ga_ref, gb_ref, w_ref, o_ref):
    y = jnp.concatenate([_rms(ya_ref[...], ga_ref[...]), _rms(yb_ref[...], gb_ref[...])],
                        axis=1).astype(jnp.bfloat16)
    o_ref[...] = x_ref[...] + jnp.dot(y, w_ref[...], preferred_element_type=jnp.float32)


def _out_proj(x, ya, yb, ga, gb, w):
    n = x.shape[0]
    return pl.pallas_call(
        _out_proj_kernel,
        grid=(n // OUT_TM,),
        in_specs=[
            pl.BlockSpec((OUT_TM, D_MODEL), lambda i: (i, 0)),
            pl.BlockSpec((OUT_TM, SWA_Q), lambda i: (i, 0)),
            pl.BlockSpec((OUT_TM, SB_W), lambda i: (i, 0)),
            pl.BlockSpec((1, SWA_Q), lambda i: (0, 0)),
            pl.BlockSpec((1, SB_W), lambda i: (0, 0)),
            pl.BlockSpec((SWA_Q + SB_W, D_MODEL), lambda i: (0, 0)),
        ],
        out_specs=pl.BlockSpec((OUT_TM, D_MODEL), lambda i: (i, 0)),
        out_shape=jax.ShapeDtypeStruct((n, D_MODEL), jnp.float32),
        compiler_params=pltpu.CompilerParams(
            dimension_semantics=("arbitrary",),
            vmem_limit_bytes=VMEM_LIMIT),
        name="out_proj",
    )(x, ya, yb, ga, gb, w)


def _prep_ffn_weights(w_gate, w_up, w_down):
    return (w_gate.astype(jnp.bfloat16), w_up.astype(jnp.bfloat16),
            w_down.astype(jnp.bfloat16))


def _prep_w_in(w):
    def twice(cols):
        c = cols.reshape(D_MODEL, N_SWA_KV, 1, HEAD_DIM)
        return jnp.broadcast_to(c, (D_MODEL, N_SWA_KV, 2, HEAD_DIM)).reshape(D_MODEL, -1)
    o1 = SWA_Q
    o2 = o1 + N_SWA_KV * HEAD_DIM
    o3 = o2 + N_SWA_KV * HEAD_DIM
    return jnp.concatenate([w[:, :o1], twice(w[:, o1:o2]), twice(w[:, o2:o3]), w[:, o3:]],
                           axis=1).astype(jnp.bfloat16)


def _proj_col_scale():
    s = jnp.ones((PROJ_W,), jnp.float32)
    s = s.at[:SWA_Q].set(HEAD_DIM ** -0.5 * LOG2E)
    s = s.at[COL_QB:COL_KB].set(HEAD_DIM ** -0.5 * LOG2E)
    return s.reshape(1, PROJ_W)


def kernel(x, ffn1_norm, ffn1_w_gate, ffn1_w_up, ffn1_w_down, mix_norm, w_in, swa_sinks,
           swa_out_norm, sb_out_norm, w_out, ffn2_norm, ffn2_w_gate, ffn2_w_up, ffn2_w_down,
           final_norm):
    b, s, d = x.shape
    depth = ffn1_norm.shape[0]
    xf = x.reshape(b * s, d)
    i = jnp.arange(1, N_SWA_HEADS + 1, dtype=jnp.float32)
    slopes = jnp.exp2(-8.0 * i / N_SWA_HEADS)
    fg = final_norm.reshape(1, d)
    col_scale = _proj_col_scale()
    for l in range(depth):
        wg, wu, wd = _prep_ffn_weights(ffn1_w_gate[l], ffn1_w_up[l], ffn1_w_down[l])
        xf = _ffn(xf, ffn1_norm[l].reshape(1, d), wg, wu, wd, fg, final_norm=False)

        proj = _in_proj(xf, mix_norm[l].reshape(1, d), _prep_w_in(w_in[l]), col_scale)
        proj = proj.reshape(b, s, PROJ_W)
        ya = _swa(proj, slopes, swa_sinks[l])
        yb = _sb(proj)
        xf = _out_proj(xf, ya.reshape(b * s, SWA_Q), yb.reshape(b * s, SB_W),
                       swa_out_norm[l].reshape(1, SWA_Q), sb_out_norm[l].reshape(1, SB_W),
                       w_out[l].astype(jnp.bfloat16))

        wg, wu, wd = _prep_ffn_weights(ffn2_w_gate[l], ffn2_w_up[l], ffn2_w_down[l])
        xf = _ffn(xf, ffn2_norm[l].reshape(1, d), wg, wu, wd, fg,
                  final_norm=(l == depth - 1))
    return xf.reshape(b, s, d)
```

```python
import functools
import math

import jax
import jax.numpy as jnp
from jax import lax
from jax.experimental import pallas as pl
from jax.experimental.pallas import tpu as pltpu

D_MODEL = 2048
HEAD_DIM = 64
N_SWA_HEADS = 16
N_SWA_KV = 4
N_SB_HEADS = 16
WINDOW = 128
BLOCK = 128
D_FF = 5504
EPS = 1e-6
LOG2E = math.log2(math.e)

LANES = 128
FFN_TM = 1024
FFN_TF = 512
PROJ_TM = 1024
PROJ_TN = 1024
OUT_TM = 512
SWA_QB = 16
SB_TQ = 2048
SB_TK = 128
SB_CK = 512
VMEM_LIMIT = 56 * 1024 * 1024

SWA_Q = N_SWA_HEADS * HEAD_DIM
SB_W = N_SB_HEADS * HEAD_DIM
COL_KA = SWA_Q
COL_VA = COL_KA + 2 * N_SWA_KV * HEAD_DIM
COL_QB = COL_VA + 2 * N_SWA_KV * HEAD_DIM
COL_KB = COL_QB + SB_W
COL_VB = COL_KB + SB_W
PROJ_W = COL_VB + SB_W

_NT = (((1,), (1,)), ((), ()))


def _rms(x, g):
    return x * lax.rsqrt(jnp.mean(x * x, axis=-1, keepdims=True) + EPS) * g


def _ffn_kernel(x_ref, g_ref, wg_ref, wu_ref, wd_ref, fg_ref, o_ref, h_ref, *, final_norm):
    j = pl.program_id(1)
    last = pl.cdiv(D_FF, FFN_TF) - 1
    tail = D_FF - last * FFN_TF

    def half_swiglu(h, width):
        gate = jnp.dot(h, wg_ref[:, :width], preferred_element_type=jnp.float32)
        up = jnp.dot(h, wu_ref[:, :width], preferred_element_type=jnp.float32)
        act = (gate * jax.nn.sigmoid(gate) * up * 0.5).astype(jnp.bfloat16)
        return jnp.dot(act, wd_ref[:width, :], preferred_element_type=jnp.float32)

    @pl.when(j == 0)
    def _():
        x = x_ref[...]
        h = _rms(x, g_ref[...]).astype(jnp.bfloat16)
        h_ref[...] = h
        o_ref[...] = x + half_swiglu(h, FFN_TF)

    @pl.when(jnp.logical_and(j > 0, j < last))
    def _():
        o_ref[...] += half_swiglu(h_ref[...], FFN_TF)

    @pl.when(j == last)
    def _():
        y = o_ref[...] + half_swiglu(h_ref[...], tail)
        o_ref[...] = _rms(y, fg_ref[...]) if final_norm else y


def _ffn(x, g, wg, wu, wd, fg, final_norm):
    n = x.shape[0]
    grid = (n // FFN_TM, pl.cdiv(D_FF, FFN_TF))
    return pl.pallas_call(
        functools.partial(_ffn_kernel, final_norm=final_norm),
        grid=grid,
        in_specs=[
            pl.BlockSpec((FFN_TM, D_MODEL), lambda i, j: (i, 0)),
            pl.BlockSpec((1, D_MODEL), lambda i, j: (0, 0)),
            pl.BlockSpec((D_MODEL, FFN_TF), lambda i, j: (0, j)),
            pl.BlockSpec((D_MODEL, FFN_TF), lambda i, j: (0, j)),
            pl.BlockSpec((FFN_TF, D_MODEL), lambda i, j: (j, 0)),
            pl.BlockSpec((1, D_MODEL), lambda i, j: (0, 0)),
        ],
        out_specs=pl.BlockSpec((FFN_TM, D_MODEL), lambda i, j: (i, 0)),
        out_shape=jax.ShapeDtypeStruct((n, D_MODEL), jnp.float32),
        scratch_shapes=[pltpu.VMEM((FFN_TM, D_MODEL), jnp.bfloat16)],
        compiler_params=pltpu.CompilerParams(
            dimension_semantics=("arbitrary", "arbitrary"),
            vmem_limit_bytes=VMEM_LIMIT),
        name="ffn_final" if final_norm else "ffn",
    )(x, g, wg, wu, wd, fg)


def _in_proj_kernel(x_ref, g_ref, w_ref, s_ref, o_ref, h_ref):
    def project(h):
        acc = jnp.dot(h, w_ref[...], preferred_element_type=jnp.float32)
        return (acc * s_ref[...]).astype(jnp.bfloat16)

    @pl.when(pl.program_id(1) == 0)
    def _():
        h = _rms(x_ref[...], g_ref[...]).astype(jnp.bfloat16)
        h_ref[...] = h
        o_ref[...] = project(h)

    @pl.when(pl.program_id(1) > 0)
    def _():
        o_ref[...] = project(h_ref[...])


def _in_proj(x, g, w, col_scale):
    n = x.shape[0]
    grid = (n // PROJ_TM, PROJ_W // PROJ_TN)
    return pl.pallas_call(
        _in_proj_kernel,
        grid=grid,
        in_specs=[
            pl.BlockSpec((PROJ_TM, D_MODEL), lambda i, j: (i, 0)),
            pl.BlockSpec((1, D_MODEL), lambda i, j: (0, 0)),
            pl.BlockSpec((D_MODEL, PROJ_TN), lambda i, j: (0, j)),
            pl.BlockSpec((1, PROJ_TN), lambda i, j: (0, j)),
        ],
        out_specs=pl.BlockSpec((PROJ_TM, PROJ_TN), lambda i, j: (i, j)),
        out_shape=jax.ShapeDtypeStruct((n, PROJ_W), jnp.bfloat16),
        scratch_shapes=[pltpu.VMEM((PROJ_TM, D_MODEL), jnp.bfloat16)],
        compiler_params=pltpu.CompilerParams(
            dimension_semantics=("arbitrary", "arbitrary"),
            vmem_limit_bytes=VMEM_LIMIT),
        name="in_proj",
    )(x, g, w, col_scale)


def _swa_kernel(slopes_ref, sinks_ref, q_ref, k_ref, v_ref, o_ref, bias_ref):
    g = pl.program_id(1)
    n_heads = N_SWA_HEADS // N_SWA_KV
    nb = q_ref.shape[1] // BLOCK
    row = lax.broadcasted_iota(jnp.int32, (BLOCK, BLOCK), 0)
    col = lax.broadcasted_iota(jnp.int32, (BLOCK, BLOCK), 1)
    low_half = col < HEAD_DIM
    dist_p = (row + BLOCK - col).astype(jnp.float32)
    dist_c = (row - col).astype(jnp.float32)
    for jh in range(n_heads):
        slope2 = slopes_ref[g * n_heads + jh] * LOG2E
        bias_ref[jh, :, :BLOCK] = jnp.where(col > row, -slope2 * dist_p, -jnp.inf)
        bias_ref[jh, :, BLOCK:] = jnp.where(col <= row, -slope2 * dist_c, -jnp.inf)
    ones = jnp.ones((BLOCK, LANES), jnp.bfloat16)
    prev_cols = lax.broadcasted_iota(jnp.int32, (BLOCK, 2 * BLOCK), 1) < BLOCK

    def group(gi, _):
        for t in range(SWA_QB):
            qb = gi * SWA_QB + t
            q0 = pl.multiple_of(qb * BLOCK, BLOCK)
            p0 = pl.multiple_of(jnp.maximum(qb - 1, 0) * BLOCK, BLOCK)
            kk = jnp.concatenate([k_ref[0, pl.ds(p0, BLOCK), :],
                                  k_ref[0, pl.ds(q0, BLOCK), :]], axis=0)
            vv = jnp.concatenate(
                [jnp.concatenate([v_ref[0, pl.ds(p0, BLOCK), :], ones], axis=1),
                 jnp.concatenate([v_ref[0, pl.ds(q0, BLOCK), :], ones], axis=1)], axis=0)
            q = q_ref[0, pl.ds(q0, BLOCK), :]
            outs = []
            for jh in range(n_heads):
                sink2 = sinks_ref[g * n_heads + jh] * LOG2E
                qpair = q[:, (jh // 2) * LANES:(jh // 2 + 1) * LANES]
                keep = low_half if jh % 2 == 0 else jnp.logical_not(low_half)
                qh = jnp.where(keep, qpair, jnp.zeros_like(qpair))
                bias = bias_ref[jh]
                if t == 0:
                    bias = jnp.where(jnp.logical_and(prev_cols, qb == 0), -jnp.inf, bias)
                s = lax.dot_general(qh, kk, _NT, preferred_element_type=jnp.float32) + bias
                m = jnp.maximum(jnp.max(s, axis=-1, keepdims=True), sink2)
                e = jnp.exp2(s - m).astype(jnp.bfloat16)
                pv = jnp.dot(e, vv, preferred_element_type=jnp.float32)
                outs.append(pv[:, :LANES] / (pv[:, LANES:] + jnp.exp2(sink2 - m)))
            o_ref[0, pl.ds(q0, BLOCK), :] = jnp.concatenate(
                [jnp.where(low_half, outs[0], outs[1]), jnp.where(low_half, outs[2], outs[3])],
                axis=1)
        return 0

    lax.fori_loop(0, nb // SWA_QB, group, 0)


def _swa(proj, slopes, sinks):
    b, s, _ = proj.shape
    kblk, vblk = COL_KA // LANES, COL_VA // LANES
    smem = pl.BlockSpec(memory_space=pltpu.SMEM)
    return pl.pallas_call(
        _swa_kernel,
        grid=(b, N_SWA_KV),
        in_specs=[
            smem, smem,
            pl.BlockSpec((1, s, 2 * LANES), lambda i, g: (i, 0, g)),
            pl.BlockSpec((1, s, LANES), lambda i, g: (i, 0, kblk + g)),
            pl.BlockSpec((1, s, LANES), lambda i, g: (i, 0, vblk + g)),
        ],
        out_specs=pl.BlockSpec((1, s, 2 * LANES), lambda i, g: (i, 0, g)),
        out_shape=jax.ShapeDtypeStruct((b, s, SWA_Q), jnp.float32),
        scratch_shapes=[pltpu.VMEM((N_SWA_HEADS // N_SWA_KV, BLOCK, 2 * BLOCK), jnp.float32)],
        compiler_params=pltpu.CompilerParams(
            dimension_semantics=("arbitrary", "arbitrary")),
        name="swa",
    )(slopes, sinks, proj, proj, proj)


def _sb_kernel(q_ref, k_ref, v_ref, o_ref, carry_ref, acc_ref):
    s_len = q_ref.shape[1]
    nq = s_len // SB_TQ
    n_sub = SB_TQ // SB_TK
    r = lax.broadcasted_iota(jnp.int32, (2 * SB_TK, 2 * SB_TK), 0)
    c = lax.broadcasted_iota(jnp.int32, (2 * SB_TK, 2 * SB_TK), 1)
    tri = jnp.where(jnp.logical_and(r >= c, (r < SB_TK) == (c < SB_TK)), 1.0, 0.0
                    ).astype(jnp.bfloat16)
    low_half = lax.broadcasted_iota(jnp.int32, (SB_TK, LANES), 1) < HEAD_DIM
    strict_lower = (lax.broadcasted_iota(jnp.int32, (SB_TK, SB_TK), 1)
                    < lax.broadcasted_iota(jnp.int32, (SB_TK, SB_TK), 0))

    def q_block(qi, _):
        q0 = pl.multiple_of(qi * SB_TQ, SB_TQ)
        carry_ref[...] = jnp.zeros_like(carry_ref)
        acc_ref[...] = jnp.zeros_like(acc_ref)

        def k_block(k0, r0, diag):
            kblk = k_ref[0, pl.ds(k0, SB_TK), :]
            vblk = v_ref[0, pl.ds(k0, SB_TK), :]
            zero = jnp.zeros_like(kblk)
            kk = jnp.concatenate([jnp.where(low_half, kblk, zero),
                                  jnp.where(low_half, zero, kblk)], axis=0)
            vv = jnp.concatenate([jnp.where(low_half, vblk, zero),
                                  jnp.where(low_half, zero, vblk)], axis=0)
            def mask_top(x):
                top = jnp.where(strict_lower, x[:SB_TK], 0.0)
                return top if r0 == SB_TQ - SB_TK else jnp.concatenate([top, x[SB_TK:]], axis=0)

            q = q_ref[0, pl.ds(q0 + r0, SB_TQ - r0), :]
            u2 = lax.dot_general(q, kk, _NT, preferred_element_type=jnp.float32)
            sps = []
            for hd in range(2):
                u = u2[:, hd * SB_TK:(hd + 1) * SB_TK]
                neg_abs = lax.bitcast_convert_type(
                    lax.bitcast_convert_type(u, jnp.int32) | jnp.int32(-2 ** 31), jnp.float32)
                sp = jnp.maximum(u, 0.0) + jnp.log(1.0 + jnp.exp2(neg_abs)) * LOG2E
                if diag:
                    sp = mask_top(sp)
                sps.append(sp.astype(jnp.bfloat16))
            sums = jnp.dot(jnp.concatenate(sps, axis=1), tri,
                           preferred_element_type=jnp.float32)
            a_heads = []
            for hd in range(2):
                u = u2[:, hd * SB_TK:(hd + 1) * SB_TK]
                s_in = sums[:, hd * SB_TK:(hd + 1) * SB_TK]
                carry = carry_ref[hd, r0:, :]
                a = jnp.exp2(u - (carry + s_in))
                if diag:
                    a = mask_top(a)
                carry_ref[hd, r0:, :] = carry + jnp.broadcast_to(s_in[:, 0:1], s_in.shape)
                a_heads.append(a.astype(jnp.bfloat16))
            acc_ref[r0:, :] += jnp.dot(jnp.concatenate(a_heads, axis=1), vv,
                                       preferred_element_type=jnp.float32)

        for d in range(n_sub - 1, -1, -1):
            k_block(pl.multiple_of(q0 + d * SB_TK, SB_TK), d * SB_TK, True)

        def chunk(i, _):
            base = q0 - (i + 1) * SB_CK
            for d in range(SB_CK // SB_TK - 1, -1, -1):
                k_block(pl.multiple_of(base + d * SB_TK, SB_TK), 0, False)
            return 0

        lax.fori_loop(0, qi * (SB_TQ // SB_CK), chunk, 0)
        o_ref[0, pl.ds(q0, SB_TQ), :] = acc_ref[...]
        return 0

    lax.fori_loop(0, nq, q_block, 0)


def _sb(proj):
    b, s, _ = proj.shape
    npair = N_SB_HEADS // 2
    qblk, kblk, vblk = COL_QB // LANES, COL_KB // LANES, COL_VB // LANES
    return pl.pallas_call(
        _sb_kernel,
        grid=(b, npair),
        in_specs=[
            pl.BlockSpec((1, s, LANES), lambda i, p: (i, 0, qblk + p)),
            pl.BlockSpec((1, s, LANES), lambda i, p: (i, 0, kblk + p)),
            pl.BlockSpec((1, s, LANES), lambda i, p: (i, 0, vblk + p)),
        ],
        out_specs=pl.BlockSpec((1, s, LANES), lambda i, p: (i, 0, p)),
        out_shape=jax.ShapeDtypeStruct((b, s, SB_W), jnp.float32),
        scratch_shapes=[pltpu.VMEM((2, SB_TQ, SB_TK), jnp.float32),
                        pltpu.VMEM((SB_TQ, LANES), jnp.float32)],
        compiler_params=pltpu.CompilerParams(
            dimension_semantics=("arbitrary", "arbitrary")),
        name="sb",
    )(proj, proj, proj)


def _out_proj_kernel(x_ref, ya_ref, yb_ref, ga_ref, gb_ref, w_ref, o_ref):
    y = jnp.concatenate([_rms(ya_ref[...], ga_ref[...]), _rms(yb_ref[...], gb_ref[...])],
                        axis=1).astype(jnp.bfloat16)
    o_ref[...] = x_ref[...] + jnp.dot(y, w_ref[...], preferred_element_type=jnp.float32)


def _out_proj(x, ya, yb, ga, gb, w):
    n = x.shape[0]
    return pl.pallas_call(
        _out_proj_kernel,
        grid=(n // OUT_TM,),
        in_specs=[
            pl.BlockSpec((OUT_TM, D_MODEL), lambda i: (i, 0)),
            pl.BlockSpec((OUT_TM, SWA_Q), lambda i: (i, 0)),
            pl.BlockSpec((OUT_TM, SB_W), lambda i: (i, 0)),
            pl.BlockSpec((1, SWA_Q), lambda i: (0, 0)),
            pl.BlockSpec((1, SB_W), lambda i: (0, 0)),
            pl.BlockSpec((SWA_Q + SB_W, D_MODEL), lambda i: (0, 0)),
        ],
        out_specs=pl.BlockSpec((OUT_TM, D_MODEL), lambda i: (i, 0)),
        out_shape=jax.ShapeDtypeStruct((n, D_MODEL), jnp.float32),
        compiler_params=pltpu.CompilerParams(
            dimension_semantics=("arbitrary",),
            vmem_limit_bytes=VMEM_LIMIT),
        name="out_proj",
    )(x, ya, yb, ga, gb, w)


def _prep_ffn_weights(w_gate, w_up, w_down):
    return (w_gate.astype(jnp.bfloat16), w_up.astype(jnp.bfloat16),
            w_down.astype(jnp.bfloat16))


def _prep_w_in(w):
    def twice(cols):
        c = cols.reshape(D_MODEL, N_SWA_KV, 1, HEAD_DIM)
        return jnp.broadcast_to(c, (D_MODEL, N_SWA_KV, 2, HEAD_DIM)).reshape(D_MODEL, -1)
    o1 = SWA_Q
    o2 = o1 + N_SWA_KV * HEAD_DIM
    o3 = o2 + N_SWA_KV * HEAD_DIM
    return jnp.concatenate([w[:, :o1], twice(w[:, o1:o2]), twice(w[:, o2:o3]), w[:, o3:]],
                           axis=1).astype(jnp.bfloat16)


def _proj_col_scale():
    s = jnp.ones((PROJ_W,), jnp.float32)
    s = s.at[:SWA_Q].set(HEAD_DIM ** -0.5 * LOG2E)
    s = s.at[COL_QB:COL_KB].set(HEAD_DIM ** -0.5 * LOG2E)
    return s.reshape(1, PROJ_W)


def kernel(x, ffn1_norm, ffn1_w_gate, ffn1_w_up, ffn1_w_down, mix_norm, w_in, swa_sinks,
           swa_out_norm, sb_out_norm, w_out, ffn2_norm, ffn2_w_gate, ffn2_w_up, ffn2_w_down,
           final_norm):
    b, s, d = x.shape
    depth = ffn1_norm.shape[0]
    xf = x.reshape(b * s, d)
    i = jnp.arange(1, N_SWA_HEADS + 1, dtype=jnp.float32)
    slopes = jnp.exp2(-8.0 * i / N_SWA_HEADS)
    fg = final_norm.reshape(1, d)
    col_scale = _proj_col_scale()
    for l in range(depth):
        wg, wu, wd = _prep_ffn_weights(ffn1_w_gate[l], ffn1_w_up[l], ffn1_w_down[l])
        xf = _ffn(xf, ffn1_norm[l].reshape(1, d), wg, wu, wd, fg, final_norm=False)

        proj = _in_proj(xf, mix_norm[l].reshape(1, d), _prep_w_in(w_in[l]), col_scale)
        proj = proj.reshape(b, s, PROJ_W)
        ya = _swa(proj, slopes, swa_sinks[l])
        yb = _sb(proj)
        xf = _out_proj(xf, ya.reshape(b * s, SWA_Q), yb.reshape(b * s, SB_W),
                       swa_out_norm[l].reshape(1, SWA_Q), sb_out_norm[l].reshape(1, SB_W),
                       w_out[l].astype(jnp.bfloat16))

        wg, wu, wd = _prep_ffn_weights(ffn2_w_gate[l], ffn2_w_up[l], ffn2_w_down[l])
        xf = _ffn(xf, ffn2_norm[l].reshape(1, d), wg, wu, wd, fg,
                  final_norm=(l == depth - 1))
    return xf.reshape(b, s, d)
```

```python
import functools
import math

import jax
import jax.numpy as jnp
from jax import lax
from jax.experimental import pallas as pl
from jax.experimental.pallas import tpu as pltpu

D_MODEL = 2048
HEAD_DIM = 64
N_SWA_HEADS = 16
N_SWA_KV = 4
N_SB_HEADS = 16
WINDOW = 128
BLOCK = 128
D_FF = 5504
EPS = 1e-6
LOG2E = math.log2(math.e)

LANES = 128
FFN_TM = 1024
FFN_TF = 512
PROJ_TM = 1024
PROJ_TN = 2560
OUT_TM = 512
SWA_QB = 16
SB_TQ = 2048
SB_TK = 128
SB_CK = 512
VMEM_LIMIT = 56 * 1024 * 1024

SWA_Q = N_SWA_HEADS * HEAD_DIM
SB_W = N_SB_HEADS * HEAD_DIM
COL_KA = SWA_Q
COL_VA = COL_KA + 2 * N_SWA_KV * HEAD_DIM
COL_QB = COL_VA + 2 * N_SWA_KV * HEAD_DIM
COL_KB = COL_QB + SB_W
COL_VB = COL_KB + SB_W
PROJ_W = COL_VB + SB_W

_NT = (((1,), (1,)), ((), ()))


def _rms(x, g):
    return x * lax.rsqrt(jnp.mean(x * x, axis=-1, keepdims=True) + EPS) * g


def _ffn_kernel(x_ref, g_ref, wg_ref, wu_ref, wd_ref, fg_ref, o_ref, h_ref, *, final_norm):
    j = pl.program_id(1)
    last = pl.cdiv(D_FF, FFN_TF) - 1
    tail = D_FF - last * FFN_TF

    def half_swiglu(h, width):
        gate = jnp.dot(h, wg_ref[:, :width], preferred_element_type=jnp.float32)
        up = jnp.dot(h, wu_ref[:, :width], preferred_element_type=jnp.float32)
        act = (gate * jax.nn.sigmoid(gate) * up * 0.5).astype(jnp.bfloat16)
        return jnp.dot(act, wd_ref[:width, :], preferred_element_type=jnp.float32)

    @pl.when(j == 0)
    def _():
        x = x_ref[...]
        h = _rms(x, g_ref[...]).astype(jnp.bfloat16)
        h_ref[...] = h
        o_ref[...] = x + half_swiglu(h, FFN_TF)

    @pl.when(jnp.logical_and(j > 0, j < last))
    def _():
        o_ref[...] += half_swiglu(h_ref[...], FFN_TF)

    @pl.when(j == last)
    def _():
        y = o_ref[...] + half_swiglu(h_ref[...], tail)
        o_ref[...] = _rms(y, fg_ref[...]) if final_norm else y


def _ffn(x, g, wg, wu, wd, fg, final_norm):
    n = x.shape[0]
    grid = (n // FFN_TM, pl.cdiv(D_FF, FFN_TF))
    return pl.pallas_call(
        functools.partial(_ffn_kernel, final_norm=final_norm),
        grid=grid,
        in_specs=[
            pl.BlockSpec((FFN_TM, D_MODEL), lambda i, j: (i, 0)),
            pl.BlockSpec((1, D_MODEL), lambda i, j: (0, 0)),
            pl.BlockSpec((D_MODEL, FFN_TF), lambda i, j: (0, j)),
            pl.BlockSpec((D_MODEL, FFN_TF), lambda i, j: (0, j)),
            pl.BlockSpec((FFN_TF, D_MODEL), lambda i, j: (j, 0)),
            pl.BlockSpec((1, D_MODEL), lambda i, j: (0, 0)),
        ],
        out_specs=pl.BlockSpec((FFN_TM, D_MODEL), lambda i, j: (i, 0)),
        out_shape=jax.ShapeDtypeStruct((n, D_MODEL), jnp.float32),
        scratch_shapes=[pltpu.VMEM((FFN_TM, D_MODEL), jnp.bfloat16)],
        compiler_params=pltpu.CompilerParams(
            dimension_semantics=("arbitrary", "arbitrary"),
            vmem_limit_bytes=VMEM_LIMIT),
        name="ffn_final" if final_norm else "ffn",
    )(x, g, wg, wu, wd, fg)


def _in_proj_kernel(x_ref, g_ref, w_ref, s_ref, o_ref, h_ref):
    def project(h):
        acc = jnp.dot(h, w_ref[...], preferred_element_type=jnp.float32)
        return (acc * s_ref[...]).astype(jnp.bfloat16)

    @pl.when(pl.program_id(1) == 0)
    def _():
        h = _rms(x_ref[...], g_ref[...]).astype(jnp.bfloat16)
        h_ref[...] = h
        o_ref[...] = project(h)

    @pl.when(pl.program_id(1) > 0)
    def _():
        o_ref[...] = project(h_ref[...])


def _in_proj(x, g, w, col_scale):
    n = x.shape[0]
    grid = (n // PROJ_TM, PROJ_W // PROJ_TN)
    return pl.pallas_call(
        _in_proj_kernel,
        grid=grid,
        in_specs=[
            pl.BlockSpec((PROJ_TM, D_MODEL), lambda i, j: (i, 0)),
            pl.BlockSpec((1, D_MODEL), lambda i, j: (0, 0)),
            pl.BlockSpec((D_MODEL, PROJ_TN), lambda i, j: (0, j)),
            pl.BlockSpec((1, PROJ_TN), lambda i, j: (0, j)),
        ],
        out_specs=pl.BlockSpec((PROJ_TM, PROJ_TN), lambda i, j: (i, j)),
        out_shape=jax.ShapeDtypeStruct((n, PROJ_W), jnp.bfloat16),
        scratch_shapes=[pltpu.VMEM((PROJ_TM, D_MODEL), jnp.bfloat16)],
        compiler_params=pltpu.CompilerParams(
            dimension_semantics=("arbitrary", "arbitrary"),
            vmem_limit_bytes=VMEM_LIMIT),
        name="in_proj",
    )(x, g, w, col_scale)


def _swa_kernel(slopes_ref, sinks_ref, q_ref, k_ref, v_ref, o_ref, bias_ref):
    g = pl.program_id(1)
    n_heads = N_SWA_HEADS // N_SWA_KV
    nb = q_ref.shape[1] // BLOCK
    row = lax.broadcasted_iota(jnp.int32, (BLOCK, BLOCK), 0)
    col = lax.broadcasted_iota(jnp.int32, (BLOCK, BLOCK), 1)
    low_half = col < HEAD_DIM
    dist_p = (row + BLOCK - col).astype(jnp.float32)
    dist_c = (row - col).astype(jnp.float32)
    for jh in range(n_heads):
        slope2 = slopes_ref[g * n_heads + jh] * LOG2E
        bias_ref[jh, :, :BLOCK] = jnp.where(col > row, -slope2 * dist_p, -jnp.inf)
        bias_ref[jh, :, BLOCK:] = jnp.where(col <= row, -slope2 * dist_c, -jnp.inf)
    ones = jnp.ones((BLOCK, LANES), jnp.bfloat16)
    prev_cols = lax.broadcasted_iota(jnp.int32, (BLOCK, 2 * BLOCK), 1) < BLOCK

    def group(gi, _):
        for t in range(SWA_QB):
            qb = gi * SWA_QB + t
            q0 = pl.multiple_of(qb * BLOCK, BLOCK)
            p0 = pl.multiple_of(jnp.maximum(qb - 1, 0) * BLOCK, BLOCK)
            kk = jnp.concatenate([k_ref[0, pl.ds(p0, BLOCK), :],
                                  k_ref[0, pl.ds(q0, BLOCK), :]], axis=0)
            vv = jnp.concatenate(
                [jnp.concatenate([v_ref[0, pl.ds(p0, BLOCK), :], ones], axis=1),
                 jnp.concatenate([v_ref[0, pl.ds(q0, BLOCK), :], ones], axis=1)], axis=0)
            q = q_ref[0, pl.ds(q0, BLOCK), :]
            outs = []
            for jh in range(n_heads):
                sink2 = sinks_ref[g * n_heads + jh] * LOG2E
                qpair = q[:, (jh // 2) * LANES:(jh // 2 + 1) * LANES]
                keep = low_half if jh % 2 == 0 else jnp.logical_not(low_half)
                qh = jnp.where(keep, qpair, jnp.zeros_like(qpair))
                bias = bias_ref[jh]
                if t == 0:
                    bias = jnp.where(jnp.logical_and(prev_cols, qb == 0), -jnp.inf, bias)
                s = lax.dot_general(qh, kk, _NT, preferred_element_type=jnp.float32) + bias
                m = jnp.maximum(jnp.max(s, axis=-1, keepdims=True), sink2)
                e = jnp.exp2(s - m).astype(jnp.bfloat16)
                pv = jnp.dot(e, vv, preferred_element_type=jnp.float32)
                outs.append(pv[:, :LANES] / (pv[:, LANES:] + jnp.exp2(sink2 - m)))
            o_ref[0, pl.ds(q0, BLOCK), :] = jnp.concatenate(
                [jnp.where(low_half, outs[0], outs[1]), jnp.where(low_half, outs[2], outs[3])],
                axis=1)
        return 0

    lax.fori_loop(0, nb // SWA_QB, group, 0)


def _swa(proj, slopes, sinks):
    b, s, _ = proj.shape
    kblk, vblk = COL_KA // LANES, COL_VA // LANES
    smem = pl.BlockSpec(memory_space=pltpu.SMEM)
    return pl.pallas_call(
        _swa_kernel,
        grid=(b, N_SWA_KV),
        in_specs=[
            smem, smem,
            pl.BlockSpec((1, s, 2 * LANES), lambda i, g: (i, 0, g)),
            pl.BlockSpec((1, s, LANES), lambda i, g: (i, 0, kblk + g)),
            pl.BlockSpec((1, s, LANES), lambda i, g: (i, 0, vblk + g)),
        ],
        out_specs=pl.BlockSpec((1, s, 2 * LANES), lambda i, g: (i, 0, g)),
        out_shape=jax.ShapeDtypeStruct((b, s, SWA_Q), jnp.float32),
        scratch_shapes=[pltpu.VMEM((N_SWA_HEADS // N_SWA_KV, BLOCK, 2 * BLOCK), jnp.float32)],
        compiler_params=pltpu.CompilerParams(
            dimension_semantics=("arbitrary", "arbitrary")),
        name="swa",
    )(slopes, sinks, proj, proj, proj)


def _sb_kernel(q_ref, k_ref, v_ref, o_ref, carry_ref, acc_ref):
    s_len = q_ref.shape[1]
    nq = s_len // SB_TQ
    n_sub = SB_TQ // SB_TK
    r = lax.broadcasted_iota(jnp.int32, (2 * SB_TK, 2 * SB_TK), 0)
    c = lax.broadcasted_iota(jnp.int32, (2 * SB_TK, 2 * SB_TK), 1)
    tri = jnp.where(jnp.logical_and(r >= c, (r < SB_TK) == (c < SB_TK)), 1.0, 0.0
                    ).astype(jnp.bfloat16)
    low_half = lax.broadcasted_iota(jnp.int32, (SB_TK, LANES), 1) < HEAD_DIM
    strict_lower = (lax.broadcasted_iota(jnp.int32, (SB_TK, SB_TK), 1)
                    < lax.broadcasted_iota(jnp.int32, (SB_TK, SB_TK), 0))

    def q_block(qi, _):
        q0 = pl.multiple_of(qi * SB_TQ, SB_TQ)
        carry_ref[...] = jnp.zeros_like(carry_ref)
        acc_ref[...] = jnp.zeros_like(acc_ref)

        def k_block(k0, r0, diag):
            kblk = k_ref[0, pl.ds(k0, SB_TK), :]
            vblk = v_ref[0, pl.ds(k0, SB_TK), :]
            zero = jnp.zeros_like(kblk)
            kk = jnp.concatenate([jnp.where(low_half, kblk, zero),
                                  jnp.where(low_half, zero, kblk)], axis=0)
            vv = jnp.concatenate([jnp.where(low_half, vblk, zero),
                                  jnp.where(low_half, zero, vblk)], axis=0)
            def mask_top(x):
                top = jnp.where(strict_lower, x[:SB_TK], 0.0)
                return top if r0 == SB_TQ - SB_TK else jnp.concatenate([top, x[SB_TK:]], axis=0)

            q = q_ref[0, pl.ds(q0 + r0, SB_TQ - r0), :]
            u2 = lax.dot_general(q, kk, _NT, preferred_element_type=jnp.float32)
            sps = []
            for hd in range(2):
                u = u2[:, hd * SB_TK:(hd + 1) * SB_TK]
                neg_abs = lax.bitcast_convert_type(
                    lax.bitcast_convert_type(u, jnp.int32) | jnp.int32(-2 ** 31), jnp.float32)
                sp = jnp.maximum(u, 0.0) + jnp.log(1.0 + jnp.exp2(neg_abs)) * LOG2E
                if diag:
                    sp = mask_top(sp)
                sps.append(sp.astype(jnp.bfloat16))
            sums = jnp.dot(jnp.concatenate(sps, axis=1), tri,
                           preferred_element_type=jnp.float32)
            a_heads = []
            for hd in range(2):
                u = u2[:, hd * SB_TK:(hd + 1) * SB_TK]
                s_in = sums[:, hd * SB_TK:(hd + 1) * SB_TK]
                carry = carry_ref[hd, r0:, :]
                a = jnp.exp2(u - (carry + s_in))
                if diag:
                    a = mask_top(a)
                carry_ref[hd, r0:, :] = carry + jnp.broadcast_to(s_in[:, 0:1], s_in.shape)
                a_heads.append(a.astype(jnp.bfloat16))
            acc_ref[r0:, :] += jnp.dot(jnp.concatenate(a_heads, axis=1), vv,
                                       preferred_element_type=jnp.float32)

        for d in range(n_sub - 1, -1, -1):
            k_block(pl.multiple_of(q0 + d * SB_TK, SB_TK), d * SB_TK, True)

        def chunk(i, _):
            base = q0 - (i + 1) * SB_CK
            for d in range(SB_CK // SB_TK - 1, -1, -1):
                k_block(pl.multiple_of(base + d * SB_TK, SB_TK), 0, False)
            return 0

        lax.fori_loop(0, qi * (SB_TQ // SB_CK), chunk, 0)
        o_ref[0, pl.ds(q0, SB_TQ), :] = acc_ref[...]
        return 0

    lax.fori_loop(0, nq, q_block, 0)


def _sb(proj):
    b, s, _ = proj.shape
    npair = N_SB_HEADS // 2
    qblk, kblk, vblk = COL_QB // LANES, COL_KB // LANES, COL_VB // LANES
    return pl.pallas_call(
        _sb_kernel,
        grid=(b, npair),
        in_specs=[
            pl.BlockSpec((1, s, LANES), lambda i, p: (i, 0, qblk + p)),
            pl.BlockSpec((1, s, LANES), lambda i, p: (i, 0, kblk + p)),
            pl.BlockSpec((1, s, LANES), lambda i, p: (i, 0, vblk + p)),
        ],
        out_specs=pl.BlockSpec((1, s, LANES), lambda i, p: (i, 0, p)),
        out_shape=jax.ShapeDtypeStruct((b, s, SB_W), jnp.float32),
        scratch_shapes=[pltpu.VMEM((2, SB_TQ, SB_TK), jnp.float32),
                        pltpu.VMEM((SB_TQ, LANES), jnp.float32)],
        compiler_params=pltpu.CompilerParams(
            dimension_semantics=("arbitrary", "arbitrary")),
        name="sb",
    )(proj, proj, proj)


def _out_proj_kernel(x_ref, ya_ref, yb_ref, ga_ref, gb_ref, w_ref, o_ref):
    y = jnp.concatenate([_rms(ya_ref[...], ga_ref[...]), _rms(yb_ref[...], gb_ref[...])],
                        axis=1).astype(jnp.bfloat16)
    o_ref[...] = x_ref[...] + jnp.dot(y, w_ref[...], preferred_element_type=jnp.float32)


def _out_proj(x, ya, yb, ga, gb, w):
    n = x.shape[0]
    return pl.pallas_call(
        _out_proj_kernel,
        grid=(n // OUT_TM,),
        in_specs=[
            pl.BlockSpec((OUT_TM, D_MODEL), lambda i: (i, 0)),
            pl.BlockSpec((OUT_TM, SWA_Q), lambda i: (i, 0)),
            pl.BlockSpec((OUT_TM, SB_W), lambda i: (i, 0)),
            pl.BlockSpec((1, SWA_Q), lambda i: (0, 0)),
            pl.BlockSpec((1, SB_W), lambda i: (0, 0)),
            pl.BlockSpec((SWA_Q + SB_W, D_MODEL), lambda i: (0, 0)),
        ],
        out_specs=pl.BlockSpec((OUT_TM, D_MODEL), lambda i: (i, 0)),
        out_shape=jax.ShapeDtypeStruct((n, D_MODEL), jnp.float32),
        compiler_params=pltpu.CompilerParams(
            dimension_semantics=("arbitrary",),
            vmem_limit_bytes=VMEM_LIMIT),
        name="out_proj",
    )(x, ya, yb, ga, gb, w)


def _prep_ffn_weights(w_gate, w_up, w_down):
    return (w_gate.astype(jnp.bfloat16), w_up.astype(jnp.bfloat16),
            w_down.astype(jnp.bfloat16))


def _prep_w_in(w):
    def twice(cols):
        c = cols.reshape(D_MODEL, N_SWA_KV, 1, HEAD_DIM)
        return jnp.broadcast_to(c, (D_MODEL, N_SWA_KV, 2, HEAD_DIM)).reshape(D_MODEL, -1)
    o1 = SWA_Q
    o2 = o1 + N_SWA_KV * HEAD_DIM
    o3 = o2 + N_SWA_KV * HEAD_DIM
    return jnp.concatenate([w[:, :o1], twice(w[:, o1:o2]), twice(w[:, o2:o3]), w[:, o3:]],
                           axis=1).astype(jnp.bfloat16)


def _proj_col_scale():
    s = jnp.ones((PROJ_W,), jnp.float32)
    s = s.at[:SWA_Q].set(HEAD_DIM ** -0.5 * LOG2E)
    s = s.at[COL_QB:COL_KB].set(HEAD_DIM ** -0.5 * LOG2E)
    return s.reshape(1, PROJ_W)


def kernel(x, ffn1_norm, ffn1_w_gate, ffn1_w_up, ffn1_w_down, mix_norm, w_in, swa_sinks,
           swa_out_norm, sb_out_norm, w_out, ffn2_norm, ffn2_w_gate, ffn2_w_up, ffn2_w_down,
           final_norm):
    b, s, d = x.shape
    depth = ffn1_norm.shape[0]
    xf = x.reshape(b * s, d)
    i = jnp.arange(1, N_SWA_HEADS + 1, dtype=jnp.float32)
    slopes = jnp.exp2(-8.0 * i / N_SWA_HEADS)
    fg = final_norm.reshape(1, d)
    col_scale = _proj_col_scale()
    for l in range(depth):
        wg, wu, wd = _prep_ffn_weights(ffn1_w_gate[l], ffn1_w_up[l], ffn1_w_down[l])
        xf = _ffn(xf, ffn1_norm[l].reshape(1, d), wg, wu, wd, fg, final_norm=False)

        proj = _in_proj(xf, mix_norm[l].reshape(1, d), _prep_w_in(w_in[l]), col_scale)
        proj = proj.reshape(b, s, PROJ_W)
        ya = _swa(proj, slopes, swa_sinks[l])
        yb = _sb(proj)
        xf = _out_proj(xf, ya.reshape(b * s, SWA_Q), yb.reshape(b * s, SB_W),
                       swa_out_norm[l].reshape(1, SWA_Q), sb_out_norm[l].reshape(1, SB_W),
                       w_out[l].astype(jnp.bfloat16))

        wg, wu, wd = _prep_ffn_weights(ffn2_w_gate[l], ffn2_w_up[l], ffn2_w_down[l])
        xf = _ffn(xf, ffn2_norm[l].reshape(1, d), wg, wu, wd, fg,
                  final_norm=(l == depth - 1))
    return xf.reshape(b, s, d)
```

```python
import functools
import math

import jax
import jax.numpy as jnp
from jax import lax
from jax.experimental import pallas as pl
from jax.experimental.pallas import tpu as pltpu

D_MODEL = 2048
HEAD_DIM = 64
N_SWA_HEADS = 16
N_SWA_KV = 4
N_SB_HEADS = 16
WINDOW = 128
BLOCK = 128
D_FF = 5504
EPS = 1e-6
LOG2E = math.log2(math.e)

LANES = 128
FFN_TM = 1024
FFN_TF = 512
PROJ_TM = 1024
PROJ_TN = 1536
OUT_TM = 512
SWA_QB = 16
SB_TQ = 2048
SB_TK = 128
SB_CK = 512
VMEM_LIMIT = 56 * 1024 * 1024

SWA_Q = N_SWA_HEADS * HEAD_DIM
SB_W = N_SB_HEADS * HEAD_DIM
COL_KA = SWA_Q
COL_VA = COL_KA + 2 * N_SWA_KV * HEAD_DIM
COL_QB = COL_VA + 2 * N_SWA_KV * HEAD_DIM
COL_KB = COL_QB + SB_W
COL_VB = COL_KB + SB_W
PROJ_W = COL_VB + SB_W
IN_W = SWA_Q + 2 * N_SWA_KV * HEAD_DIM + 3 * SB_W
assert PROJ_TN == SWA_Q + 2 * N_SWA_KV * HEAD_DIM and IN_W % PROJ_TN == 0

_NT = (((1,), (1,)), ((), ()))


def _rms(x, g):
    return x * lax.rsqrt(jnp.mean(x * x, axis=-1, keepdims=True) + EPS) * g


def _ffn_kernel(x_ref, g_ref, wg_ref, wu_ref, wd_ref, fg_ref, o_ref, h_ref, *, final_norm):
    j = pl.program_id(1)
    last = pl.cdiv(D_FF, FFN_TF) - 1
    tail = D_FF - last * FFN_TF

    def half_swiglu(h, width):
        gate = jnp.dot(h, wg_ref[:, :width], preferred_element_type=jnp.float32)
        up = jnp.dot(h, wu_ref[:, :width], preferred_element_type=jnp.float32)
        act = (gate * jax.nn.sigmoid(gate) * up * 0.5).astype(jnp.bfloat16)
        return jnp.dot(act, wd_ref[:width, :], preferred_element_type=jnp.float32)

    @pl.when(j == 0)
    def _():
        x = x_ref[...]
        h = _rms(x, g_ref[...]).astype(jnp.bfloat16)
        h_ref[...] = h
        o_ref[...] = x + half_swiglu(h, FFN_TF)

    @pl.when(jnp.logical_and(j > 0, j < last))
    def _():
        o_ref[...] += half_swiglu(h_ref[...], FFN_TF)

    @pl.when(j == last)
    def _():
        y = o_ref[...] + half_swiglu(h_ref[...], tail)
        o_ref[...] = _rms(y, fg_ref[...]) if final_norm else y


def _ffn(x, g, wg, wu, wd, fg, final_norm):
    n = x.shape[0]
    grid = (n // FFN_TM, pl.cdiv(D_FF, FFN_TF))
    return pl.pallas_call(
        functools.partial(_ffn_kernel, final_norm=final_norm),
        grid=grid,
        in_specs=[
            pl.BlockSpec((FFN_TM, D_MODEL), lambda i, j: (i, 0)),
            pl.BlockSpec((1, D_MODEL), lambda i, j: (0, 0)),
            pl.BlockSpec((D_MODEL, FFN_TF), lambda i, j: (0, j)),
            pl.BlockSpec((D_MODEL, FFN_TF), lambda i, j: (0, j)),
            pl.BlockSpec((FFN_TF, D_MODEL), lambda i, j: (j, 0)),
            pl.BlockSpec((1, D_MODEL), lambda i, j: (0, 0)),
        ],
        out_specs=pl.BlockSpec((FFN_TM, D_MODEL), lambda i, j: (i, 0)),
        out_shape=jax.ShapeDtypeStruct((n, D_MODEL), jnp.float32),
        scratch_shapes=[pltpu.VMEM((FFN_TM, D_MODEL), jnp.bfloat16)],
        compiler_params=pltpu.CompilerParams(
            dimension_semantics=("arbitrary", "arbitrary"),
            vmem_limit_bytes=VMEM_LIMIT),
        name="ffn_final" if final_norm else "ffn",
    )(x, g, wg, wu, wd, fg)


def _in_proj_kernel(x_ref, g_ref, w_ref, s_ref, o_ref, h_ref):
    def project(h):
        acc = jnp.dot(h, w_ref[...], preferred_element_type=jnp.float32)
        return acc * s_ref[...]

    @pl.when(pl.program_id(1) == 0)
    def _():
        h = _rms(x_ref[...], g_ref[...]).astype(jnp.bfloat16)
        h_ref[...] = h
        p = project(h)
        o_ref[:, :SWA_Q] = p[:, :SWA_Q].astype(jnp.bfloat16)
        for g in range(2 * N_SWA_KV):
            head = p[:, SWA_Q + g * HEAD_DIM:SWA_Q + (g + 1) * HEAD_DIM]
            o_ref[:, COL_KA + g * LANES:COL_KA + (g + 1) * LANES] = jnp.concatenate(
                [head, head], axis=1).astype(jnp.bfloat16)

    for jb in range(1, IN_W // PROJ_TN):
        @pl.when(pl.program_id(1) == jb)
        def _(jb=jb):
            start = jb * PROJ_TN + PROJ_W - IN_W
            o_ref[:, start:start + PROJ_TN] = project(h_ref[...]).astype(jnp.bfloat16)


def _in_proj(x, g, w, col_scale):
    n = x.shape[0]
    grid = (n // PROJ_TM, IN_W // PROJ_TN)
    return pl.pallas_call(
        _in_proj_kernel,
        grid=grid,
        in_specs=[
            pl.BlockSpec((PROJ_TM, D_MODEL), lambda i, j: (i, 0)),
            pl.BlockSpec((1, D_MODEL), lambda i, j: (0, 0)),
            pl.BlockSpec((D_MODEL, PROJ_TN), lambda i, j: (0, j)),
            pl.BlockSpec((1, PROJ_TN), lambda i, j: (0, j)),
        ],
        out_specs=pl.BlockSpec((PROJ_TM, PROJ_W), lambda i, j: (i, 0)),
        out_shape=jax.ShapeDtypeStruct((n, PROJ_W), jnp.bfloat16),
        scratch_shapes=[pltpu.VMEM((PROJ_TM, D_MODEL), jnp.bfloat16)],
        compiler_params=pltpu.CompilerParams(
            dimension_semantics=("arbitrary", "arbitrary"),
            vmem_limit_bytes=VMEM_LIMIT),
        name="in_proj",
    )(x, g, w, col_scale)


def _swa_kernel(slopes_ref, sinks_ref, q_ref, k_ref, v_ref, o_ref, bias_ref):
    g = pl.program_id(1)
    n_heads = N_SWA_HEADS // N_SWA_KV
    nb = q_ref.shape[1] // BLOCK
    row = lax.broadcasted_iota(jnp.int32, (BLOCK, BLOCK), 0)
    col = lax.broadcasted_iota(jnp.int32, (BLOCK, BLOCK), 1)
    low_half = col < HEAD_DIM
    dist_p = (row + BLOCK - col).astype(jnp.float32)
    dist_c = (row - col).astype(jnp.float32)
    for jh in range(n_heads):
        slope2 = slopes_ref[g * n_heads + jh] * LOG2E
        bias_ref[jh, :, :BLOCK] = jnp.where(col > row, -slope2 * dist_p, -jnp.inf)
        bias_ref[jh, :, BLOCK:] = jnp.where(col <= row, -slope2 * dist_c, -jnp.inf)
    ones = jnp.ones((BLOCK, LANES), jnp.bfloat16)
    prev_cols = lax.broadcasted_iota(jnp.int32, (BLOCK, 2 * BLOCK), 1) < BLOCK

    def group(gi, _):
        for t in range(SWA_QB):
            qb = gi * SWA_QB + t
            q0 = pl.multiple_of(qb * BLOCK, BLOCK)
            p0 = pl.multiple_of(jnp.maximum(qb - 1, 0) * BLOCK, BLOCK)
            kk = jnp.concatenate([k_ref[0, pl.ds(p0, BLOCK), :],
                                  k_ref[0, pl.ds(q0, BLOCK), :]], axis=0)
            vv = jnp.concatenate(
                [jnp.concatenate([v_ref[0, pl.ds(p0, BLOCK), :], ones], axis=1),
                 jnp.concatenate([v_ref[0, pl.ds(q0, BLOCK), :], ones], axis=1)], axis=0)
            q = q_ref[0, pl.ds(q0, BLOCK), :]
            outs = []
            for jh in range(n_heads):
                sink2 = sinks_ref[g * n_heads + jh] * LOG2E
                qpair = q[:, (jh // 2) * LANES:(jh // 2 + 1) * LANES]
                keep = low_half if jh % 2 == 0 else jnp.logical_not(low_half)
                qh = jnp.where(keep, qpair, jnp.zeros_like(qpair))
                bias = bias_ref[jh]
                if t == 0:
                    bias = jnp.where(jnp.logical_and(prev_cols, qb == 0), -jnp.inf, bias)
                s = lax.dot_general(qh, kk, _NT, preferred_element_type=jnp.float32) + bias
                m = jnp.maximum(jnp.max(s, axis=-1, keepdims=True), sink2)
                e = jnp.exp2(s - m).astype(jnp.bfloat16)
                pv = jnp.dot(e, vv, preferred_element_type=jnp.float32)
                outs.append(pv[:, :LANES] / (pv[:, LANES:] + jnp.exp2(sink2 - m)))
            o_ref[0, pl.ds(q0, BLOCK), :] = jnp.concatenate(
                [jnp.where(low_half, outs[0], outs[1]), jnp.where(low_half, outs[2], outs[3])],
                axis=1)
        return 0

    lax.fori_loop(0, nb // SWA_QB, group, 0)


def _swa(proj, slopes, sinks):
    b, s, _ = proj.shape
    kblk, vblk = COL_KA // LANES, COL_VA // LANES
    smem = pl.BlockSpec(memory_space=pltpu.SMEM)
    return pl.pallas_call(
        _swa_kernel,
        grid=(b, N_SWA_KV),
        in_specs=[
            smem, smem,
            pl.BlockSpec((1, s, 2 * LANES), lambda i, g: (i, 0, g)),
            pl.BlockSpec((1, s, LANES), lambda i, g: (i, 0, kblk + g)),
            pl.BlockSpec((1, s, LANES), lambda i, g: (i, 0, vblk + g)),
        ],
        out_specs=pl.BlockSpec((1, s, 2 * LANES), lambda i, g: (i, 0, g)),
        out_shape=jax.ShapeDtypeStruct((b, s, SWA_Q), jnp.float32),
        scratch_shapes=[pltpu.VMEM((N_SWA_HEADS // N_SWA_KV, BLOCK, 2 * BLOCK), jnp.float32)],
        compiler_params=pltpu.CompilerParams(
            dimension_semantics=("arbitrary", "arbitrary")),
        name="swa",
    )(slopes, sinks, proj, proj, proj)


def _sb_kernel(q_ref, k_ref, v_ref, o_ref, carry_ref, acc_ref):
    s_len = q_ref.shape[1]
    nq = s_len // SB_TQ
    n_sub = SB_TQ // SB_TK
    r = lax.broadcasted_iota(jnp.int32, (2 * SB_TK, 2 * SB_TK), 0)
    c = lax.broadcasted_iota(jnp.int32, (2 * SB_TK, 2 * SB_TK), 1)
    tri = jnp.where(jnp.logical_and(r >= c, (r < SB_TK) == (c < SB_TK)), 1.0, 0.0
                    ).astype(jnp.bfloat16)
    low_half = lax.broadcasted_iota(jnp.int32, (SB_TK, LANES), 1) < HEAD_DIM
    strict_lower = (lax.broadcasted_iota(jnp.int32, (SB_TK, SB_TK), 1)
                    < lax.broadcasted_iota(jnp.int32, (SB_TK, SB_TK), 0))

    def q_block(qi, _):
        q0 = pl.multiple_of(qi * SB_TQ, SB_TQ)
        carry_ref[...] = jnp.zeros_like(carry_ref)
        acc_ref[...] = jnp.zeros_like(acc_ref)

        def k_block(k0, r0, diag):
            kblk = k_ref[0, pl.ds(k0, SB_TK), :]
            vblk = v_ref[0, pl.ds(k0, SB_TK), :]
            zero = jnp.zeros_like(kblk)
            kk = jnp.concatenate([jnp.where(low_half, kblk, zero),
                                  jnp.where(low_half, zero, kblk)], axis=0)
            vv = jnp.concatenate([jnp.where(low_half, vblk, zero),
                                  jnp.where(low_half, zero, vblk)], axis=0)
            def mask_top(x):
                top = jnp.where(strict_lower, x[:SB_TK], 0.0)
                return top if r0 == SB_TQ - SB_TK else jnp.concatenate([top, x[SB_TK:]], axis=0)

            q = q_ref[0, pl.ds(q0 + r0, SB_TQ - r0), :]
            u2 = lax.dot_general(q, kk, _NT, preferred_element_type=jnp.float32)
            sps = []
            for hd in range(2):
                u = u2[:, hd * SB_TK:(hd + 1) * SB_TK]
                neg_abs = lax.bitcast_convert_type(
                    lax.bitcast_convert_type(u, jnp.int32) | jnp.int32(-2 ** 31), jnp.float32)
                sp = jnp.maximum(u, 0.0) + jnp.log(1.0 + jnp.exp2(neg_abs)) * LOG2E
                if diag:
                    sp = mask_top(sp)
                sps.append(sp.astype(jnp.bfloat16))
            sums = jnp.dot(jnp.concatenate(sps, axis=1), tri,
                           preferred_element_type=jnp.float32)
            a_heads = []
            for hd in range(2):
                u = u2[:, hd * SB_TK:(hd + 1) * SB_TK]
                s_in = sums[:, hd * SB_TK:(hd + 1) * SB_TK]
                carry = carry_ref[hd, r0:, :]
                a = jnp.exp2(u - (carry + s_in))
                if diag:
                    a = mask_top(a)
                carry_ref[hd, r0:, :] = carry + jnp.broadcast_to(s_in[:, 0:1], s_in.shape)
                a_heads.append(a.astype(jnp.bfloat16))
            acc_ref[r0:, :] += jnp.dot(jnp.concatenate(a_heads, axis=1), vv,
                                       preferred_element_type=jnp.float32)

        for d in range(n_sub - 1, -1, -1):
            k_block(pl.multiple_of(q0 + d * SB_TK, SB_TK), d * SB_TK, True)

        def chunk(i, _):
            base = q0 - (i + 1) * SB_CK
            for d in range(SB_CK // SB_TK - 1, -1, -1):
                k_block(pl.multiple_of(base + d * SB_TK, SB_TK), 0, False)
            return 0

        lax.fori_loop(0, qi * (SB_TQ // SB_CK), chunk, 0)
        o_ref[0, pl.ds(q0, SB_TQ), :] = acc_ref[...]
        return 0

    lax.fori_loop(0, nq, q_block, 0)


def _sb(proj):
    b, s, _ = proj.shape
    npair = N_SB_HEADS // 2
    qblk, kblk, vblk = COL_QB // LANES, COL_KB // LANES, COL_VB // LANES
    return pl.pallas_call(
        _sb_kernel,
        grid=(b, npair),
        in_specs=[
            pl.BlockSpec((1, s, LANES), lambda i, p: (i, 0, qblk + p)),
            pl.BlockSpec((1, s, LANES), lambda i, p: (i, 0, kblk + p)),
            pl.BlockSpec((1, s, LANES), lambda i, p: (i, 0, vblk + p)),
        ],
        out_specs=pl.BlockSpec((1, s, LANES), lambda i, p: (i, 0, p)),
        out_shape=jax.ShapeDtypeStruct((b, s, SB_W), jnp.float32),
        scratch_shapes=[pltpu.VMEM((2, SB_TQ, SB_TK), jnp.float32),
                        pltpu.VMEM((SB_TQ, LANES), jnp.float32)],
        compiler_params=pltpu.CompilerParams(
            dimension_semantics=("arbitrary", "arbitrary")),
        name="sb",
    )(proj, proj, proj)


def _out_proj_kernel(x_ref, ya_ref, yb_ref, ga_ref, gb_ref, w_ref, o_ref):
    y = jnp.concatenate([_rms(ya_ref[...], ga_ref[...]), _rms(yb_ref[...], gb_ref[...])],
                        axis=1).astype(jnp.bfloat16)
    o_ref[...] = x_ref[...] + jnp.dot(y, w_ref[...], preferred_element_type=jnp.float32)


def _out_proj(x, ya, yb, ga, gb, w):
    n = x.shape[0]
    return pl.pallas_call(
        _out_proj_kernel,
        grid=(n // OUT_TM,),
        in_specs=[
            pl.BlockSpec((OUT_TM, D_MODEL), lambda i: (i, 0)),
            pl.BlockSpec((OUT_TM, SWA_Q), lambda i: (i, 0)),
            pl.BlockSpec((OUT_TM, SB_W), lambda i: (i, 0)),
            pl.BlockSpec((1, SWA_Q), lambda i: (0, 0)),
            pl.BlockSpec((1, SB_W), lambda i: (0, 0)),
            pl.BlockSpec((SWA_Q + SB_W, D_MODEL), lambda i: (0, 0)),
        ],
        out_specs=pl.BlockSpec((OUT_TM, D_MODEL), lambda i: (i, 0)),
        out_shape=jax.ShapeDtypeStruct((n, D_MODEL), jnp.float32),
        compiler_params=pltpu.CompilerParams(
            dimension_semantics=("arbitrary",),
            vmem_limit_bytes=VMEM_LIMIT),
        name="out_proj",
    )(x, ya, yb, ga, gb, w)


def _prep_ffn_weights(w_gate, w_up, w_down):
    return (w_gate.astype(jnp.bfloat16), w_up.astype(jnp.bfloat16),
            w_down.astype(jnp.bfloat16))


def _proj_col_scale():
    qb0 = SWA_Q + 2 * N_SWA_KV * HEAD_DIM
    s = jnp.ones((IN_W,), jnp.float32)
    s = s.at[:SWA_Q].set(HEAD_DIM ** -0.5 * LOG2E)
    s = s.at[qb0:qb0 + SB_W].set(HEAD_DIM ** -0.5 * LOG2E)
    return s.reshape(1, IN_W)


def kernel(x, ffn1_norm, ffn1_w_gate, ffn1_w_up, ffn1_w_down, mix_norm, w_in, swa_sinks,
           swa_out_norm, sb_out_norm, w_out, ffn2_norm, ffn2_w_gate, ffn2_w_up, ffn2_w_down,
           final_norm):
    b, s, d = x.shape
    depth = ffn1_norm.shape[0]
    xf = x.reshape(b * s, d)
    i = jnp.arange(1, N_SWA_HEADS + 1, dtype=jnp.float32)
    slopes = jnp.exp2(-8.0 * i / N_SWA_HEADS)
    fg = final_norm.reshape(1, d)
    col_scale = _proj_col_scale()
    for l in range(depth):
        wg, wu, wd = _prep_ffn_weights(ffn1_w_gate[l], ffn1_w_up[l], ffn1_w_down[l])
        xf = _ffn(xf, ffn1_norm[l].reshape(1, d), wg, wu, wd, fg, final_norm=False)

        proj = _in_proj(xf, mix_norm[l].reshape(1, d), w_in[l].astype(jnp.bfloat16), col_scale)
        proj = proj.reshape(b, s, PROJ_W)
        ya = _swa(proj, slopes, swa_sinks[l])
        yb = _sb(proj)
        xf = _out_proj(xf, ya.reshape(b * s, SWA_Q), yb.reshape(b * s, SB_W),
                       swa_out_norm[l].reshape(1, SWA_Q), sb_out_norm[l].reshape(1, SB_W),
                       w_out[l].astype(jnp.bfloat16))

        wg, wu, wd = _prep_ffn_weights(ffn2_w_gate[l], ffn2_w_up[l], ffn2_w_down[l])
        xf = _ffn(xf, ffn2_norm[l].reshape(1, d), wg, wu, wd, fg,
                  final_norm=(l == depth - 1))
    return xf.reshape(b, s, d)
```

```python
import functools
import math

import jax
import jax.numpy as jnp
from jax import lax
from jax.experimental import pallas as pl
from jax.experimental.pallas import tpu as pltpu

D_MODEL = 2048
HEAD_DIM = 64
N_SWA_HEADS = 16
N_SWA_KV = 4
N_SB_HEADS = 16
WINDOW = 128
BLOCK = 128
D_FF = 5504
EPS = 1e-6
LOG2E = math.log2(math.e)

LANES = 128
FFN_TM = 1024
FFN_TF = 512
FFN_HEAD_TF = 256
PROJ_TM = 1024
PROJ_TN = 1536
OUT_TM = 512
SWA_QB = 16
SB_TQ = 2048
SB_TK = 128
SB_CK = 512
VMEM_LIMIT = 56 * 1024 * 1024

SWA_Q = N_SWA_HEADS * HEAD_DIM
SB_W = N_SB_HEADS * HEAD_DIM
COL_KA = SWA_Q
COL_VA = COL_KA + 2 * N_SWA_KV * HEAD_DIM
COL_QB = COL_VA + 2 * N_SWA_KV * HEAD_DIM
COL_KB = COL_QB + SB_W
COL_VB = COL_KB + SB_W
PROJ_W = COL_VB + SB_W
IN_W = SWA_Q + 2 * N_SWA_KV * HEAD_DIM + 3 * SB_W
assert PROJ_TN == SWA_Q + 2 * N_SWA_KV * HEAD_DIM and IN_W % PROJ_TN == 0

_NT = (((1,), (1,)), ((), ()))


def _rms(x, g):
    return x * lax.rsqrt(jnp.mean(x * x, axis=-1, keepdims=True) + EPS) * g


def _ffn_kernel(*refs, tf, final_norm, cast_weights):
    if cast_weights:
        (x_ref, g_ref, wg_ref, wu_ref, wd_ref, fg_ref,
         o_ref, wg_out, wu_out, wd_out, h_ref) = refs
    else:
        x_ref, g_ref, wg_ref, wu_ref, wd_ref, fg_ref, _, o_ref, h_ref = refs
    j = pl.program_id(1)
    last = pl.cdiv(D_FF, tf) - 1
    tail = D_FF - last * tf

    def half_swiglu(h, width):
        wg, wu, wd = wg_ref[:, :width], wu_ref[:, :width], wd_ref[:width, :]
        if cast_weights:
            wg, wu, wd = (w.astype(jnp.bfloat16) for w in (wg, wu, wd))
            wg_out[:, :width] = wg
            wu_out[:, :width] = wu
            wd_out[:width, :] = wd
        gate = jnp.dot(h, wg, preferred_element_type=jnp.float32)
        up = jnp.dot(h, wu, preferred_element_type=jnp.float32)
        act = (gate * jax.nn.sigmoid(gate) * up * 0.5).astype(jnp.bfloat16)
        return jnp.dot(act, wd, preferred_element_type=jnp.float32)

    @pl.when(j == 0)
    def _():
        x = x_ref[...]
        h = _rms(x, g_ref[...]).astype(jnp.bfloat16)
        h_ref[...] = h
        o_ref[...] = x + half_swiglu(h, tf)

    @pl.when(jnp.logical_and(j > 0, j < last))
    def _():
        o_ref[...] += half_swiglu(h_ref[...], tf)

    @pl.when(j == last)
    def _():
        y = o_ref[...] + half_swiglu(h_ref[...], tail)
        o_ref[...] = _rms(y, fg_ref[...]) if final_norm else y


def _ffn(x, g, w_gate, w_up, w_down, fg, final_norm):
    n = x.shape[0]
    suffix = "_final" if final_norm else ""
    params = pltpu.CompilerParams(dimension_semantics=("arbitrary", "arbitrary"),
                                  vmem_limit_bytes=VMEM_LIMIT)
    vec = pl.BlockSpec((1, D_MODEL), lambda i, j: (0, 0))

    def weight_specs(tf):
        return [pl.BlockSpec((D_MODEL, tf), lambda i, j: (0, j)),
                pl.BlockSpec((D_MODEL, tf), lambda i, j: (0, j)),
                pl.BlockSpec((tf, D_MODEL), lambda i, j: (j, 0))]

    out_shape = jax.ShapeDtypeStruct((n, D_MODEL), jnp.float32)
    head_out, wg, wu, wd = pl.pallas_call(
        functools.partial(_ffn_kernel, tf=FFN_HEAD_TF, final_norm=final_norm, cast_weights=True),
        grid=(1, pl.cdiv(D_FF, FFN_HEAD_TF)),
        in_specs=[pl.BlockSpec((FFN_TM, D_MODEL), lambda i, j: (0, 0), pipeline_mode=pl.Buffered(1)),
                  vec, *weight_specs(FFN_HEAD_TF), vec],
        out_specs=[pl.BlockSpec((FFN_TM, D_MODEL), lambda i, j: (0, 0)),
                   *weight_specs(FFN_HEAD_TF)],
        out_shape=[out_shape,
                   jax.ShapeDtypeStruct(w_gate.shape, jnp.bfloat16),
                   jax.ShapeDtypeStruct(w_up.shape, jnp.bfloat16),
                   jax.ShapeDtypeStruct(w_down.shape, jnp.bfloat16)],
        scratch_shapes=[pltpu.VMEM((FFN_TM, D_MODEL), jnp.bfloat16)],
        compiler_params=params,
        name="ffn_head" + suffix,
    )(x, g, w_gate, w_up, w_down, fg)

    return pl.pallas_call(
        functools.partial(_ffn_kernel, tf=FFN_TF, final_norm=final_norm, cast_weights=False),
        grid=(n // FFN_TM - 1, pl.cdiv(D_FF, FFN_TF)),
        in_specs=[pl.BlockSpec((FFN_TM, D_MODEL), lambda i, j: (i + 1, 0)),
                  vec, *weight_specs(FFN_TF), vec,
                  pl.BlockSpec(memory_space=pl.ANY)],
        out_specs=pl.BlockSpec((FFN_TM, D_MODEL), lambda i, j: (i + 1, 0)),
        out_shape=out_shape,
        input_output_aliases={6: 0},
        scratch_shapes=[pltpu.VMEM((FFN_TM, D_MODEL), jnp.bfloat16)],
        compiler_params=params,
        name="ffn" + suffix,
    )(x, g, wg, wu, wd, fg, head_out)


def _in_proj_kernel(x_ref, g_ref, w_ref, s_ref, o_ref, h_ref):
    def project(h):
        acc = jnp.dot(h, w_ref[...], preferred_element_type=jnp.float32)
        return acc * s_ref[...]

    @pl.when(pl.program_id(1) == 0)
    def _():
        h = _rms(x_ref[...], g_ref[...]).astype(jnp.bfloat16)
        h_ref[...] = h
        p = project(h)
        o_ref[:, :SWA_Q] = p[:, :SWA_Q].astype(jnp.bfloat16)
        for g in range(2 * N_SWA_KV):
            head = p[:, SWA_Q + g * HEAD_DIM:SWA_Q + (g + 1) * HEAD_DIM]
            o_ref[:, COL_KA + g * LANES:COL_KA + (g + 1) * LANES] = jnp.concatenate(
                [head, head], axis=1).astype(jnp.bfloat16)

    for jb in range(1, IN_W // PROJ_TN):
        @pl.when(pl.program_id(1) == jb)
        def _(jb=jb):
            start = jb * PROJ_TN + PROJ_W - IN_W
            o_ref[:, start:start + PROJ_TN] = project(h_ref[...]).astype(jnp.bfloat16)


def _in_proj(x, g, w, col_scale):
    n = x.shape[0]
    grid = (n // PROJ_TM, IN_W // PROJ_TN)
    return pl.pallas_call(
        _in_proj_kernel,
        grid=grid,
        in_specs=[
            pl.BlockSpec((PROJ_TM, D_MODEL), lambda i, j: (i, 0)),
            pl.BlockSpec((1, D_MODEL), lambda i, j: (0, 0)),
            pl.BlockSpec((D_MODEL, PROJ_TN), lambda i, j: (0, j)),
            pl.BlockSpec((1, PROJ_TN), lambda i, j: (0, j)),
        ],
        out_specs=pl.BlockSpec((PROJ_TM, PROJ_W), lambda i, j: (i, 0)),
        out_shape=jax.ShapeDtypeStruct((n, PROJ_W), jnp.bfloat16),
        scratch_shapes=[pltpu.VMEM((PROJ_TM, D_MODEL), jnp.bfloat16)],
        compiler_params=pltpu.CompilerParams(
            dimension_semantics=("arbitrary", "arbitrary"),
            vmem_limit_bytes=VMEM_LIMIT),
        name="in_proj",
    )(x, g, w, col_scale)


def _swa_kernel(slopes_ref, sinks_ref, q_ref, k_ref, v_ref, o_ref, bias_ref):
    g = pl.program_id(1)
    n_heads = N_SWA_HEADS // N_SWA_KV
    nb = q_ref.shape[1] // BLOCK
    row = lax.broadcasted_iota(jnp.int32, (BLOCK, BLOCK), 0)
    col = lax.broadcasted_iota(jnp.int32, (BLOCK, BLOCK), 1)
    low_half = col < HEAD_DIM
    dist_p = (row + BLOCK - col).astype(jnp.float32)
    dist_c = (row - col).astype(jnp.float32)
    for jh in range(n_heads):
        slope2 = slopes_ref[g * n_heads + jh] * LOG2E
        bias_ref[jh, :, :BLOCK] = jnp.where(col > row, -slope2 * dist_p, -jnp.inf)
        bias_ref[jh, :, BLOCK:] = jnp.where(col <= row, -slope2 * dist_c, -jnp.inf)
    ones = jnp.ones((BLOCK, LANES), jnp.bfloat16)
    prev_cols = lax.broadcasted_iota(jnp.int32, (BLOCK, 2 * BLOCK), 1) < BLOCK

    def group(gi, _):
        for t in range(SWA_QB):
            qb = gi * SWA_QB + t
            q0 = pl.multiple_of(qb * BLOCK, BLOCK)
            p0 = pl.multiple_of(jnp.maximum(qb - 1, 0) * BLOCK, BLOCK)
            kk = jnp.concatenate([k_ref[0, pl.ds(p0, BLOCK), :],
                                  k_ref[0, pl.ds(q0, BLOCK), :]], axis=0)
            vv = jnp.concatenate(
                [jnp.concatenate([v_ref[0, pl.ds(p0, BLOCK), :], ones], axis=1),
                 jnp.concatenate([v_ref[0, pl.ds(q0, BLOCK), :], ones], axis=1)], axis=0)
            q = q_ref[0, pl.ds(q0, BLOCK), :]
            outs = []
            for jh in range(n_heads):
                sink2 = sinks_ref[g * n_heads + jh] * LOG2E
                qpair = q[:, (jh // 2) * LANES:(jh // 2 + 1) * LANES]
                keep = low_half if jh % 2 == 0 else jnp.logical_not(low_half)
                qh = jnp.where(keep, qpair, jnp.zeros_like(qpair))
                bias = bias_ref[jh]
                if t == 0:
                    bias = jnp.where(jnp.logical_and(prev_cols, qb == 0), -jnp.inf, bias)
                s = lax.dot_general(qh, kk, _NT, preferred_element_type=jnp.float32) + bias
                m = jnp.maximum(jnp.max(s, axis=-1, keepdims=True), sink2)
                e = jnp.exp2(s - m).astype(jnp.bfloat16)
                pv = jnp.dot(e, vv, preferred_element_type=jnp.float32)
                outs.append(pv[:, :LANES] / (pv[:, LANES:] + jnp.exp2(sink2 - m)))
            o_ref[0, pl.ds(q0, BLOCK), :] = jnp.concatenate(
                [jnp.where(low_half, outs[0], outs[1]), jnp.where(low_half, outs[2], outs[3])],
                axis=1)
        return 0

    lax.fori_loop(0, nb // SWA_QB, group, 0)


def _swa(proj, slopes, sinks):
    b, s, _ = proj.shape
    kblk, vblk = COL_KA // LANES, COL_VA // LANES
    smem = pl.BlockSpec(memory_space=pltpu.SMEM)
    return pl.pallas_call(
        _swa_kernel,
        grid=(b, N_SWA_KV),
        in_specs=[
            smem, smem,
            pl.BlockSpec((1, s, 2 * LANES), lambda i, g: (i, 0, g)),
            pl.BlockSpec((1, s, LANES), lambda i, g: (i, 0, kblk + g)),
            pl.BlockSpec((1, s, LANES), lambda i, g: (i, 0, vblk + g)),
        ],
        out_specs=pl.BlockSpec((1, s, 2 * LANES), lambda i, g: (i, 0, g)),
        out_shape=jax.ShapeDtypeStruct((b, s, SWA_Q), jnp.float32),
        scratch_shapes=[pltpu.VMEM((N_SWA_HEADS // N_SWA_KV, BLOCK, 2 * BLOCK), jnp.float32)],
        compiler_params=pltpu.CompilerParams(
            dimension_semantics=("arbitrary", "arbitrary")),
        name="swa",
    )(slopes, sinks, proj, proj, proj)


def _sb_kernel(q_ref, k_ref, v_ref, o_ref, carry_ref, acc_ref):
    s_len = q_ref.shape[1]
    nq = s_len // SB_TQ
    n_sub = SB_TQ // SB_TK
    r = lax.broadcasted_iota(jnp.int32, (2 * SB_TK, 2 * SB_TK), 0)
    c = lax.broadcasted_iota(jnp.int32, (2 * SB_TK, 2 * SB_TK), 1)
    tri = jnp.where(jnp.logical_and(r >= c, (r < SB_TK) == (c < SB_TK)), 1.0, 0.0
                    ).astype(jnp.bfloat16)
    low_half = lax.broadcasted_iota(jnp.int32, (SB_TK, LANES), 1) < HEAD_DIM
    strict_lower = (lax.broadcasted_iota(jnp.int32, (SB_TK, SB_TK), 1)
                    < lax.broadcasted_iota(jnp.int32, (SB_TK, SB_TK), 0))

    def q_block(qi, _):
        q0 = pl.multiple_of(qi * SB_TQ, SB_TQ)
        carry_ref[...] = jnp.zeros_like(carry_ref)
        acc_ref[...] = jnp.zeros_like(acc_ref)

        def k_block(k0, r0, diag):
            kblk = k_ref[0, pl.ds(k0, SB_TK), :]
            vblk = v_ref[0, pl.ds(k0, SB_TK), :]
            zero = jnp.zeros_like(kblk)
            kk = jnp.concatenate([jnp.where(low_half, kblk, zero),
                                  jnp.where(low_half, zero, kblk)], axis=0)
            vv = jnp.concatenate([jnp.where(low_half, vblk, zero),
                                  jnp.where(low_half, zero, vblk)], axis=0)
            def mask_top(x):
                top = jnp.where(strict_lower, x[:SB_TK], 0.0)
                return top if r0 == SB_TQ - SB_TK else jnp.concatenate([top, x[SB_TK:]], axis=0)

            q = q_ref[0, pl.ds(q0 + r0, SB_TQ - r0), :]
            u2 = lax.dot_general(q, kk, _NT, preferred_element_type=jnp.float32)
            sps = []
            for hd in range(2):
                u = u2[:, hd * SB_TK:(hd + 1) * SB_TK]
                neg_abs = lax.bitcast_convert_type(
                    lax.bitcast_convert_type(u, jnp.int32) | jnp.int32(-2 ** 31), jnp.float32)
                sp = jnp.maximum(u, 0.0) + jnp.log(1.0 + jnp.exp2(neg_abs)) * LOG2E
                if diag:
                    sp = mask_top(sp)
                sps.append(sp.astype(jnp.bfloat16))
            sums = jnp.dot(jnp.concatenate(sps, axis=1), tri,
                           preferred_element_type=jnp.float32)
            a_heads = []
            for hd in range(2):
                u = u2[:, hd * SB_TK:(hd + 1) * SB_TK]
                s_in = sums[:, hd * SB_TK:(hd + 1) * SB_TK]
                carry = carry_ref[hd, r0:, :]
                a = jnp.exp2(u - (carry + s_in))
                if diag:
                    a = mask_top(a)
                carry_ref[hd, r0:, :] = carry + jnp.broadcast_to(s_in[:, 0:1], s_in.shape)
                a_heads.append(a.astype(jnp.bfloat16))
            acc_ref[r0:, :] += jnp.dot(jnp.concatenate(a_heads, axis=1), vv,
                                       preferred_element_type=jnp.float32)

        for d in range(n_sub - 1, -1, -1):
            k_block(pl.multiple_of(q0 + d * SB_TK, SB_TK), d * SB_TK, True)

        def chunk(i, _):
            base = q0 - (i + 1) * SB_CK
            for d in range(SB_CK // SB_TK - 1, -1, -1):
                k_block(pl.multiple_of(base + d * SB_TK, SB_TK), 0, False)
            return 0

        lax.fori_loop(0, qi * (SB_TQ // SB_CK), chunk, 0)
        o_ref[0, pl.ds(q0, SB_TQ), :] = acc_ref[...]
        return 0

    lax.fori_loop(0, nq, q_block, 0)


def _sb(proj):
    b, s, _ = proj.shape
    npair = N_SB_HEADS // 2
    qblk, kblk, vblk = COL_QB // LANES, COL_KB // LANES, COL_VB // LANES
    return pl.pallas_call(
        _sb_kernel,
        grid=(b, npair),
        in_specs=[
            pl.BlockSpec((1, s, LANES), lambda i, p: (i, 0, qblk + p)),
            pl.BlockSpec((1, s, LANES), lambda i, p: (i, 0, kblk + p)),
            pl.BlockSpec((1, s, LANES), lambda i, p: (i, 0, vblk + p)),
        ],
        out_specs=pl.BlockSpec((1, s, LANES), lambda i, p: (i, 0, p)),
        out_shape=jax.ShapeDtypeStruct((b, s, SB_W), jnp.float32),
        scratch_shapes=[pltpu.VMEM((2, SB_TQ, SB_TK), jnp.float32),
                        pltpu.VMEM((SB_TQ, LANES), jnp.float32)],
        compiler_params=pltpu.CompilerParams(
            dimension_semantics=("arbitrary", "arbitrary")),
        name="sb",
    )(proj, proj, proj)


def _out_proj_kernel(x_ref, ya_ref, yb_ref, ga_ref, gb_ref, w_ref, o_ref):
    y = jnp.concatenate([_rms(ya_ref[...], ga_ref[...]), _rms(yb_ref[...], gb_ref[...])],
                        axis=1).astype(jnp.bfloat16)
    o_ref[...] = x_ref[...] + jnp.dot(y, w_ref[...], preferred_element_type=jnp.float32)


def _out_proj(x, ya, yb, ga, gb, w):
    n = x.shape[0]
    return pl.pallas_call(
        _out_proj_kernel,
        grid=(n // OUT_TM,),
        in_specs=[
            pl.BlockSpec((OUT_TM, D_MODEL), lambda i: (i, 0)),
            pl.BlockSpec((OUT_TM, SWA_Q), lambda i: (i, 0)),
            pl.BlockSpec((OUT_TM, SB_W), lambda i: (i, 0)),
            pl.BlockSpec((1, SWA_Q), lambda i: (0, 0)),
            pl.BlockSpec((1, SB_W), lambda i: (0, 0)),
            pl.BlockSpec((SWA_Q + SB_W, D_MODEL), lambda i: (0, 0)),
        ],
        out_specs=pl.BlockSpec((OUT_TM, D_MODEL), lambda i: (i, 0)),
        out_shape=jax.ShapeDtypeStruct((n, D_MODEL), jnp.float32),
        compiler_params=pltpu.CompilerParams(
            dimension_semantics=("arbitrary",),
            vmem_limit_bytes=VMEM_LIMIT),
        name="out_proj",
    )(x, ya, yb, ga, gb, w)


def _proj_col_scale():
    qb0 = SWA_Q + 2 * N_SWA_KV * HEAD_DIM
    s = jnp.ones((IN_W,), jnp.float32)
    s = s.at[:SWA_Q].set(HEAD_DIM ** -0.5 * LOG2E)
    s = s.at[qb0:qb0 + SB_W].set(HEAD_DIM ** -0.5 * LOG2E)
    return s.reshape(1, IN_W)


def kernel(x, ffn1_norm, ffn1_w_gate, ffn1_w_up, ffn1_w_down, mix_norm, w_in, swa_sinks,
           swa_out_norm, sb_out_norm, w_out, ffn2_norm, ffn2_w_gate, ffn2_w_up, ffn2_w_down,
           final_norm):
    b, s, d = x.shape
    depth = ffn1_norm.shape[0]
    xf = x.reshape(b * s, d)
    i = jnp.arange(1, N_SWA_HEADS + 1, dtype=jnp.float32)
    slopes = jnp.exp2(-8.0 * i / N_SWA_HEADS)
    fg = final_norm.reshape(1, d)
    col_scale = _proj_col_scale()
    for l in range(depth):
        xf = _ffn(xf, ffn1_norm[l].reshape(1, d), ffn1_w_gate[l], ffn1_w_up[l], ffn1_w_down[l],
                  fg, final_norm=False)

        proj = _in_proj(xf, mix_norm[l].reshape(1, d), w_in[l].astype(jnp.bfloat16), col_scale)
        proj = proj.reshape(b, s, PROJ_W)
        ya = _swa(proj, slopes, swa_sinks[l])
        yb = _sb(proj)
        xf = _out_proj(xf, ya.reshape(b * s, SWA_Q), yb.reshape(b * s, SB_W),
                       swa_out_norm[l].reshape(1, SWA_Q), sb_out_norm[l].reshape(1, SB_W),
                       w_out[l].astype(jnp.bfloat16))

        xf = _ffn(xf, ffn2_norm[l].reshape(1, d), ffn2_w_gate[l], ffn2_w_up[l], ffn2_w_down[l],
                  fg, final_norm=(l == depth - 1))
    return xf.reshape(b, s, d)
```

```python
import functools
import math

import jax
import jax.numpy as jnp
from jax import lax
from jax.experimental import pallas as pl
from jax.experimental.pallas import tpu as pltpu

D_MODEL = 2048
HEAD_DIM = 64
N_SWA_HEADS = 16
N_SWA_KV = 4
N_SB_HEADS = 16
WINDOW = 128
BLOCK = 128
D_FF = 5504
EPS = 1e-6
LOG2E = math.log2(math.e)

LANES = 128
FFN_TM = 1024
FFN_TF = 512
FFN_HEAD_TF = 256
PROJ_TM = 1024
PROJ_TN = 1536
OUT_TM = 512
SWA_QB = 16
SB_TQ = 2048
SB_TK = 128
SB_CK = 512
VMEM_LIMIT = 56 * 1024 * 1024

SWA_Q = N_SWA_HEADS * HEAD_DIM
SB_W = N_SB_HEADS * HEAD_DIM
COL_KA = SWA_Q
COL_VA = COL_KA + 2 * N_SWA_KV * HEAD_DIM
COL_QB = COL_VA + 2 * N_SWA_KV * HEAD_DIM
COL_KB = COL_QB + SB_W
COL_VB = COL_KB + SB_W
PROJ_W = COL_VB + SB_W
IN_W = SWA_Q + 2 * N_SWA_KV * HEAD_DIM + 3 * SB_W
assert PROJ_TN == SWA_Q + 2 * N_SWA_KV * HEAD_DIM and IN_W % PROJ_TN == 0

_NT = (((1,), (1,)), ((), ()))


def _rms(x, g):
    return x * lax.rsqrt(jnp.mean(x * x, axis=-1, keepdims=True) + EPS) * g


def _ffn_kernel(*refs, tf, final_norm, cast_weights):
    if cast_weights:
        (x_ref, g_ref, wg_ref, wu_ref, wd_ref, fg_ref,
         o_ref, wg_out, wu_out, wd_out, h_ref) = refs
    else:
        x_ref, g_ref, wg_ref, wu_ref, wd_ref, fg_ref, _, o_ref, h_ref = refs
    j = pl.program_id(1)
    last = pl.cdiv(D_FF, tf) - 1
    tail = D_FF - last * tf

    def half_swiglu(h, width):
        wg, wu, wd = wg_ref[:, :width], wu_ref[:, :width], wd_ref[:width, :]
        if cast_weights:
            wg, wu, wd = (w.astype(jnp.bfloat16) for w in (wg, wu, wd))
            wg_out[:, :width] = wg
            wu_out[:, :width] = wu
            wd_out[:width, :] = wd
        gate = jnp.dot(h, wg, preferred_element_type=jnp.float32)
        up = jnp.dot(h, wu, preferred_element_type=jnp.float32)
        act = (gate * jax.nn.sigmoid(gate) * up * 0.5).astype(jnp.bfloat16)
        return jnp.dot(act, wd, preferred_element_type=jnp.float32)

    @pl.when(j == 0)
    def _():
        x = x_ref[...]
        h = _rms(x, g_ref[...]).astype(jnp.bfloat16)
        h_ref[...] = h
        o_ref[...] = x + half_swiglu(h, tf)

    @pl.when(jnp.logical_and(j > 0, j < last))
    def _():
        o_ref[...] += half_swiglu(h_ref[...], tf)

    @pl.when(j == last)
    def _():
        y = o_ref[...] + half_swiglu(h_ref[...], tail)
        o_ref[...] = _rms(y, fg_ref[...]) if final_norm else y


def _ffn(x, g, w_gate, w_up, w_down, fg, final_norm):
    n = x.shape[0]
    suffix = "_final" if final_norm else ""
    params = pltpu.CompilerParams(dimension_semantics=("arbitrary", "arbitrary"),
                                  vmem_limit_bytes=VMEM_LIMIT)
    vec = pl.BlockSpec((1, D_MODEL), lambda i, j: (0, 0))

    def weight_specs(tf):
        return [pl.BlockSpec((D_MODEL, tf), lambda i, j: (0, j)),
                pl.BlockSpec((D_MODEL, tf), lambda i, j: (0, j)),
                pl.BlockSpec((tf, D_MODEL), lambda i, j: (j, 0))]

    out_shape = jax.ShapeDtypeStruct((n, D_MODEL), jnp.float32)
    head_out, wg, wu, wd = pl.pallas_call(
        functools.partial(_ffn_kernel, tf=FFN_HEAD_TF, final_norm=final_norm, cast_weights=True),
        grid=(1, pl.cdiv(D_FF, FFN_HEAD_TF)),
        in_specs=[pl.BlockSpec((FFN_TM, D_MODEL), lambda i, j: (0, 0), pipeline_mode=pl.Buffered(1)),
                  vec, *weight_specs(FFN_HEAD_TF), vec],
        out_specs=[pl.BlockSpec((FFN_TM, D_MODEL), lambda i, j: (0, 0)),
                   *weight_specs(FFN_HEAD_TF)],
        out_shape=[out_shape,
                   jax.ShapeDtypeStruct(w_gate.shape, jnp.bfloat16),
                   jax.ShapeDtypeStruct(w_up.shape, jnp.bfloat16),
                   jax.ShapeDtypeStruct(w_down.shape, jnp.bfloat16)],
        scratch_shapes=[pltpu.VMEM((FFN_TM, D_MODEL), jnp.bfloat16)],
        compiler_params=params,
        name="ffn_head" + suffix,
    )(x, g, w_gate, w_up, w_down, fg)

    return pl.pallas_call(
        functools.partial(_ffn_kernel, tf=FFN_TF, final_norm=final_norm, cast_weights=False),
        grid=(n // FFN_TM - 1, pl.cdiv(D_FF, FFN_TF)),
        in_specs=[pl.BlockSpec((FFN_TM, D_MODEL), lambda i, j: (i + 1, 0)),
                  vec, *weight_specs(FFN_TF), vec,
                  pl.BlockSpec(memory_space=pl.ANY)],
        out_specs=pl.BlockSpec((FFN_TM, D_MODEL), lambda i, j: (i + 1, 0)),
        out_shape=out_shape,
        input_output_aliases={6: 0},
        scratch_shapes=[pltpu.VMEM((FFN_TM, D_MODEL), jnp.bfloat16)],
        compiler_params=params,
        name="ffn" + suffix,
    )(x, g, wg, wu, wd, fg, head_out)


def _in_proj_kernel(x_ref, g_ref, w_ref, s_ref, o_ref, h_ref):
    def project(h):
        acc = jnp.dot(h, w_ref[...], preferred_element_type=jnp.float32)
        return acc * s_ref[...]

    @pl.when(pl.program_id(1) == 0)
    def _():
        h = _rms(x_ref[...], g_ref[...]).astype(jnp.bfloat16)
        h_ref[...] = h
        p = project(h)
        o_ref[:, :SWA_Q] = p[:, :SWA_Q].astype(jnp.bfloat16)
        for g in range(2 * N_SWA_KV):
            head = p[:, SWA_Q + g * HEAD_DIM:SWA_Q + (g + 1) * HEAD_DIM]
            o_ref[:, COL_KA + g * LANES:COL_KA + (g + 1) * LANES] = jnp.concatenate(
                [head, head], axis=1).astype(jnp.bfloat16)

    for jb in range(1, IN_W // PROJ_TN):
        @pl.when(pl.program_id(1) == jb)
        def _(jb=jb):
            start = jb * PROJ_TN + PROJ_W - IN_W
            o_ref[:, start:start + PROJ_TN] = project(h_ref[...]).astype(jnp.bfloat16)


def _in_proj(x, g, w, col_scale):
    n = x.shape[0]
    grid = (n // PROJ_TM, IN_W // PROJ_TN)
    return pl.pallas_call(
        _in_proj_kernel,
        grid=grid,
        in_specs=[
            pl.BlockSpec((PROJ_TM, D_MODEL), lambda i, j: (i, 0)),
            pl.BlockSpec((1, D_MODEL), lambda i, j: (0, 0)),
            pl.BlockSpec((D_MODEL, PROJ_TN), lambda i, j: (0, j)),
            pl.BlockSpec((1, PROJ_TN), lambda i, j: (0, j)),
        ],
        out_specs=pl.BlockSpec((PROJ_TM, PROJ_W), lambda i, j: (i, 0)),
        out_shape=jax.ShapeDtypeStruct((n, PROJ_W), jnp.bfloat16),
        scratch_shapes=[pltpu.VMEM((PROJ_TM, D_MODEL), jnp.bfloat16)],
        compiler_params=pltpu.CompilerParams(
            dimension_semantics=("arbitrary", "arbitrary"),
            vmem_limit_bytes=VMEM_LIMIT),
        name="in_proj",
    )(x, g, w, col_scale)


def _swa_kernel(slopes_ref, sinks_ref, q_ref, k_ref, v_ref, o_ref, bias_ref):
    g = pl.program_id(1)
    n_heads = N_SWA_HEADS // N_SWA_KV
    nb = q_ref.shape[1] // BLOCK
    row = lax.broadcasted_iota(jnp.int32, (BLOCK, BLOCK), 0)
    col = lax.broadcasted_iota(jnp.int32, (BLOCK, BLOCK), 1)
    low_half = col < HEAD_DIM
    dist_p = (row + BLOCK - col).astype(jnp.float32)
    dist_c = (row - col).astype(jnp.float32)
    for jh in range(n_heads):
        slope2 = slopes_ref[g * n_heads + jh] * LOG2E
        bias_ref[jh, :, :BLOCK] = jnp.where(col > row, -slope2 * dist_p, -jnp.inf)
        bias_ref[jh, :, BLOCK:] = jnp.where(col <= row, -slope2 * dist_c, -jnp.inf)
    ones = jnp.ones((BLOCK, LANES), jnp.bfloat16)
    prev_cols = lax.broadcasted_iota(jnp.int32, (BLOCK, 2 * BLOCK), 1) < BLOCK

    def group(gi, _):
        for t in range(SWA_QB):
            qb = gi * SWA_QB + t
            q0 = pl.multiple_of(qb * BLOCK, BLOCK)
            p0 = pl.multiple_of(jnp.maximum(qb - 1, 0) * BLOCK, BLOCK)
            kk = jnp.concatenate([k_ref[0, pl.ds(p0, BLOCK), :],
                                  k_ref[0, pl.ds(q0, BLOCK), :]], axis=0)
            vv = jnp.concatenate(
                [jnp.concatenate([v_ref[0, pl.ds(p0, BLOCK), :], ones], axis=1),
                 jnp.concatenate([v_ref[0, pl.ds(q0, BLOCK), :], ones], axis=1)], axis=0)
            q = q_ref[0, pl.ds(q0, BLOCK), :]
            outs = []
            for jh in range(n_heads):
                sink2 = sinks_ref[g * n_heads + jh] * LOG2E
                qpair = q[:, (jh // 2) * LANES:(jh // 2 + 1) * LANES]
                keep = low_half if jh % 2 == 0 else jnp.logical_not(low_half)
                qh = jnp.where(keep, qpair, jnp.zeros_like(qpair))
                bias = bias_ref[jh]
                if t == 0:
                    bias = jnp.where(jnp.logical_and(prev_cols, qb == 0), -jnp.inf, bias)
                s = lax.dot_general(qh, kk, _NT, preferred_element_type=jnp.float32) + bias
                m = jnp.maximum(jnp.max(s, axis=-1, keepdims=True), sink2)
                e = jnp.exp2(s - m).astype(jnp.bfloat16)
                pv = jnp.dot(e, vv, preferred_element_type=jnp.float32)
                outs.append(pv[:, :LANES] / (pv[:, LANES:] + jnp.exp2(sink2 - m)))
            o_ref[0, pl.ds(q0, BLOCK), :] = jnp.concatenate(
                [jnp.where(low_half, outs[0], outs[1]), jnp.where(low_half, outs[2], outs[3])],
                axis=1)
        return 0

    lax.fori_loop(0, nb // SWA_QB, group, 0)


def _swa(proj, slopes, sinks):
    b, s, _ = proj.shape
    kblk, vblk = COL_KA // LANES, COL_VA // LANES
    smem = pl.BlockSpec(memory_space=pltpu.SMEM)
    return pl.pallas_call(
        _swa_kernel,
        grid=(b, N_SWA_KV),
        in_specs=[
            smem, smem,
            pl.BlockSpec((1, s, 2 * LANES), lambda i, g: (i, 0, g)),
            pl.BlockSpec((1, s, LANES), lambda i, g: (i, 0, kblk + g)),
            pl.BlockSpec((1, s, LANES), lambda i, g: (i, 0, vblk + g)),
        ],
        out_specs=pl.BlockSpec((1, s, 2 * LANES), lambda i, g: (i, 0, g)),
        out_shape=jax.ShapeDtypeStruct((b, s, SWA_Q), jnp.float32),
        scratch_shapes=[pltpu.VMEM((N_SWA_HEADS // N_SWA_KV, BLOCK, 2 * BLOCK), jnp.float32)],
        compiler_params=pltpu.CompilerParams(
            dimension_semantics=("arbitrary", "arbitrary")),
        name="swa",
    )(slopes, sinks, proj, proj, proj)


def _sb_kernel(q_ref, k_ref, v_ref, o_ref, carry_ref, acc_ref):
    s_len = q_ref.shape[1]
    nq = s_len // SB_TQ
    n_sub = SB_TQ // SB_TK
    r = lax.broadcasted_iota(jnp.int32, (2 * SB_TK, 2 * SB_TK), 0)
    c = lax.broadcasted_iota(jnp.int32, (2 * SB_TK, 2 * SB_TK), 1)
    tri = jnp.where(jnp.logical_and(r > c, (r < SB_TK) == (c < SB_TK)), 1.0, 0.0
                    ).astype(jnp.bfloat16)
    low_half = lax.broadcasted_iota(jnp.int32, (SB_TK, LANES), 1) < HEAD_DIM
    strict_lower = (lax.broadcasted_iota(jnp.int32, (SB_TK, SB_TK), 1)
                    < lax.broadcasted_iota(jnp.int32, (SB_TK, SB_TK), 0))

    def q_block(qi, _):
        q0 = pl.multiple_of(qi * SB_TQ, SB_TQ)
        carry_ref[...] = jnp.zeros_like(carry_ref)
        acc_ref[...] = jnp.zeros_like(acc_ref)

        def k_block(k0, r0, diag):
            kblk = k_ref[0, pl.ds(k0, SB_TK), :]
            vblk = v_ref[0, pl.ds(k0, SB_TK), :]
            zero = jnp.zeros_like(kblk)
            kk = jnp.concatenate([jnp.where(low_half, kblk, zero),
                                  jnp.where(low_half, zero, kblk)], axis=0)
            vv = jnp.concatenate([jnp.where(low_half, vblk, zero),
                                  jnp.where(low_half, zero, vblk)], axis=0)
            def mask_top(x):
                top = jnp.where(strict_lower, x[:SB_TK], 0.0)
                return top if r0 == SB_TQ - SB_TK else jnp.concatenate([top, x[SB_TK:]], axis=0)

            q = q_ref[0, pl.ds(q0 + r0, SB_TQ - r0), :]
            u2 = lax.dot_general(q, kk, _NT, preferred_element_type=jnp.float32)
            sps = []
            for hd in range(2):
                u = u2[:, hd * SB_TK:(hd + 1) * SB_TK]
                neg_abs = lax.bitcast_convert_type(
                    lax.bitcast_convert_type(u, jnp.int32) | jnp.int32(-2 ** 31), jnp.float32)
                sp = jnp.maximum(u, 0.0) + jnp.log(1.0 + jnp.exp2(neg_abs)) * LOG2E
                sps.append(mask_top(sp) if diag else sp)
            later = jnp.dot(jnp.concatenate([sp.astype(jnp.bfloat16) for sp in sps], axis=1), tri,
                            preferred_element_type=jnp.float32)
            a_heads = []
            for hd in range(2):
                u = u2[:, hd * SB_TK:(hd + 1) * SB_TK]
                s_incl = later[:, hd * SB_TK:(hd + 1) * SB_TK] + sps[hd]
                carry = carry_ref[hd, r0:, :]
                a = jnp.exp2(u - (carry + s_incl))
                if diag:
                    a = mask_top(a)
                carry_ref[hd, r0:, :] = carry + jnp.broadcast_to(s_incl[:, 0:1], s_incl.shape)
                a_heads.append(a.astype(jnp.bfloat16))
            acc_ref[r0:, :] += jnp.dot(jnp.concatenate(a_heads, axis=1), vv,
                                       preferred_element_type=jnp.float32)

        for d in range(n_sub - 1, -1, -1):
            k_block(pl.multiple_of(q0 + d * SB_TK, SB_TK), d * SB_TK, True)

        def chunk(i, _):
            base = q0 - (i + 1) * SB_CK
            for d in range(SB_CK // SB_TK - 1, -1, -1):
                k_block(pl.multiple_of(base + d * SB_TK, SB_TK), 0, False)
            return 0

        lax.fori_loop(0, qi * (SB_TQ // SB_CK), chunk, 0)
        o_ref[0, pl.ds(q0, SB_TQ), :] = acc_ref[...]
        return 0

    lax.fori_loop(0, nq, q_block, 0)


def _sb(proj):
    b, s, _ = proj.shape
    npair = N_SB_HEADS // 2
    qblk, kblk, vblk = COL_QB // LANES, COL_KB // LANES, COL_VB // LANES
    return pl.pallas_call(
        _sb_kernel,
        grid=(b, npair),
        in_specs=[
            pl.BlockSpec((1, s, LANES), lambda i, p: (i, 0, qblk + p)),
            pl.BlockSpec((1, s, LANES), lambda i, p: (i, 0, kblk + p)),
            pl.BlockSpec((1, s, LANES), lambda i, p: (i, 0, vblk + p)),
        ],
        out_specs=pl.BlockSpec((1, s, LANES), lambda i, p: (i, 0, p)),
        out_shape=jax.ShapeDtypeStruct((b, s, SB_W), jnp.float32),
        scratch_shapes=[pltpu.VMEM((2, SB_TQ, SB_TK), jnp.float32),
                        pltpu.VMEM((SB_TQ, LANES), jnp.float32)],
        compiler_params=pltpu.CompilerParams(
            dimension_semantics=("arbitrary", "arbitrary")),
        name="sb",
    )(proj, proj, proj)


def _out_proj_kernel(x_ref, ya_ref, yb_ref, ga_ref, gb_ref, w_ref, o_ref):
    y = jnp.concatenate([_rms(ya_ref[...], ga_ref[...]), _rms(yb_ref[...], gb_ref[...])],
                        axis=1).astype(jnp.bfloat16)
    o_ref[...] = x_ref[...] + jnp.dot(y, w_ref[...], preferred_element_type=jnp.float32)


def _out_proj(x, ya, yb, ga, gb, w):
    n = x.shape[0]
    return pl.pallas_call(
        _out_proj_kernel,
        grid=(n // OUT_TM,),
        in_specs=[
            pl.BlockSpec((OUT_TM, D_MODEL), lambda i: (i, 0)),
            pl.BlockSpec((OUT_TM, SWA_Q), lambda i: (i, 0)),
            pl.BlockSpec((OUT_TM, SB_W), lambda i: (i, 0)),
            pl.BlockSpec((1, SWA_Q), lambda i: (0, 0)),
            pl.BlockSpec((1, SB_W), lambda i: (0, 0)),
            pl.BlockSpec((SWA_Q + SB_W, D_MODEL), lambda i: (0, 0)),
        ],
        out_specs=pl.BlockSpec((OUT_TM, D_MODEL), lambda i: (i, 0)),
        out_shape=jax.ShapeDtypeStruct((n, D_MODEL), jnp.float32),
        compiler_params=pltpu.CompilerParams(
            dimension_semantics=("arbitrary",),
            vmem_limit_bytes=VMEM_LIMIT),
        name="out_proj",
    )(x, ya, yb, ga, gb, w)


def _proj_col_scale():
    qb0 = SWA_Q + 2 * N_SWA_KV * HEAD_DIM
    s = jnp.ones((IN_W,), jnp.float32)
    s = s.at[:SWA_Q].set(HEAD_DIM ** -0.5 * LOG2E)
    s = s.at[qb0:qb0 + SB_W].set(HEAD_DIM ** -0.5 * LOG2E)
    return s.reshape(1, IN_W)


def kernel(x, ffn1_norm, ffn1_w_gate, ffn1_w_up, ffn1_w_down, mix_norm, w_in, swa_sinks,
           swa_out_norm, sb_out_norm, w_out, ffn2_norm, ffn2_w_gate, ffn2_w_up, ffn2_w_down,
           final_norm):
    b, s, d = x.shape
    depth = ffn1_norm.shape[0]
    xf = x.reshape(b * s, d)
    i = jnp.arange(1, N_SWA_HEADS + 1, dtype=jnp.float32)
    slopes = jnp.exp2(-8.0 * i / N_SWA_HEADS)
    fg = final_norm.reshape(1, d)
    col_scale = _proj_col_scale()
    for l in range(depth):
        xf = _ffn(xf, ffn1_norm[l].reshape(1, d), ffn1_w_gate[l], ffn1_w_up[l], ffn1_w_down[l],
                  fg, final_norm=False)

        proj = _in_proj(xf, mix_norm[l].reshape(1, d), w_in[l].astype(jnp.bfloat16), col_scale)
        proj = proj.reshape(b, s, PROJ_W)
        ya = _swa(proj, slopes, swa_sinks[l])
        yb = _sb(proj)
        xf = _out_proj(xf, ya.reshape(b * s, SWA_Q), yb.reshape(b * s, SB_W),
                       swa_out_norm[l].reshape(1, SWA_Q), sb_out_norm[l].reshape(1, SB_W),
                       w_out[l].astype(jnp.bfloat16))

        xf = _ffn(xf, ffn2_norm[l].reshape(1, d), ffn2_w_gate[l], ffn2_w_up[l], ffn2_w_down[l],
                  fg, final_norm=(l == depth - 1))
    return xf.reshape(b, s, d)
```

```python
import functools
import math

import jax
import jax.numpy as jnp
from jax import lax
from jax.experimental import pallas as pl
from jax.experimental.pallas import tpu as pltpu

D_MODEL = 2048
HEAD_DIM = 64
N_SWA_HEADS = 16
N_SWA_KV = 4
N_SB_HEADS = 16
WINDOW = 128
BLOCK = 128
D_FF = 5504
EPS = 1e-6
LOG2E = math.log2(math.e)

LANES = 128
FFN_TM = 1024
FFN_TF = 512
FFN_HEAD_TF = 256
PROJ_TM = 1024
PROJ_TN = 1536
OUT_TM = 512
SWA_QB = 16
SB_TQ = 2048
SB_TK = 128
SB_CK = 512
VMEM_LIMIT = 56 * 1024 * 1024

SWA_Q = N_SWA_HEADS * HEAD_DIM
SB_W = N_SB_HEADS * HEAD_DIM
COL_KA = SWA_Q
COL_VA = COL_KA + 2 * N_SWA_KV * HEAD_DIM
COL_QB = COL_VA + 2 * N_SWA_KV * HEAD_DIM
COL_KB = COL_QB + SB_W
COL_VB = COL_KB + SB_W
PROJ_W = COL_VB + SB_W
IN_W = SWA_Q + 2 * N_SWA_KV * HEAD_DIM + 3 * SB_W
assert PROJ_TN == SWA_Q + 2 * N_SWA_KV * HEAD_DIM and IN_W % PROJ_TN == 0

_NT = (((1,), (1,)), ((), ()))


def _rms(x, g):
    return x * lax.rsqrt(jnp.mean(x * x, axis=-1, keepdims=True) + EPS) * g


def _ffn_kernel(*refs, tf, final_norm, cast_weights):
    if cast_weights:
        (x_ref, g_ref, wg_ref, wu_ref, wd_ref, fg_ref,
         o_ref, wg_out, wu_out, wd_out, h_ref) = refs
    else:
        x_ref, g_ref, wg_ref, wu_ref, wd_ref, fg_ref, _, o_ref, h_ref = refs
    j = pl.program_id(1)
    last = pl.cdiv(D_FF, tf) - 1
    tail = D_FF - last * tf

    def half_swiglu(h, width):
        wg, wu, wd = wg_ref[:, :width], wu_ref[:, :width], wd_ref[:width, :]
        if cast_weights:
            wg, wu, wd = (w.astype(jnp.bfloat16) for w in (wg, wu, wd))
            wg_out[:, :width] = wg
            wu_out[:, :width] = wu
            wd_out[:width, :] = wd
        gate = jnp.dot(h, wg, preferred_element_type=jnp.float32)
        up = jnp.dot(h, wu, preferred_element_type=jnp.float32)
        act = (gate * jax.nn.sigmoid(gate) * up * 0.5).astype(jnp.bfloat16)
        return jnp.dot(act, wd, preferred_element_type=jnp.float32)

    @pl.when(j == 0)
    def _():
        x = x_ref[...]
        h = _rms(x, g_ref[...]).astype(jnp.bfloat16)
        h_ref[...] = h
        o_ref[...] = x + half_swiglu(h, tf)

    @pl.when(jnp.logical_and(j > 0, j < last))
    def _():
        o_ref[...] += half_swiglu(h_ref[...], tf)

    @pl.when(j == last)
    def _():
        y = o_ref[...] + half_swiglu(h_ref[...], tail)
        o_ref[...] = _rms(y, fg_ref[...]) if final_norm else y


def _ffn(x, g, w_gate, w_up, w_down, fg, final_norm):
    n = x.shape[0]
    suffix = "_final" if final_norm else ""
    params = pltpu.CompilerParams(dimension_semantics=("arbitrary", "arbitrary"),
                                  vmem_limit_bytes=VMEM_LIMIT)
    vec = pl.BlockSpec((1, D_MODEL), lambda i, j: (0, 0))

    def weight_specs(tf):
        return [pl.BlockSpec((D_MODEL, tf), lambda i, j: (0, j)),
                pl.BlockSpec((D_MODEL, tf), lambda i, j: (0, j)),
                pl.BlockSpec((tf, D_MODEL), lambda i, j: (j, 0))]

    out_shape = jax.ShapeDtypeStruct((n, D_MODEL), jnp.float32)
    head_out, wg, wu, wd = pl.pallas_call(
        functools.partial(_ffn_kernel, tf=FFN_HEAD_TF, final_norm=final_norm, cast_weights=True),
        grid=(1, pl.cdiv(D_FF, FFN_HEAD_TF)),
        in_specs=[pl.BlockSpec((FFN_TM, D_MODEL), lambda i, j: (0, 0), pipeline_mode=pl.Buffered(1)),
                  vec, *weight_specs(FFN_HEAD_TF), vec],
        out_specs=[pl.BlockSpec((FFN_TM, D_MODEL), lambda i, j: (0, 0)),
                   *weight_specs(FFN_HEAD_TF)],
        out_shape=[out_shape,
                   jax.ShapeDtypeStruct(w_gate.shape, jnp.bfloat16),
                   jax.ShapeDtypeStruct(w_up.shape, jnp.bfloat16),
                   jax.ShapeDtypeStruct(w_down.shape, jnp.bfloat16)],
        scratch_shapes=[pltpu.VMEM((FFN_TM, D_MODEL), jnp.bfloat16)],
        compiler_params=params,
        name="ffn_head" + suffix,
    )(x, g, w_gate, w_up, w_down, fg)

    return pl.pallas_call(
        functools.partial(_ffn_kernel, tf=FFN_TF, final_norm=final_norm, cast_weights=False),
        grid=(n // FFN_TM - 1, pl.cdiv(D_FF, FFN_TF)),
        in_specs=[pl.BlockSpec((FFN_TM, D_MODEL), lambda i, j: (i + 1, 0)),
                  vec, *weight_specs(FFN_TF), vec,
                  pl.BlockSpec(memory_space=pl.ANY)],
        out_specs=pl.BlockSpec((FFN_TM, D_MODEL), lambda i, j: (i + 1, 0)),
        out_shape=out_shape,
        input_output_aliases={6: 0},
        scratch_shapes=[pltpu.VMEM((FFN_TM, D_MODEL), jnp.bfloat16)],
        compiler_params=params,
        name="ffn" + suffix,
    )(x, g, wg, wu, wd, fg, head_out)


def _in_proj_kernel(x_ref, g_ref, w_ref, s_ref, o_ref, h_ref):
    def project(h):
        acc = jnp.dot(h, w_ref[...], preferred_element_type=jnp.float32)
        return acc * s_ref[...]

    @pl.when(pl.program_id(1) == 0)
    def _():
        h = _rms(x_ref[...], g_ref[...]).astype(jnp.bfloat16)
        h_ref[...] = h
        p = project(h)
        o_ref[:, :SWA_Q] = p[:, :SWA_Q].astype(jnp.bfloat16)
        for g in range(2 * N_SWA_KV):
            head = p[:, SWA_Q + g * HEAD_DIM:SWA_Q + (g + 1) * HEAD_DIM]
            o_ref[:, COL_KA + g * LANES:COL_KA + (g + 1) * LANES] = jnp.concatenate(
                [head, head], axis=1).astype(jnp.bfloat16)

    for jb in range(1, IN_W // PROJ_TN):
        @pl.when(pl.program_id(1) == jb)
        def _(jb=jb):
            start = jb * PROJ_TN + PROJ_W - IN_W
            o_ref[:, start:start + PROJ_TN] = project(h_ref[...]).astype(jnp.bfloat16)


def _in_proj(x, g, w, col_scale):
    n = x.shape[0]
    grid = (n // PROJ_TM, IN_W // PROJ_TN)
    return pl.pallas_call(
        _in_proj_kernel,
        grid=grid,
        in_specs=[
            pl.BlockSpec((PROJ_TM, D_MODEL), lambda i, j: (i, 0)),
            pl.BlockSpec((1, D_MODEL), lambda i, j: (0, 0)),
            pl.BlockSpec((D_MODEL, PROJ_TN), lambda i, j: (0, j)),
            pl.BlockSpec((1, PROJ_TN), lambda i, j: (0, j)),
        ],
        out_specs=pl.BlockSpec((PROJ_TM, PROJ_W), lambda i, j: (i, 0)),
        out_shape=jax.ShapeDtypeStruct((n, PROJ_W), jnp.bfloat16),
        scratch_shapes=[pltpu.VMEM((PROJ_TM, D_MODEL), jnp.bfloat16)],
        compiler_params=pltpu.CompilerParams(
            dimension_semantics=("arbitrary", "arbitrary"),
            vmem_limit_bytes=VMEM_LIMIT),
        name="in_proj",
    )(x, g, w, col_scale)


def _swa_kernel(slopes_ref, sinks_ref, q_ref, k_ref, v_ref, o_ref, bias_ref):
    g = pl.program_id(1)
    n_heads = N_SWA_HEADS // N_SWA_KV
    nb = q_ref.shape[1] // BLOCK
    row = lax.broadcasted_iota(jnp.int32, (BLOCK, BLOCK), 0)
    col = lax.broadcasted_iota(jnp.int32, (BLOCK, BLOCK), 1)
    low_half = col < HEAD_DIM
    dist_p = (row + BLOCK - col).astype(jnp.float32)
    dist_c = (row - col).astype(jnp.float32)
    for jh in range(n_heads):
        slope2 = slopes_ref[g * n_heads + jh] * LOG2E
        bias_ref[jh, :, :BLOCK] = jnp.where(col > row, -slope2 * dist_p, -jnp.inf)
        bias_ref[jh, :, BLOCK:] = jnp.where(col <= row, -slope2 * dist_c, -jnp.inf)
    ones = jnp.ones((BLOCK, LANES), jnp.bfloat16)
    prev_cols = lax.broadcasted_iota(jnp.int32, (BLOCK, 2 * BLOCK), 1) < BLOCK

    def group(gi, _):
        for t in range(SWA_QB):
            qb = gi * SWA_QB + t
            q0 = pl.multiple_of(qb * BLOCK, BLOCK)
            p0 = pl.multiple_of(jnp.maximum(qb - 1, 0) * BLOCK, BLOCK)
            kk = jnp.concatenate([k_ref[0, pl.ds(p0, BLOCK), :],
                                  k_ref[0, pl.ds(q0, BLOCK), :]], axis=0)
            vv = jnp.concatenate(
                [jnp.concatenate([v_ref[0, pl.ds(p0, BLOCK), :], ones], axis=1),
                 jnp.concatenate([v_ref[0, pl.ds(q0, BLOCK), :], ones], axis=1)], axis=0)
            q = q_ref[0, pl.ds(q0, BLOCK), :]
            outs = []
            for jh in range(n_heads):
                sink2 = sinks_ref[g * n_heads + jh] * LOG2E
                qpair = q[:, (jh // 2) * LANES:(jh // 2 + 1) * LANES]
                keep = low_half if jh % 2 == 0 else jnp.logical_not(low_half)
                qh = jnp.where(keep, qpair, jnp.zeros_like(qpair))
                bias = bias_ref[jh]
                if t == 0:
                    bias = jnp.where(jnp.logical_and(prev_cols, qb == 0), -jnp.inf, bias)
                s = lax.dot_general(qh, kk, _NT, preferred_element_type=jnp.float32) + bias
                m = jnp.maximum(jnp.max(s, axis=-1, keepdims=True), sink2)
                e = jnp.exp2(s - m).astype(jnp.bfloat16)
                pv = jnp.dot(e, vv, preferred_element_type=jnp.float32)
                outs.append(pv[:, :LANES] / (pv[:, LANES:] + jnp.exp2(sink2 - m)))
            o_ref[0, pl.ds(q0, BLOCK), :] = jnp.concatenate(
                [jnp.where(low_half, outs[0], outs[1]), jnp.where(low_half, outs[2], outs[3])],
                axis=1)
        return 0

    lax.fori_loop(0, nb // SWA_QB, group, 0)


def _swa(proj, slopes, sinks):
    b, s, _ = proj.shape
    kblk, vblk = COL_KA // LANES, COL_VA // LANES
    smem = pl.BlockSpec(memory_space=pltpu.SMEM)
    return pl.pallas_call(
        _swa_kernel,
        grid=(b, N_SWA_KV),
        in_specs=[
            smem, smem,
            pl.BlockSpec((1, s, 2 * LANES), lambda i, g: (i, 0, g)),
            pl.BlockSpec((1, s, LANES), lambda i, g: (i, 0, kblk + g)),
            pl.BlockSpec((1, s, LANES), lambda i, g: (i, 0, vblk + g)),
        ],
        out_specs=pl.BlockSpec((1, s, 2 * LANES), lambda i, g: (i, 0, g)),
        out_shape=jax.ShapeDtypeStruct((b, s, SWA_Q), jnp.float32),
        scratch_shapes=[pltpu.VMEM((N_SWA_HEADS // N_SWA_KV, BLOCK, 2 * BLOCK), jnp.float32)],
        compiler_params=pltpu.CompilerParams(
            dimension_semantics=("arbitrary", "arbitrary")),
        name="swa",
    )(slopes, sinks, proj, proj, proj)


def _sb_kernel(q_ref, k_ref, v_ref, o_ref, carry_ref, acc_ref):
    s_len = q_ref.shape[1]
    nq = s_len // SB_TQ
    n_sub = SB_TQ // SB_TK
    r = lax.broadcasted_iota(jnp.int32, (2 * SB_TK, 2 * SB_TK), 0)
    c = lax.broadcasted_iota(jnp.int32, (2 * SB_TK, 2 * SB_TK), 1)
    tri = jnp.where(jnp.logical_and(r > c, (r < SB_TK) == (c < SB_TK)), 1.0, 0.0
                    ).astype(jnp.bfloat16)
    low_half = lax.broadcasted_iota(jnp.int32, (SB_TK, LANES), 1) < HEAD_DIM
    strict_lower = (lax.broadcasted_iota(jnp.int32, (SB_TK, SB_TK), 1)
                    < lax.broadcasted_iota(jnp.int32, (SB_TK, SB_TK), 0))

    def q_block(qi, _):
        q0 = pl.multiple_of(qi * SB_TQ, SB_TQ)
        carry_ref[...] = jnp.zeros_like(carry_ref)
        acc_ref[...] = jnp.zeros_like(acc_ref)

        def k_block(k0, r0, diag):
            kblk = k_ref[0, pl.ds(k0, SB_TK), :]
            vblk = v_ref[0, pl.ds(k0, SB_TK), :]
            zero = jnp.zeros_like(kblk)
            kk = jnp.concatenate([jnp.where(low_half, kblk, zero),
                                  jnp.where(low_half, zero, kblk)], axis=0)
            vv = jnp.concatenate([jnp.where(low_half, vblk, zero),
                                  jnp.where(low_half, zero, vblk)], axis=0)
            def mask_top(x):
                top = jnp.where(strict_lower, x[:SB_TK], 0.0)
                return top if r0 == SB_TQ - SB_TK else jnp.concatenate([top, x[SB_TK:]], axis=0)

            q = q_ref[0, pl.ds(q0 + r0, SB_TQ - r0), :]
            u2 = lax.dot_general(q, kk, _NT, preferred_element_type=jnp.float32)
            sps = []
            for hd in range(2):
                u = u2[:, hd * SB_TK:(hd + 1) * SB_TK]
                neg_abs = lax.bitcast_convert_type(
                    lax.bitcast_convert_type(u, jnp.int32) | jnp.int32(-2 ** 31), jnp.float32)
                sp = jnp.maximum(u, 0.0) + jnp.log(1.0 + jnp.exp2(neg_abs)) * LOG2E
                sps.append(mask_top(sp) if diag else sp)
            later = jnp.dot(jnp.concatenate([sp.astype(jnp.bfloat16) for sp in sps], axis=1), tri,
                            preferred_element_type=jnp.float32)
            a_heads = []
            for hd in range(2):
                u = u2[:, hd * SB_TK:(hd + 1) * SB_TK]
                s_incl = later[:, hd * SB_TK:(hd + 1) * SB_TK] + sps[hd]
                total = carry_ref[hd, r0:, :] + s_incl
                a = jnp.exp2(u - total)
                if diag:
                    a = mask_top(a)
                carry_ref[hd, r0:, :] = jnp.broadcast_to(total[:, 0:1], total.shape)
                a_heads.append(a.astype(jnp.bfloat16))
            acc_ref[r0:, :] += jnp.dot(jnp.concatenate(a_heads, axis=1), vv,
                                       preferred_element_type=jnp.float32)

        for d in range(n_sub - 1, -1, -1):
            k_block(pl.multiple_of(q0 + d * SB_TK, SB_TK), d * SB_TK, True)

        def chunk(i, _):
            base = q0 - (i + 1) * SB_CK
            for d in range(SB_CK // SB_TK - 1, -1, -1):
                k_block(pl.multiple_of(base + d * SB_TK, SB_TK), 0, False)
            return 0

        lax.fori_loop(0, qi * (SB_TQ // SB_CK), chunk, 0)
        o_ref[0, pl.ds(q0, SB_TQ), :] = acc_ref[...]
        return 0

    lax.fori_loop(0, nq, q_block, 0)


def _sb(proj):
    b, s, _ = proj.shape
    npair = N_SB_HEADS // 2
    qblk, kblk, vblk = COL_QB // LANES, COL_KB // LANES, COL_VB // LANES
    return pl.pallas_call(
        _sb_kernel,
        grid=(b, npair),
        in_specs=[
            pl.BlockSpec((1, s, LANES), lambda i, p: (i, 0, qblk + p)),
            pl.BlockSpec((1, s, LANES), lambda i, p: (i, 0, kblk + p)),
            pl.BlockSpec((1, s, LANES), lambda i, p: (i, 0, vblk + p)),
        ],
        out_specs=pl.BlockSpec((1, s, LANES), lambda i, p: (i, 0, p)),
        out_shape=jax.ShapeDtypeStruct((b, s, SB_W), jnp.float32),
        scratch_shapes=[pltpu.VMEM((2, SB_TQ, SB_TK), jnp.float32),
                        pltpu.VMEM((SB_TQ, LANES), jnp.float32)],
        compiler_params=pltpu.CompilerParams(
            dimension_semantics=("arbitrary", "arbitrary")),
        name="sb",
    )(proj, proj, proj)


def _out_proj_kernel(x_ref, ya_ref, yb_ref, ga_ref, gb_ref, w_ref, o_ref):
    y = jnp.concatenate([_rms(ya_ref[...], ga_ref[...]), _rms(yb_ref[...], gb_ref[...])],
                        axis=1).astype(jnp.bfloat16)
    o_ref[...] = x_ref[...] + jnp.dot(y, w_ref[...], preferred_element_type=jnp.float32)


def _out_proj(x, ya, yb, ga, gb, w):
    n = x.shape[0]
    return pl.pallas_call(
        _out_proj_kernel,
        grid=(n // OUT_TM,),
        in_specs=[
            pl.BlockSpec((OUT_TM, D_MODEL), lambda i: (i, 0)),
            pl.BlockSpec((OUT_TM, SWA_Q), lambda i: (i, 0)),
            pl.BlockSpec((OUT_TM, SB_W), lambda i: (i, 0)),
            pl.BlockSpec((1, SWA_Q), lambda i: (0, 0)),
            pl.BlockSpec((1, SB_W), lambda i: (0, 0)),
            pl.BlockSpec((SWA_Q + SB_W, D_MODEL), lambda i: (0, 0)),
        ],
        out_specs=pl.BlockSpec((OUT_TM, D_MODEL), lambda i: (i, 0)),
        out_shape=jax.ShapeDtypeStruct((n, D_MODEL), jnp.float32),
        compiler_params=pltpu.CompilerParams(
            dimension_semantics=("arbitrary",),
            vmem_limit_bytes=VMEM_LIMIT),
        name="out_proj",
    )(x, ya, yb, ga, gb, w)


def _proj_col_scale():
    qb0 = SWA_Q + 2 * N_SWA_KV * HEAD_DIM
    s = jnp.ones((IN_W,), jnp.float32)
    s = s.at[:SWA_Q].set(HEAD_DIM ** -0.5 * LOG2E)
    s = s.at[qb0:qb0 + SB_W].set(HEAD_DIM ** -0.5 * LOG2E)
    return s.reshape(1, IN_W)


def kernel(x, ffn1_norm, ffn1_w_gate, ffn1_w_up, ffn1_w_down, mix_norm, w_in, swa_sinks,
           swa_out_norm, sb_out_norm, w_out, ffn2_norm, ffn2_w_gate, ffn2_w_up, ffn2_w_down,
           final_norm):
    b, s, d = x.shape
    depth = ffn1_norm.shape[0]
    xf = x.reshape(b * s, d)
    i = jnp.arange(1, N_SWA_HEADS + 1, dtype=jnp.float32)
    slopes = jnp.exp2(-8.0 * i / N_SWA_HEADS)
    fg = final_norm.reshape(1, d)
    col_scale = _proj_col_scale()
    for l in range(depth):
        xf = _ffn(xf, ffn1_norm[l].reshape(1, d), ffn1_w_gate[l], ffn1_w_up[l], ffn1_w_down[l],
                  fg, final_norm=False)

        proj = _in_proj(xf, mix_norm[l].reshape(1, d), w_in[l].astype(jnp.bfloat16), col_scale)
        proj = proj.reshape(b, s, PROJ_W)
        ya = _swa(proj, slopes, swa_sinks[l])
        yb = _sb(proj)
        xf = _out_proj(xf, ya.reshape(b * s, SWA_Q), yb.reshape(b * s, SB_W),
                       swa_out_norm[l].reshape(1, SWA_Q), sb_out_norm[l].reshape(1, SB_W),
                       w_out[l].astype(jnp.bfloat16))

        xf = _ffn(xf, ffn2_norm[l].reshape(1, d), ffn2_w_gate[l], ffn2_w_up[l], ffn2_w_down[l],
                  fg, final_norm=(l == depth - 1))
    return xf.reshape(b, s, d)
```

```python
import functools
import math

import jax
import jax.numpy as jnp
from jax import lax
from jax.experimental import pallas as pl
from jax.experimental.pallas import tpu as pltpu

D_MODEL = 2048
HEAD_DIM = 64
N_SWA_HEADS = 16
N_SWA_KV = 4
N_SB_HEADS = 16
WINDOW = 128
BLOCK = 128
D_FF = 5504
EPS = 1e-6
LOG2E = math.log2(math.e)

LANES = 128
FFN_TM = 1024
FFN_TF = 768
FFN_HEAD_TF = 256
PROJ_TM = 1024
PROJ_TN = 1536
OUT_TM = 512
SWA_QB = 16
SB_TQ = 2048
SB_TK = 128
SB_CK = 512
VMEM_LIMIT = 62 * 1024 * 1024

SWA_Q = N_SWA_HEADS * HEAD_DIM
SB_W = N_SB_HEADS * HEAD_DIM
COL_KA = SWA_Q
COL_VA = COL_KA + 2 * N_SWA_KV * HEAD_DIM
COL_QB = COL_VA + 2 * N_SWA_KV * HEAD_DIM
COL_KB = COL_QB + SB_W
COL_VB = COL_KB + SB_W
PROJ_W = COL_VB + SB_W
IN_W = SWA_Q + 2 * N_SWA_KV * HEAD_DIM + 3 * SB_W
assert PROJ_TN == SWA_Q + 2 * N_SWA_KV * HEAD_DIM and IN_W % PROJ_TN == 0

_NT = (((1,), (1,)), ((), ()))


def _rms(x, g):
    return x * lax.rsqrt(jnp.mean(x * x, axis=-1, keepdims=True) + EPS) * g


def _ffn_kernel(*refs, tf, final_norm, cast_weights):
    if cast_weights:
        (x_ref, g_ref, wg_ref, wu_ref, wd_ref, fg_ref,
         o_ref, wg_out, wu_out, wd_out, h_ref) = refs
    else:
        x_ref, g_ref, wg_ref, wu_ref, wd_ref, fg_ref, _, o_ref, h_ref = refs
    j = pl.program_id(1)
    last = pl.cdiv(D_FF, tf) - 1
    tail = D_FF - last * tf

    def half_swiglu(h, width):
        wg, wu, wd = wg_ref[:, :width], wu_ref[:, :width], wd_ref[:width, :]
        if cast_weights:
            wg, wu, wd = (w.astype(jnp.bfloat16) for w in (wg, wu, wd))
            wg_out[:, :width] = wg
            wu_out[:, :width] = wu
            wd_out[:width, :] = wd
        gate = jnp.dot(h, wg, preferred_element_type=jnp.float32)
        up = jnp.dot(h, wu, preferred_element_type=jnp.float32)
        act = (gate * jax.nn.sigmoid(gate) * up * 0.5).astype(jnp.bfloat16)
        return jnp.dot(act, wd, preferred_element_type=jnp.float32)

    @pl.when(j == 0)
    def _():
        x = x_ref[...]
        h = _rms(x, g_ref[...]).astype(jnp.bfloat16)
        h_ref[...] = h
        o_ref[...] = x + half_swiglu(h, tf)

    @pl.when(jnp.logical_and(j > 0, j < last))
    def _():
        o_ref[...] += half_swiglu(h_ref[...], tf)

    @pl.when(j == last)
    def _():
        y = o_ref[...] + half_swiglu(h_ref[...], tail)
        o_ref[...] = _rms(y, fg_ref[...]) if final_norm else y


def _ffn(x, g, w_gate, w_up, w_down, fg, final_norm):
    n = x.shape[0]
    suffix = "_final" if final_norm else ""
    params = pltpu.CompilerParams(dimension_semantics=("arbitrary", "arbitrary"),
                                  vmem_limit_bytes=VMEM_LIMIT)
    vec = pl.BlockSpec((1, D_MODEL), lambda i, j: (0, 0))

    def weight_specs(tf):
        return [pl.BlockSpec((D_MODEL, tf), lambda i, j: (0, j)),
                pl.BlockSpec((D_MODEL, tf), lambda i, j: (0, j)),
                pl.BlockSpec((tf, D_MODEL), lambda i, j: (j, 0))]

    out_shape = jax.ShapeDtypeStruct((n, D_MODEL), jnp.float32)
    head_out, wg, wu, wd = pl.pallas_call(
        functools.partial(_ffn_kernel, tf=FFN_HEAD_TF, final_norm=final_norm, cast_weights=True),
        grid=(1, pl.cdiv(D_FF, FFN_HEAD_TF)),
        in_specs=[pl.BlockSpec((FFN_TM, D_MODEL), lambda i, j: (0, 0), pipeline_mode=pl.Buffered(1)),
                  vec, *weight_specs(FFN_HEAD_TF), vec],
        out_specs=[pl.BlockSpec((FFN_TM, D_MODEL), lambda i, j: (0, 0)),
                   *weight_specs(FFN_HEAD_TF)],
        out_shape=[out_shape,
                   jax.ShapeDtypeStruct(w_gate.shape, jnp.bfloat16),
                   jax.ShapeDtypeStruct(w_up.shape, jnp.bfloat16),
                   jax.ShapeDtypeStruct(w_down.shape, jnp.bfloat16)],
        scratch_shapes=[pltpu.VMEM((FFN_TM, D_MODEL), jnp.bfloat16)],
        compiler_params=params,
        name="ffn_head" + suffix,
    )(x, g, w_gate, w_up, w_down, fg)

    return pl.pallas_call(
        functools.partial(_ffn_kernel, tf=FFN_TF, final_norm=final_norm, cast_weights=False),
        grid=(n // FFN_TM - 1, pl.cdiv(D_FF, FFN_TF)),
        in_specs=[pl.BlockSpec((FFN_TM, D_MODEL), lambda i, j: (i + 1, 0)),
                  vec, *weight_specs(FFN_TF), vec,
                  pl.BlockSpec(memory_space=pl.ANY)],
        out_specs=pl.BlockSpec((FFN_TM, D_MODEL), lambda i, j: (i + 1, 0)),
        out_shape=out_shape,
        input_output_aliases={6: 0},
        scratch_shapes=[pltpu.VMEM((FFN_TM, D_MODEL), jnp.bfloat16)],
        compiler_params=params,
        name="ffn" + suffix,
    )(x, g, wg, wu, wd, fg, head_out)


def _in_proj_kernel(x_ref, g_ref, w_ref, s_ref, o_ref, h_ref):
    def project(h):
        acc = jnp.dot(h, w_ref[...], preferred_element_type=jnp.float32)
        return acc * s_ref[...]

    @pl.when(pl.program_id(1) == 0)
    def _():
        h = _rms(x_ref[...], g_ref[...]).astype(jnp.bfloat16)
        h_ref[...] = h
        p = project(h)
        o_ref[:, :SWA_Q] = p[:, :SWA_Q].astype(jnp.bfloat16)
        for g in range(2 * N_SWA_KV):
            head = p[:, SWA_Q + g * HEAD_DIM:SWA_Q + (g + 1) * HEAD_DIM]
            o_ref[:, COL_KA + g * LANES:COL_KA + (g + 1) * LANES] = jnp.concatenate(
                [head, head], axis=1).astype(jnp.bfloat16)

    for jb in range(1, IN_W // PROJ_TN):
        @pl.when(pl.program_id(1) == jb)
        def _(jb=jb):
            start = jb * PROJ_TN + PROJ_W - IN_W
            o_ref[:, start:start + PROJ_TN] = project(h_ref[...]).astype(jnp.bfloat16)


def _in_proj(x, g, w, col_scale):
    n = x.shape[0]
    grid = (n // PROJ_TM, IN_W // PROJ_TN)
    return pl.pallas_call(
        _in_proj_kernel,
        grid=grid,
        in_specs=[
            pl.BlockSpec((PROJ_TM, D_MODEL), lambda i, j: (i, 0)),
            pl.BlockSpec((1, D_MODEL), lambda i, j: (0, 0)),
            pl.BlockSpec((D_MODEL, PROJ_TN), lambda i, j: (0, j)),
            pl.BlockSpec((1, PROJ_TN), lambda i, j: (0, j)),
        ],
        out_specs=pl.BlockSpec((PROJ_TM, PROJ_W), lambda i, j: (i, 0)),
        out_shape=jax.ShapeDtypeStruct((n, PROJ_W), jnp.bfloat16),
        scratch_shapes=[pltpu.VMEM((PROJ_TM, D_MODEL), jnp.bfloat16)],
        compiler_params=pltpu.CompilerParams(
            dimension_semantics=("arbitrary", "arbitrary"),
            vmem_limit_bytes=VMEM_LIMIT),
        name="in_proj",
    )(x, g, w, col_scale)


def _swa_kernel(slopes_ref, sinks_ref, q_ref, k_ref, v_ref, o_ref, bias_ref):
    g = pl.program_id(1)
    n_heads = N_SWA_HEADS // N_SWA_KV
    nb = q_ref.shape[1] // BLOCK
    row = lax.broadcasted_iota(jnp.int32, (BLOCK, BLOCK), 0)
    col = lax.broadcasted_iota(jnp.int32, (BLOCK, BLOCK), 1)
    low_half = col < HEAD_DIM
    dist_p = (row + BLOCK - col).astype(jnp.float32)
    dist_c = (row - col).astype(jnp.float32)
    for jh in range(n_heads):
        slope2 = slopes_ref[g * n_heads + jh] * LOG2E
        bias_ref[jh, :, :BLOCK] = jnp.where(col > row, -slope2 * dist_p, -jnp.inf)
        bias_ref[jh, :, BLOCK:] = jnp.where(col <= row, -slope2 * dist_c, -jnp.inf)
    ones = jnp.ones((BLOCK, LANES), jnp.bfloat16)
    prev_cols = lax.broadcasted_iota(jnp.int32, (BLOCK, 2 * BLOCK), 1) < BLOCK

    def group(gi, _):
        for t in range(SWA_QB):
            qb = gi * SWA_QB + t
            q0 = pl.multiple_of(qb * BLOCK, BLOCK)
            p0 = pl.multiple_of(jnp.maximum(qb - 1, 0) * BLOCK, BLOCK)
            kk = jnp.concatenate([k_ref[0, pl.ds(p0, BLOCK), :],
                                  k_ref[0, pl.ds(q0, BLOCK), :]], axis=0)
            vv = jnp.concatenate(
                [jnp.concatenate([v_ref[0, pl.ds(p0, BLOCK), :], ones], axis=1),
                 jnp.concatenate([v_ref[0, pl.ds(q0, BLOCK), :], ones], axis=1)], axis=0)
            q = q_ref[0, pl.ds(q0, BLOCK), :]
            outs = []
            for jh in range(n_heads):
                sink2 = sinks_ref[g * n_heads + jh] * LOG2E
                qpair = q[:, (jh // 2) * LANES:(jh // 2 + 1) * LANES]
                keep = low_half if jh % 2 == 0 else jnp.logical_not(low_half)
                qh = jnp.where(keep, qpair, jnp.zeros_like(qpair))
                bias = bias_ref[jh]
                if t == 0:
                    bias = jnp.where(jnp.logical_and(prev_cols, qb == 0), -jnp.inf, bias)
                s = lax.dot_general(qh, kk, _NT, preferred_element_type=jnp.float32) + bias
                m = jnp.maximum(jnp.max(s, axis=-1, keepdims=True), sink2)
                e = jnp.exp2(s - m).astype(jnp.bfloat16)
                pv = jnp.dot(e, vv, preferred_element_type=jnp.float32)
                outs.append(pv[:, :LANES] / (pv[:, LANES:] + jnp.exp2(sink2 - m)))
            o_ref[0, pl.ds(q0, BLOCK), :] = jnp.concatenate(
                [jnp.where(low_half, outs[0], outs[1]), jnp.where(low_half, outs[2], outs[3])],
                axis=1)
        return 0

    lax.fori_loop(0, nb // SWA_QB, group, 0)


def _swa(proj, slopes, sinks):
    b, s, _ = proj.shape
    kblk, vblk = COL_KA // LANES, COL_VA // LANES
    smem = pl.BlockSpec(memory_space=pltpu.SMEM)
    return pl.pallas_call(
        _swa_kernel,
        grid=(b, N_SWA_KV),
        in_specs=[
            smem, smem,
            pl.BlockSpec((1, s, 2 * LANES), lambda i, g: (i, 0, g)),
            pl.BlockSpec((1, s, LANES), lambda i, g: (i, 0, kblk + g)),
            pl.BlockSpec((1, s, LANES), lambda i, g: (i, 0, vblk + g)),
        ],
        out_specs=pl.BlockSpec((1, s, 2 * LANES), lambda i, g: (i, 0, g)),
        out_shape=jax.ShapeDtypeStruct((b, s, SWA_Q), jnp.float32),
        scratch_shapes=[pltpu.VMEM((N_SWA_HEADS // N_SWA_KV, BLOCK, 2 * BLOCK), jnp.float32)],
        compiler_params=pltpu.CompilerParams(
            dimension_semantics=("arbitrary", "arbitrary")),
        name="swa",
    )(slopes, sinks, proj, proj, proj)


def _sb_kernel(q_ref, k_ref, v_ref, o_ref, carry_ref, acc_ref):
    s_len = q_ref.shape[1]
    nq = s_len // SB_TQ
    n_sub = SB_TQ // SB_TK
    r = lax.broadcasted_iota(jnp.int32, (2 * SB_TK, 2 * SB_TK), 0)
    c = lax.broadcasted_iota(jnp.int32, (2 * SB_TK, 2 * SB_TK), 1)
    tri = jnp.where(jnp.logical_and(r > c, (r < SB_TK) == (c < SB_TK)), 1.0, 0.0
                    ).astype(jnp.bfloat16)
    low_half = lax.broadcasted_iota(jnp.int32, (SB_TK, LANES), 1) < HEAD_DIM
    strict_lower = (lax.broadcasted_iota(jnp.int32, (SB_TK, SB_TK), 1)
                    < lax.broadcasted_iota(jnp.int32, (SB_TK, SB_TK), 0))

    def q_block(qi, _):
        q0 = pl.multiple_of(qi * SB_TQ, SB_TQ)
        carry_ref[...] = jnp.zeros_like(carry_ref)
        acc_ref[...] = jnp.zeros_like(acc_ref)

        def k_block(k0, r0, diag):
            kblk = k_ref[0, pl.ds(k0, SB_TK), :]
            vblk = v_ref[0, pl.ds(k0, SB_TK), :]
            zero = jnp.zeros_like(kblk)
            kk = jnp.concatenate([jnp.where(low_half, kblk, zero),
                                  jnp.where(low_half, zero, kblk)], axis=0)
            vv = jnp.concatenate([jnp.where(low_half, vblk, zero),
                                  jnp.where(low_half, zero, vblk)], axis=0)
            def mask_top(x):
                top = jnp.where(strict_lower, x[:SB_TK], 0.0)
                return top if r0 == SB_TQ - SB_TK else jnp.concatenate([top, x[SB_TK:]], axis=0)

            q = q_ref[0, pl.ds(q0 + r0, SB_TQ - r0), :]
            u2 = lax.dot_general(q, kk, _NT, preferred_element_type=jnp.float32)
            sps = []
            for hd in range(2):
                u = u2[:, hd * SB_TK:(hd + 1) * SB_TK]
                neg_abs = lax.bitcast_convert_type(
                    lax.bitcast_convert_type(u, jnp.int32) | jnp.int32(-2 ** 31), jnp.float32)
                sp = jnp.maximum(u, 0.0) + jnp.log(1.0 + jnp.exp2(neg_abs)) * LOG2E
                sps.append(mask_top(sp) if diag else sp)
            later = jnp.dot(jnp.concatenate([sp.astype(jnp.bfloat16) for sp in sps], axis=1), tri,
                            preferred_element_type=jnp.float32)
            a_heads = []
            for hd in range(2):
                u = u2[:, hd * SB_TK:(hd + 1) * SB_TK]
                s_incl = later[:, hd * SB_TK:(hd + 1) * SB_TK] + sps[hd]
                total = carry_ref[hd, r0:, :] + s_incl
                a = jnp.exp2(u - total)
                if diag:
                    a = mask_top(a)
                carry_ref[hd, r0:, :] = jnp.broadcast_to(total[:, 0:1], total.shape)
                a_heads.append(a.astype(jnp.bfloat16))
            acc_ref[r0:, :] += jnp.dot(jnp.concatenate(a_heads, axis=1), vv,
                                       preferred_element_type=jnp.float32)

        for d in range(n_sub - 1, -1, -1):
            k_block(pl.multiple_of(q0 + d * SB_TK, SB_TK), d * SB_TK, True)

        def chunk(i, _):
            base = q0 - (i + 1) * SB_CK
            for d in range(SB_CK // SB_TK - 1, -1, -1):
                k_block(pl.multiple_of(base + d * SB_TK, SB_TK), 0, False)
            return 0

        lax.fori_loop(0, qi * (SB_TQ // SB_CK), chunk, 0)
        o_ref[0, pl.ds(q0, SB_TQ), :] = acc_ref[...]
        return 0

    lax.fori_loop(0, nq, q_block, 0)


def _sb(proj):
    b, s, _ = proj.shape
    npair = N_SB_HEADS // 2
    qblk, kblk, vblk = COL_QB // LANES, COL_KB // LANES, COL_VB // LANES
    return pl.pallas_call(
        _sb_kernel,
        grid=(b, npair),
        in_specs=[
            pl.BlockSpec((1, s, LANES), lambda i, p: (i, 0, qblk + p)),
            pl.BlockSpec((1, s, LANES), lambda i, p: (i, 0, kblk + p)),
            pl.BlockSpec((1, s, LANES), lambda i, p: (i, 0, vblk + p)),
        ],
        out_specs=pl.BlockSpec((1, s, LANES), lambda i, p: (i, 0, p)),
        out_shape=jax.ShapeDtypeStruct((b, s, SB_W), jnp.float32),
        scratch_shapes=[pltpu.VMEM((2, SB_TQ, SB_TK), jnp.float32),
                        pltpu.VMEM((SB_TQ, LANES), jnp.float32)],
        compiler_params=pltpu.CompilerParams(
            dimension_semantics=("arbitrary", "arbitrary")),
        name="sb",
    )(proj, proj, proj)


def _out_proj_kernel(x_ref, ya_ref, yb_ref, ga_ref, gb_ref, w_ref, o_ref):
    y = jnp.concatenate([_rms(ya_ref[...], ga_ref[...]), _rms(yb_ref[...], gb_ref[...])],
                        axis=1).astype(jnp.bfloat16)
    o_ref[...] = x_ref[...] + jnp.dot(y, w_ref[...], preferred_element_type=jnp.float32)


def _out_proj(x, ya, yb, ga, gb, w):
    n = x.shape[0]
    return pl.pallas_call(
        _out_proj_kernel,
        grid=(n // OUT_TM,),
        in_specs=[
            pl.BlockSpec((OUT_TM, D_MODEL), lambda i: (i, 0)),
            pl.BlockSpec((OUT_TM, SWA_Q), lambda i: (i, 0)),
            pl.BlockSpec((OUT_TM, SB_W), lambda i: (i, 0)),
            pl.BlockSpec((1, SWA_Q), lambda i: (0, 0)),
            pl.BlockSpec((1, SB_W), lambda i: (0, 0)),
            pl.BlockSpec((SWA_Q + SB_W, D_MODEL), lambda i: (0, 0)),
        ],
        out_specs=pl.BlockSpec((OUT_TM, D_MODEL), lambda i: (i, 0)),
        out_shape=jax.ShapeDtypeStruct((n, D_MODEL), jnp.float32),
        compiler_params=pltpu.CompilerParams(
            dimension_semantics=("arbitrary",),
            vmem_limit_bytes=VMEM_LIMIT),
        name="out_proj",
    )(x, ya, yb, ga, gb, w)


def _proj_col_scale():
    qb0 = SWA_Q + 2 * N_SWA_KV * HEAD_DIM
    s = jnp.ones((IN_W,), jnp.float32)
    s = s.at[:SWA_Q].set(HEAD_DIM ** -0.5 * LOG2E)
    s = s.at[qb0:qb0 + SB_W].set(HEAD_DIM ** -0.5 * LOG2E)
    return s.reshape(1, IN_W)


def kernel(x, ffn1_norm, ffn1_w_gate, ffn1_w_up, ffn1_w_down, mix_norm, w_in, swa_sinks,
           swa_out_norm, sb_out_norm, w_out, ffn2_norm, ffn2_w_gate, ffn2_w_up, ffn2_w_down,
           final_norm):
    b, s, d = x.shape
    depth = ffn1_norm.shape[0]
    xf = x.reshape(b * s, d)
    i = jnp.arange(1, N_SWA_HEADS + 1, dtype=jnp.float32)
    slopes = jnp.exp2(-8.0 * i / N_SWA_HEADS)
    fg = final_norm.reshape(1, d)
    col_scale = _proj_col_scale()
    for l in range(depth):
        xf = _ffn(xf, ffn1_norm[l].reshape(1, d), ffn1_w_gate[l], ffn1_w_up[l], ffn1_w_down[l],
                  fg, final_norm=False)

        proj = _in_proj(xf, mix_norm[l].reshape(1, d), w_in[l].astype(jnp.bfloat16), col_scale)
        proj = proj.reshape(b, s, PROJ_W)
        ya = _swa(proj, slopes, swa_sinks[l])
        yb = _sb(proj)
        xf = _out_proj(xf, ya.reshape(b * s, SWA_Q), yb.reshape(b * s, SB_W),
                       swa_out_norm[l].reshape(1, SWA_Q), sb_out_norm[l].reshape(1, SB_W),
                       w_out[l].astype(jnp.bfloat16))

        xf = _ffn(xf, ffn2_norm[l].reshape(1, d), ffn2_w_gate[l], ffn2_w_up[l], ffn2_w_down[l],
                  fg, final_norm=(l == depth - 1))
    return xf.reshape(b, s, d)
```

```python
import functools
import math

import jax
import jax.numpy as jnp
from jax import lax
from jax.experimental import pallas as pl
from jax.experimental.pallas import tpu as pltpu

D_MODEL = 2048
HEAD_DIM = 64
N_SWA_HEADS = 16
N_SWA_KV = 4
N_SB_HEADS = 16
WINDOW = 128
BLOCK = 128
D_FF = 5504
EPS = 1e-6
LOG2E = math.log2(math.e)

LANES = 128
FFN_TM = 1024
FFN_TF = 512
FFN_HEAD_TF = 256
PROJ_TM = 1024
PROJ_TN = 1536
OUT_TM = 512
SWA_QB = 16
SB_TK = 128
SB_TR = 2048
VMEM_LIMIT = 56 * 1024 * 1024

SWA_Q = N_SWA_HEADS * HEAD_DIM
SB_W = N_SB_HEADS * HEAD_DIM
COL_KA = SWA_Q
COL_VA = COL_KA + 2 * N_SWA_KV * HEAD_DIM
COL_QB = COL_VA + 2 * N_SWA_KV * HEAD_DIM
COL_KB = COL_QB + SB_W
COL_VB = COL_KB + SB_W
PROJ_W = COL_VB + SB_W
IN_W = SWA_Q + 2 * N_SWA_KV * HEAD_DIM + 3 * SB_W
assert PROJ_TN == SWA_Q + 2 * N_SWA_KV * HEAD_DIM and IN_W % PROJ_TN == 0

_NT = (((1,), (1,)), ((), ()))


def _rms(x, g):
    return x * lax.rsqrt(jnp.mean(x * x, axis=-1, keepdims=True) + EPS) * g


def _ffn_kernel(*refs, tf, final_norm, cast_weights):
    if cast_weights:
        (x_ref, g_ref, wg_ref, wu_ref, wd_ref, fg_ref,
         o_ref, wg_out, wu_out, wd_out, h_ref) = refs
    else:
        x_ref, g_ref, wg_ref, wu_ref, wd_ref, fg_ref, _, o_ref, h_ref = refs
    j = pl.program_id(1)
    last = pl.cdiv(D_FF, tf) - 1
    tail = D_FF - last * tf

    def half_swiglu(h, width):
        wg, wu, wd = wg_ref[:, :width], wu_ref[:, :width], wd_ref[:width, :]
        if cast_weights:
            wg, wu, wd = (w.astype(jnp.bfloat16) for w in (wg, wu, wd))
            wg_out[:, :width] = wg
            wu_out[:, :width] = wu
            wd_out[:width, :] = wd
        gate = jnp.dot(h, wg, preferred_element_type=jnp.float32)
        up = jnp.dot(h, wu, preferred_element_type=jnp.float32)
        act = (gate * jax.nn.sigmoid(gate) * up * 0.5).astype(jnp.bfloat16)
        return jnp.dot(act, wd, preferred_element_type=jnp.float32)

    @pl.when(j == 0)
    def _():
        x = x_ref[...]
        h = _rms(x, g_ref[...]).astype(jnp.bfloat16)
        h_ref[...] = h
        o_ref[...] = x + half_swiglu(h, tf)

    @pl.when(jnp.logical_and(j > 0, j < last))
    def _():
        o_ref[...] += half_swiglu(h_ref[...], tf)

    @pl.when(j == last)
    def _():
        y = o_ref[...] + half_swiglu(h_ref[...], tail)
        o_ref[...] = _rms(y, fg_ref[...]) if final_norm else y


def _ffn(x, g, w_gate, w_up, w_down, fg, final_norm):
    n = x.shape[0]
    suffix = "_final" if final_norm else ""
    params = pltpu.CompilerParams(dimension_semantics=("arbitrary", "arbitrary"),
                                  vmem_limit_bytes=VMEM_LIMIT)
    vec = pl.BlockSpec((1, D_MODEL), lambda i, j: (0, 0))

    def weight_specs(tf):
        return [pl.BlockSpec((D_MODEL, tf), lambda i, j: (0, j)),
                pl.BlockSpec((D_MODEL, tf), lambda i, j: (0, j)),
                pl.BlockSpec((tf, D_MODEL), lambda i, j: (j, 0))]

    out_shape = jax.ShapeDtypeStruct((n, D_MODEL), jnp.float32)
    head_out, wg, wu, wd = pl.pallas_call(
        functools.partial(_ffn_kernel, tf=FFN_HEAD_TF, final_norm=final_norm, cast_weights=True),
        grid=(1, pl.cdiv(D_FF, FFN_HEAD_TF)),
        in_specs=[pl.BlockSpec((FFN_TM, D_MODEL), lambda i, j: (0, 0), pipeline_mode=pl.Buffered(1)),
                  vec, *weight_specs(FFN_HEAD_TF), vec],
        out_specs=[pl.BlockSpec((FFN_TM, D_MODEL), lambda i, j: (0, 0)),
                   *weight_specs(FFN_HEAD_TF)],
        out_shape=[out_shape,
                   jax.ShapeDtypeStruct(w_gate.shape, jnp.bfloat16),
                   jax.ShapeDtypeStruct(w_up.shape, jnp.bfloat16),
                   jax.ShapeDtypeStruct(w_down.shape, jnp.bfloat16)],
        scratch_shapes=[pltpu.VMEM((FFN_TM, D_MODEL), jnp.bfloat16)],
        compiler_params=params,
        name="ffn_head" + suffix,
    )(x, g, w_gate, w_up, w_down, fg)

    return pl.pallas_call(
        functools.partial(_ffn_kernel, tf=FFN_TF, final_norm=final_norm, cast_weights=False),
        grid=(n // FFN_TM - 1, pl.cdiv(D_FF, FFN_TF)),
        in_specs=[pl.BlockSpec((FFN_TM, D_MODEL), lambda i, j: (i + 1, 0)),
                  vec, *weight_specs(FFN_TF), vec,
                  pl.BlockSpec(memory_space=pl.ANY)],
        out_specs=pl.BlockSpec((FFN_TM, D_MODEL), lambda i, j: (i + 1, 0)),
        out_shape=out_shape,
        input_output_aliases={6: 0},
        scratch_shapes=[pltpu.VMEM((FFN_TM, D_MODEL), jnp.bfloat16)],
        compiler_params=params,
        name="ffn" + suffix,
    )(x, g, wg, wu, wd, fg, head_out)


def _in_proj_kernel(x_ref, g_ref, w_ref, s_ref, o_ref, h_ref):
    def project(h):
        acc = jnp.dot(h, w_ref[...], preferred_element_type=jnp.float32)
        return acc * s_ref[...]

    @pl.when(pl.program_id(1) == 0)
    def _():
        h = _rms(x_ref[...], g_ref[...]).astype(jnp.bfloat16)
        h_ref[...] = h
        p = project(h)
        o_ref[:, :SWA_Q] = p[:, :SWA_Q].astype(jnp.bfloat16)
        for g in range(2 * N_SWA_KV):
            head = p[:, SWA_Q + g * HEAD_DIM:SWA_Q + (g + 1) * HEAD_DIM]
            o_ref[:, COL_KA + g * LANES:COL_KA + (g + 1) * LANES] = jnp.concatenate(
                [head, head], axis=1).astype(jnp.bfloat16)

    for jb in range(1, IN_W // PROJ_TN):
        @pl.when(pl.program_id(1) == jb)
        def _(jb=jb):
            start = jb * PROJ_TN + PROJ_W - IN_W
            o_ref[:, start:start + PROJ_TN] = project(h_ref[...]).astype(jnp.bfloat16)


def _in_proj(x, g, w, col_scale):
    n = x.shape[0]
    grid = (n // PROJ_TM, IN_W // PROJ_TN)
    return pl.pallas_call(
        _in_proj_kernel,
        grid=grid,
        in_specs=[
            pl.BlockSpec((PROJ_TM, D_MODEL), lambda i, j: (i, 0)),
            pl.BlockSpec((1, D_MODEL), lambda i, j: (0, 0)),
            pl.BlockSpec((D_MODEL, PROJ_TN), lambda i, j: (0, j)),
            pl.BlockSpec((1, PROJ_TN), lambda i, j: (0, j)),
        ],
        out_specs=pl.BlockSpec((PROJ_TM, PROJ_W), lambda i, j: (i, 0)),
        out_shape=jax.ShapeDtypeStruct((n, PROJ_W), jnp.bfloat16),
        scratch_shapes=[pltpu.VMEM((PROJ_TM, D_MODEL), jnp.bfloat16)],
        compiler_params=pltpu.CompilerParams(
            dimension_semantics=("arbitrary", "arbitrary"),
            vmem_limit_bytes=VMEM_LIMIT),
        name="in_proj",
    )(x, g, w, col_scale)


def _swa_kernel(slopes_ref, sinks_ref, q_ref, k_ref, v_ref, o_ref, bias_ref):
    g = pl.program_id(1)
    n_heads = N_SWA_HEADS // N_SWA_KV
    nb = q_ref.shape[1] // BLOCK
    row = lax.broadcasted_iota(jnp.int32, (BLOCK, BLOCK), 0)
    col = lax.broadcasted_iota(jnp.int32, (BLOCK, BLOCK), 1)
    low_half = col < HEAD_DIM
    dist_p = (row + BLOCK - col).astype(jnp.float32)
    dist_c = (row - col).astype(jnp.float32)
    for jh in range(n_heads):
        slope2 = slopes_ref[g * n_heads + jh] * LOG2E
        bias_ref[jh, :, :BLOCK] = jnp.where(col > row, -slope2 * dist_p, -jnp.inf)
        bias_ref[jh, :, BLOCK:] = jnp.where(col <= row, -slope2 * dist_c, -jnp.inf)
    ones = jnp.ones((BLOCK, LANES), jnp.bfloat16)
    prev_cols = lax.broadcasted_iota(jnp.int32, (BLOCK, 2 * BLOCK), 1) < BLOCK

    def group(gi, _):
        for t in range(SWA_QB):
            qb = gi * SWA_QB + t
            q0 = pl.multiple_of(qb * BLOCK, BLOCK)
            p0 = pl.multiple_of(jnp.maximum(qb - 1, 0) * BLOCK, BLOCK)
            kk = jnp.concatenate([k_ref[0, pl.ds(p0, BLOCK), :],
                                  k_ref[0, pl.ds(q0, BLOCK), :]], axis=0)
            vv = jnp.concatenate(
                [jnp.concatenate([v_ref[0, pl.ds(p0, BLOCK), :], ones], axis=1),
                 jnp.concatenate([v_ref[0, pl.ds(q0, BLOCK), :], ones], axis=1)], axis=0)
            q = q_ref[0, pl.ds(q0, BLOCK), :]
            outs = []
            for jh in range(n_heads):
                sink2 = sinks_ref[g * n_heads + jh] * LOG2E
                qpair = q[:, (jh // 2) * LANES:(jh // 2 + 1) * LANES]
                keep = low_half if jh % 2 == 0 else jnp.logical_not(low_half)
                qh = jnp.where(keep, qpair, jnp.zeros_like(qpair))
                bias = bias_ref[jh]
                if t == 0:
                    bias = jnp.where(jnp.logical_and(prev_cols, qb == 0), -jnp.inf, bias)
                s = lax.dot_general(qh, kk, _NT, preferred_element_type=jnp.float32) + bias
                m = jnp.maximum(jnp.max(s, axis=-1, keepdims=True), sink2)
                e = jnp.exp2(s - m).astype(jnp.bfloat16)
                pv = jnp.dot(e, vv, preferred_element_type=jnp.float32)
                outs.append(pv[:, :LANES] / (pv[:, LANES:] + jnp.exp2(sink2 - m)))
            o_ref[0, pl.ds(q0, BLOCK), :] = jnp.concatenate(
                [jnp.where(low_half, outs[0], outs[1]), jnp.where(low_half, outs[2], outs[3])],
                axis=1)
        return 0

    lax.fori_loop(0, nb // SWA_QB, group, 0)


def _swa(proj, slopes, sinks):
    b, s, _ = proj.shape
    kblk, vblk = COL_KA // LANES, COL_VA // LANES
    smem = pl.BlockSpec(memory_space=pltpu.SMEM)
    return pl.pallas_call(
        _swa_kernel,
        grid=(b, N_SWA_KV),
        in_specs=[
            smem, smem,
            pl.BlockSpec((1, s, 2 * LANES), lambda i, g: (i, 0, g)),
            pl.BlockSpec((1, s, LANES), lambda i, g: (i, 0, kblk + g)),
            pl.BlockSpec((1, s, LANES), lambda i, g: (i, 0, vblk + g)),
        ],
        out_specs=pl.BlockSpec((1, s, 2 * LANES), lambda i, g: (i, 0, g)),
        out_shape=jax.ShapeDtypeStruct((b, s, SWA_Q), jnp.float32),
        scratch_shapes=[pltpu.VMEM((N_SWA_HEADS // N_SWA_KV, BLOCK, 2 * BLOCK), jnp.float32)],
        compiler_params=pltpu.CompilerParams(
            dimension_semantics=("arbitrary", "arbitrary")),
        name="swa",
    )(slopes, sinks, proj, proj, proj)


def _sb_kernel(q_ref, k_ref, v_ref, o_ref, carry_ref, acc_ref):
    s_len = q_ref.shape[1]
    r = lax.broadcasted_iota(jnp.int32, (2 * SB_TK, 2 * SB_TK), 0)
    c = lax.broadcasted_iota(jnp.int32, (2 * SB_TK, 2 * SB_TK), 1)
    tri = jnp.where(jnp.logical_and(r > c, (r < SB_TK) == (c < SB_TK)), 1.0, 0.0
                    ).astype(jnp.bfloat16)
    low_half = lax.broadcasted_iota(jnp.int32, (SB_TK, LANES), 1) < HEAD_DIM
    strict_lower = (lax.broadcasted_iota(jnp.int32, (SB_TK, SB_TK), 1)
                    < lax.broadcasted_iota(jnp.int32, (SB_TK, SB_TK), 0))

    carry_ref[...] = jnp.zeros_like(carry_ref)
    acc_ref[...] = jnp.zeros_like(acc_ref)

    def rows(kk, vv, r0, r1, diag):
        def mask_top(x):
            top = jnp.where(strict_lower, x[:SB_TK], 0.0)
            return top if r1 - r0 == SB_TK else jnp.concatenate([top, x[SB_TK:]], axis=0)

        u2 = lax.dot_general(q_ref[0, r0:r1, :], kk, _NT,
                             preferred_element_type=jnp.float32)
        sps = []
        for hd in range(2):
            u = u2[:, hd * SB_TK:(hd + 1) * SB_TK]
            sp = jnp.maximum(u, 0.0) + jnp.log(1.0 + jnp.exp2(-jnp.abs(u))) * LOG2E
            sps.append(mask_top(sp) if diag else sp)
        later = jnp.dot(jnp.concatenate([sp.astype(jnp.bfloat16) for sp in sps], axis=1), tri,
                        preferred_element_type=jnp.float32)
        a_heads = []
        for hd in range(2):
            u = u2[:, hd * SB_TK:(hd + 1) * SB_TK]
            s_incl = later[:, hd * SB_TK:(hd + 1) * SB_TK] + sps[hd]
            total = carry_ref[hd, r0:r1, :] + s_incl
            a = jnp.exp2(u - total)
            if diag:
                a = mask_top(a)
            carry_ref[hd, r0:r1, :] = jnp.broadcast_to(total[:, 0:1], total.shape)
            a_heads.append(a.astype(jnp.bfloat16))
        acc_ref[r0:r1, :] += jnp.dot(jnp.concatenate(a_heads, axis=1), vv,
                                     preferred_element_type=jnp.float32)

    for d in range(s_len // SB_TK - 1, -1, -1):
        k0 = d * SB_TK
        kblk = k_ref[0, k0:k0 + SB_TK, :]
        vblk = v_ref[0, k0:k0 + SB_TK, :]
        zero = jnp.zeros_like(kblk)
        kk = jnp.concatenate([jnp.where(low_half, kblk, zero),
                              jnp.where(low_half, zero, kblk)], axis=0)
        vv = jnp.concatenate([jnp.where(low_half, vblk, zero),
                              jnp.where(low_half, zero, vblk)], axis=0)
        for r0 in range(k0, s_len, SB_TR):
            rows(kk, vv, r0, min(r0 + SB_TR, s_len), diag=(r0 == k0))
    o_ref[0] = acc_ref[...]


def _sb(proj):
    b, s, _ = proj.shape
    npair = N_SB_HEADS // 2
    qblk, kblk, vblk = COL_QB // LANES, COL_KB // LANES, COL_VB // LANES
    return pl.pallas_call(
        _sb_kernel,
        grid=(b, npair),
        in_specs=[
            pl.BlockSpec((1, s, LANES), lambda i, p: (i, 0, qblk + p)),
            pl.BlockSpec((1, s, LANES), lambda i, p: (i, 0, kblk + p)),
            pl.BlockSpec((1, s, LANES), lambda i, p: (i, 0, vblk + p)),
        ],
        out_specs=pl.BlockSpec((1, s, LANES), lambda i, p: (i, 0, p)),
        out_shape=jax.ShapeDtypeStruct((b, s, SB_W), jnp.float32),
        scratch_shapes=[pltpu.VMEM((2, s, SB_TK), jnp.float32),
                        pltpu.VMEM((s, LANES), jnp.float32)],
        compiler_params=pltpu.CompilerParams(
            dimension_semantics=("arbitrary", "arbitrary")),
        name="sb",
    )(proj, proj, proj)


def _out_proj_kernel(x_ref, ya_ref, yb_ref, ga_ref, gb_ref, w_ref, o_ref):
    y = jnp.concatenate([_rms(ya_ref[...], ga_ref[...]), _rms(yb_ref[...], gb_ref[...])],
                        axis=1).astype(jnp.bfloat16)
    o_ref[...] = x_ref[...] + jnp.dot(y, w_ref[...], preferred_element_type=jnp.float32)


def _out_proj(x, ya, yb, ga, gb, w):
    n = x.shape[0]
    return pl.pallas_call(
        _out_proj_kernel,
        grid=(n // OUT_TM,),
        in_specs=[
            pl.BlockSpec((OUT_TM, D_MODEL), lambda i: (i, 0)),
            pl.BlockSpec((OUT_TM, SWA_Q), lambda i: (i, 0)),
            pl.BlockSpec((OUT_TM, SB_W), lambda i: (i, 0)),
            pl.BlockSpec((1, SWA_Q), lambda i: (0, 0)),
            pl.BlockSpec((1, SB_W), lambda i: (0, 0)),
            pl.BlockSpec((SWA_Q + SB_W, D_MODEL), lambda i: (0, 0)),
        ],
        out_specs=pl.BlockSpec((OUT_TM, D_MODEL), lambda i: (i, 0)),
        out_shape=jax.ShapeDtypeStruct((n, D_MODEL), jnp.float32),
        compiler_params=pltpu.CompilerParams(
            dimension_semantics=("arbitrary",),
            vmem_limit_bytes=VMEM_LIMIT),
        name="out_proj",
    )(x, ya, yb, ga, gb, w)


def _proj_col_scale():
    qb0 = SWA_Q + 2 * N_SWA_KV * HEAD_DIM
    s = jnp.ones((IN_W,), jnp.float32)
    s = s.at[:SWA_Q].set(HEAD_DIM ** -0.5 * LOG2E)
    s = s.at[qb0:qb0 + SB_W].set(HEAD_DIM ** -0.5 * LOG2E)
    return s.reshape(1, IN_W)


def kernel(x, ffn1_norm, ffn1_w_gate, ffn1_w_up, ffn1_w_down, mix_norm, w_in, swa_sinks,
           swa_out_norm, sb_out_norm, w_out, ffn2_norm, ffn2_w_gate, ffn2_w_up, ffn2_w_down,
           final_norm):
    b, s, d = x.shape
    depth = ffn1_norm.shape[0]
    xf = x.reshape(b * s, d)
    i = jnp.arange(1, N_SWA_HEADS + 1, dtype=jnp.float32)
    slopes = jnp.exp2(-8.0 * i / N_SWA_HEADS)
    fg = final_norm.reshape(1, d)
    col_scale = _proj_col_scale()
    for l in range(depth):
        xf = _ffn(xf, ffn1_norm[l].reshape(1, d), ffn1_w_gate[l], ffn1_w_up[l], ffn1_w_down[l],
                  fg, final_norm=False)

        proj = _in_proj(xf, mix_norm[l].reshape(1, d), w_in[l].astype(jnp.bfloat16), col_scale)
        proj = proj.reshape(b, s, PROJ_W)
        ya = _swa(proj, slopes, swa_sinks[l])
        yb = _sb(proj)
        xf = _out_proj(xf, ya.reshape(b * s, SWA_Q), yb.reshape(b * s, SB_W),
                       swa_out_norm[l].reshape(1, SWA_Q), sb_out_norm[l].reshape(1, SB_W),
                       w_out[l].astype(jnp.bfloat16))

        xf = _ffn(xf, ffn2_norm[l].reshape(1, d), ffn2_w_gate[l], ffn2_w_up[l], ffn2_w_down[l],
                  fg, final_norm=(l == depth - 1))
    return xf.reshape(b, s, d)
```

```python
import functools
import math

import jax
import jax.numpy as jnp
from jax import lax
from jax.experimental import pallas as pl
from jax.experimental.pallas import tpu as pltpu

D_MODEL = 2048
HEAD_DIM = 64
N_SWA_HEADS = 16
N_SWA_KV = 4
N_SB_HEADS = 16
WINDOW = 128
BLOCK = 128
D_FF = 5504
EPS = 1e-6
LOG2E = math.log2(math.e)

LANES = 128
FFN_TM = 1024
FFN_TF = 512
FFN_HEAD_TF = 256
PROJ_TM = 1024
PROJ_TN = 1536
OUT_TM = 512
SB_TK = 128
VMEM_LIMIT = 56 * 1024 * 1024

SWA_Q = N_SWA_HEADS * HEAD_DIM
SB_W = N_SB_HEADS * HEAD_DIM
COL_KA = SWA_Q
COL_VA = COL_KA + 2 * N_SWA_KV * HEAD_DIM
COL_QB = COL_VA + 2 * N_SWA_KV * HEAD_DIM
COL_KB = COL_QB + SB_W
COL_VB = COL_KB + SB_W
PROJ_W = COL_VB + SB_W
IN_W = SWA_Q + 2 * N_SWA_KV * HEAD_DIM + 3 * SB_W
assert PROJ_TN == SWA_Q + 2 * N_SWA_KV * HEAD_DIM and IN_W % PROJ_TN == 0

_NT = (((1,), (1,)), ((), ()))


def _rms(x, g):
    return x * lax.rsqrt(jnp.mean(x * x, axis=-1, keepdims=True) + EPS) * g


def _ffn_kernel(*refs, tf, final_norm, cast_weights):
    if cast_weights:
        (x_ref, g_ref, wg_ref, wu_ref, wd_ref, fg_ref,
         o_ref, wg_out, wu_out, wd_out, h_ref) = refs
    else:
        x_ref, g_ref, wg_ref, wu_ref, wd_ref, fg_ref, _, o_ref, h_ref = refs
    j = pl.program_id(1)
    last = pl.cdiv(D_FF, tf) - 1
    tail = D_FF - last * tf

    def half_swiglu(h, width):
        wg, wu, wd = wg_ref[:, :width], wu_ref[:, :width], wd_ref[:width, :]
        if cast_weights:
            wg, wu, wd = (w.astype(jnp.bfloat16) for w in (wg, wu, wd))
            wg_out[:, :width] = wg
            wu_out[:, :width] = wu
            wd_out[:width, :] = wd
        gate = jnp.dot(h, wg, preferred_element_type=jnp.float32)
        up = jnp.dot(h, wu, preferred_element_type=jnp.float32)
        act = (gate * jax.nn.sigmoid(gate) * up * 0.5).astype(jnp.bfloat16)
        return jnp.dot(act, wd, preferred_element_type=jnp.float32)

    @pl.when(j == 0)
    def _():
        x = x_ref[...]
        h = _rms(x, g_ref[...]).astype(jnp.bfloat16)
        h_ref[...] = h
        o_ref[...] = x + half_swiglu(h, tf)

    @pl.when(jnp.logical_and(j > 0, j < last))
    def _():
        o_ref[...] += half_swiglu(h_ref[...], tf)

    @pl.when(j == last)
    def _():
        y = o_ref[...] + half_swiglu(h_ref[...], tail)
        o_ref[...] = _rms(y, fg_ref[...]) if final_norm else y


def _ffn(x, g, w_gate, w_up, w_down, fg, final_norm):
    n = x.shape[0]
    suffix = "_final" if final_norm else ""
    params = pltpu.CompilerParams(dimension_semantics=("arbitrary", "arbitrary"),
                                  vmem_limit_bytes=VMEM_LIMIT)
    vec = pl.BlockSpec((1, D_MODEL), lambda i, j: (0, 0))

    def weight_specs(tf):
        return [pl.BlockSpec((D_MODEL, tf), lambda i, j: (0, j)),
                pl.BlockSpec((D_MODEL, tf), lambda i, j: (0, j)),
                pl.BlockSpec((tf, D_MODEL), lambda i, j: (j, 0))]

    out_shape = jax.ShapeDtypeStruct((n, D_MODEL), jnp.float32)
    head_out, wg, wu, wd = pl.pallas_call(
        functools.partial(_ffn_kernel, tf=FFN_HEAD_TF, final_norm=final_norm, cast_weights=True),
        grid=(1, pl.cdiv(D_FF, FFN_HEAD_TF)),
        in_specs=[pl.BlockSpec((FFN_TM, D_MODEL), lambda i, j: (0, 0), pipeline_mode=pl.Buffered(1)),
                  vec, *weight_specs(FFN_HEAD_TF), vec],
        out_specs=[pl.BlockSpec((FFN_TM, D_MODEL), lambda i, j: (0, 0)),
                   *weight_specs(FFN_HEAD_TF)],
        out_shape=[out_shape,
                   jax.ShapeDtypeStruct(w_gate.shape, jnp.bfloat16),
                   jax.ShapeDtypeStruct(w_up.shape, jnp.bfloat16),
                   jax.ShapeDtypeStruct(w_down.shape, jnp.bfloat16)],
        scratch_shapes=[pltpu.VMEM((FFN_TM, D_MODEL), jnp.bfloat16)],
        compiler_params=params,
        name="ffn_head" + suffix,
    )(x, g, w_gate, w_up, w_down, fg)

    return pl.pallas_call(
        functools.partial(_ffn_kernel, tf=FFN_TF, final_norm=final_norm, cast_weights=False),
        grid=(n // FFN_TM - 1, pl.cdiv(D_FF, FFN_TF)),
        in_specs=[pl.BlockSpec((FFN_TM, D_MODEL), lambda i, j: (i + 1, 0)),
                  vec, *weight_specs(FFN_TF), vec,
                  pl.BlockSpec(memory_space=pl.ANY)],
        out_specs=pl.BlockSpec((FFN_TM, D_MODEL), lambda i, j: (i + 1, 0)),
        out_shape=out_shape,
        input_output_aliases={6: 0},
        scratch_shapes=[pltpu.VMEM((FFN_TM, D_MODEL), jnp.bfloat16)],
        compiler_params=params,
        name="ffn" + suffix,
    )(x, g, wg, wu, wd, fg, head_out)


def _in_proj_kernel(x_ref, g_ref, w_ref, s_ref, o_ref, h_ref):
    def project(h):
        acc = jnp.dot(h, w_ref[...], preferred_element_type=jnp.float32)
        return acc * s_ref[...]

    @pl.when(pl.program_id(1) == 0)
    def _():
        h = _rms(x_ref[...], g_ref[...]).astype(jnp.bfloat16)
        h_ref[...] = h
        p = project(h)
        o_ref[:, :SWA_Q] = p[:, :SWA_Q].astype(jnp.bfloat16)
        for g in range(2 * N_SWA_KV):
            head = p[:, SWA_Q + g * HEAD_DIM:SWA_Q + (g + 1) * HEAD_DIM]
            o_ref[:, COL_KA + g * LANES:COL_KA + (g + 1) * LANES] = jnp.concatenate(
                [head, head], axis=1).astype(jnp.bfloat16)

    for jb in range(1, IN_W // PROJ_TN):
        @pl.when(pl.program_id(1) == jb)
        def _(jb=jb):
            start = jb * PROJ_TN + PROJ_W - IN_W
            o_ref[:, start:start + PROJ_TN] = project(h_ref[...]).astype(jnp.bfloat16)


def _in_proj(x, g, w, col_scale):
    n = x.shape[0]
    grid = (n // PROJ_TM, IN_W // PROJ_TN)
    return pl.pallas_call(
        _in_proj_kernel,
        grid=grid,
        in_specs=[
            pl.BlockSpec((PROJ_TM, D_MODEL), lambda i, j: (i, 0)),
            pl.BlockSpec((1, D_MODEL), lambda i, j: (0, 0)),
            pl.BlockSpec((D_MODEL, PROJ_TN), lambda i, j: (0, j)),
            pl.BlockSpec((1, PROJ_TN), lambda i, j: (0, j)),
        ],
        out_specs=pl.BlockSpec((PROJ_TM, PROJ_W), lambda i, j: (i, 0)),
        out_shape=jax.ShapeDtypeStruct((n, PROJ_W), jnp.bfloat16),
        scratch_shapes=[pltpu.VMEM((PROJ_TM, D_MODEL), jnp.bfloat16)],
        compiler_params=pltpu.CompilerParams(
            dimension_semantics=("arbitrary", "arbitrary"),
            vmem_limit_bytes=VMEM_LIMIT),
        name="in_proj",
    )(x, g, w, col_scale)


def _swa_kernel(slopes_ref, sinks_ref, q_ref, k_ref, v_ref, o_ref, bias_ref):
    g = pl.program_id(1)
    n_heads = N_SWA_HEADS // N_SWA_KV
    nb = q_ref.shape[1] // BLOCK
    row = lax.broadcasted_iota(jnp.int32, (BLOCK, BLOCK), 0)
    col = lax.broadcasted_iota(jnp.int32, (BLOCK, BLOCK), 1)
    low_half = col < HEAD_DIM
    dist_p = (row + BLOCK - col).astype(jnp.float32)
    dist_c = (row - col).astype(jnp.float32)
    for jh in range(n_heads):
        slope2 = slopes_ref[g * n_heads + jh] * LOG2E
        bias_ref[jh, :, :BLOCK] = jnp.where(col > row, -slope2 * dist_p, -jnp.inf)
        bias_ref[jh, :, BLOCK:] = jnp.where(col <= row, -slope2 * dist_c, -jnp.inf)
    ones = jnp.ones((BLOCK, LANES), jnp.bfloat16)
    prev_cols = lax.broadcasted_iota(jnp.int32, (BLOCK, 2 * BLOCK), 1) < BLOCK

    for qb in range(nb):
        q0 = qb * BLOCK
        p0 = max(qb - 1, 0) * BLOCK
        kk = jnp.concatenate([k_ref[0, p0:p0 + BLOCK, :],
                              k_ref[0, q0:q0 + BLOCK, :]], axis=0)
        vv = jnp.concatenate(
            [jnp.concatenate([v_ref[0, p0:p0 + BLOCK, :], ones], axis=1),
             jnp.concatenate([v_ref[0, q0:q0 + BLOCK, :], ones], axis=1)], axis=0)
        q = q_ref[0, q0:q0 + BLOCK, :]
        outs = []
        for jh in range(n_heads):
            sink2 = sinks_ref[g * n_heads + jh] * LOG2E
            qpair = q[:, (jh // 2) * LANES:(jh // 2 + 1) * LANES]
            keep = low_half if jh % 2 == 0 else jnp.logical_not(low_half)
            qh = jnp.where(keep, qpair, jnp.zeros_like(qpair))
            bias = bias_ref[jh]
            if qb == 0:
                bias = jnp.where(prev_cols, -jnp.inf, bias)
            s = lax.dot_general(qh, kk, _NT, preferred_element_type=jnp.float32) + bias
            m = jnp.maximum(jnp.max(s, axis=-1, keepdims=True), sink2)
            e = jnp.exp2(s - m).astype(jnp.bfloat16)
            pv = jnp.dot(e, vv, preferred_element_type=jnp.float32)
            outs.append(pv[:, :LANES] / (pv[:, LANES:] + jnp.exp2(sink2 - m)))
        o_ref[0, q0:q0 + BLOCK, :] = jnp.concatenate(
            [jnp.where(low_half, outs[0], outs[1]), jnp.where(low_half, outs[2], outs[3])],
            axis=1)


def _swa(proj, slopes, sinks):
    b, s, _ = proj.shape
    kblk, vblk = COL_KA // LANES, COL_VA // LANES
    smem = pl.BlockSpec(memory_space=pltpu.SMEM)
    return pl.pallas_call(
        _swa_kernel,
        grid=(b, N_SWA_KV),
        in_specs=[
            smem, smem,
            pl.BlockSpec((1, s, 2 * LANES), lambda i, g: (i, 0, g)),
            pl.BlockSpec((1, s, LANES), lambda i, g: (i, 0, kblk + g)),
            pl.BlockSpec((1, s, LANES), lambda i, g: (i, 0, vblk + g)),
        ],
        out_specs=pl.BlockSpec((1, s, 2 * LANES), lambda i, g: (i, 0, g)),
        out_shape=jax.ShapeDtypeStruct((b, s, SWA_Q), jnp.float32),
        scratch_shapes=[pltpu.VMEM((N_SWA_HEADS // N_SWA_KV, BLOCK, 2 * BLOCK), jnp.float32)],
        compiler_params=pltpu.CompilerParams(
            dimension_semantics=("arbitrary", "arbitrary")),
        name="swa",
    )(slopes, sinks, proj, proj, proj)


def _sb_kernel(q_ref, k_ref, v_ref, o_ref, carry_ref, acc_ref):
    s_len = q_ref.shape[1]
    r = lax.broadcasted_iota(jnp.int32, (2 * SB_TK, 2 * SB_TK), 0)
    c = lax.broadcasted_iota(jnp.int32, (2 * SB_TK, 2 * SB_TK), 1)
    tri = jnp.where(jnp.logical_and(r > c, (r < SB_TK) == (c < SB_TK)), 1.0, 0.0
                    ).astype(jnp.bfloat16)
    low_half = lax.broadcasted_iota(jnp.int32, (SB_TK, LANES), 1) < HEAD_DIM
    strict_lower = (lax.broadcasted_iota(jnp.int32, (SB_TK, SB_TK), 1)
                    < lax.broadcasted_iota(jnp.int32, (SB_TK, SB_TK), 0))

    carry_ref[...] = jnp.zeros_like(carry_ref)
    acc_ref[...] = jnp.zeros_like(acc_ref)

    def rows(kk, vv, r0, r1):
        def mask_top(x):
            top = jnp.where(strict_lower, x[:SB_TK], 0.0)
            return top if r1 - r0 == SB_TK else jnp.concatenate([top, x[SB_TK:]], axis=0)

        u2 = lax.dot_general(q_ref[0, r0:r1, :], kk, _NT,
                             preferred_element_type=jnp.float32)
        sps = []
        for hd in range(2):
            u = u2[:, hd * SB_TK:(hd + 1) * SB_TK]
            sp = jnp.maximum(u, 0.0) + jnp.log(1.0 + jnp.exp2(-jnp.abs(u))) * LOG2E
            sps.append(mask_top(sp))
        later = jnp.dot(jnp.concatenate([sp.astype(jnp.bfloat16) for sp in sps], axis=1), tri,
                        preferred_element_type=jnp.float32)
        a_heads = []
        for hd in range(2):
            u = u2[:, hd * SB_TK:(hd + 1) * SB_TK]
            s_incl = later[:, hd * SB_TK:(hd + 1) * SB_TK] + sps[hd]
            total = carry_ref[hd, r0:r1, :] + s_incl
            a = mask_top(jnp.exp2(u - total))
            carry_ref[hd, r0:r1, :] = jnp.broadcast_to(total[:, 0:1], total.shape)
            a_heads.append(a.astype(jnp.bfloat16))
        acc_ref[r0:r1, :] += jnp.dot(jnp.concatenate(a_heads, axis=1), vv,
                                     preferred_element_type=jnp.float32)

    for d in range(s_len // SB_TK - 1, -1, -1):
        k0 = d * SB_TK
        kblk = k_ref[0, k0:k0 + SB_TK, :]
        vblk = v_ref[0, k0:k0 + SB_TK, :]
        zero = jnp.zeros_like(kblk)
        kk = jnp.concatenate([jnp.where(low_half, kblk, zero),
                              jnp.where(low_half, zero, kblk)], axis=0)
        vv = jnp.concatenate([jnp.where(low_half, vblk, zero),
                              jnp.where(low_half, zero, vblk)], axis=0)
        rows(kk, vv, k0, s_len)
    o_ref[0] = acc_ref[...]


def _sb(proj):
    b, s, _ = proj.shape
    npair = N_SB_HEADS // 2
    qblk, kblk, vblk = COL_QB // LANES, COL_KB // LANES, COL_VB // LANES
    return pl.pallas_call(
        _sb_kernel,
        grid=(b, npair),
        in_specs=[
            pl.BlockSpec((1, s, LANES), lambda i, p: (i, 0, qblk + p)),
            pl.BlockSpec((1, s, LANES), lambda i, p: (i, 0, kblk + p)),
            pl.BlockSpec((1, s, LANES), lambda i, p: (i, 0, vblk + p)),
        ],
        out_specs=pl.BlockSpec((1, s, LANES), lambda i, p: (i, 0, p)),
        out_shape=jax.ShapeDtypeStruct((b, s, SB_W), jnp.float32),
        scratch_shapes=[pltpu.VMEM((2, s, SB_TK), jnp.float32),
                        pltpu.VMEM((s, LANES), jnp.float32)],
        compiler_params=pltpu.CompilerParams(
            dimension_semantics=("arbitrary", "arbitrary")),
        name="sb",
    )(proj, proj, proj)


def _out_proj_kernel(x_ref, ya_ref, yb_ref, ga_ref, gb_ref, w_ref, o_ref):
    y = jnp.concatenate([_rms(ya_ref[...], ga_ref[...]), _rms(yb_ref[...], gb_ref[...])],
                        axis=1).astype(jnp.bfloat16)
    o_ref[...] = x_ref[...] + jnp.dot(y, w_ref[...], preferred_element_type=jnp.float32)


def _out_proj(x, ya, yb, ga, gb, w):
    n = x.shape[0]
    return pl.pallas_call(
        _out_proj_kernel,
        grid=(n // OUT_TM,),
        in_specs=[
            pl.BlockSpec((OUT_TM, D_MODEL), lambda i: (i, 0)),
            pl.BlockSpec((OUT_TM, SWA_Q), lambda i: (i, 0)),
            pl.BlockSpec((OUT_TM, SB_W), lambda i: (i, 0)),
            pl.BlockSpec((1, SWA_Q), lambda i: (0, 0)),
            pl.BlockSpec((1, SB_W), lambda i: (0, 0)),
            pl.BlockSpec((SWA_Q + SB_W, D_MODEL), lambda i: (0, 0)),
        ],
        out_specs=pl.BlockSpec((OUT_TM, D_MODEL), lambda i: (i, 0)),
        out_shape=jax.ShapeDtypeStruct((n, D_MODEL), jnp.float32),
        compiler_params=pltpu.CompilerParams(
            dimension_semantics=("arbitrary",),
            vmem_limit_bytes=VMEM_LIMIT),
        name="out_proj",
    )(x, ya, yb, ga, gb, w)


def _proj_col_scale():
    qb0 = SWA_Q + 2 * N_SWA_KV * HEAD_DIM
    s = jnp.ones((IN_W,), jnp.float32)
    s = s.at[:SWA_Q].set(HEAD_DIM ** -0.5 * LOG2E)
    s = s.at[qb0:qb0 + SB_W].set(HEAD_DIM ** -0.5 * LOG2E)
    return s.reshape(1, IN_W)


def kernel(x, ffn1_norm, ffn1_w_gate, ffn1_w_up, ffn1_w_down, mix_norm, w_in, swa_sinks,
           swa_out_norm, sb_out_norm, w_out, ffn2_norm, ffn2_w_gate, ffn2_w_up, ffn2_w_down,
           final_norm):
    b, s, d = x.shape
    depth = ffn1_norm.shape[0]
    xf = x.reshape(b * s, d)
    i = jnp.arange(1, N_SWA_HEADS + 1, dtype=jnp.float32)
    slopes = jnp.exp2(-8.0 * i / N_SWA_HEADS)
    fg = final_norm.reshape(1, d)
    col_scale = _proj_col_scale()
    for l in range(depth):
        xf = _ffn(xf, ffn1_norm[l].reshape(1, d), ffn1_w_gate[l], ffn1_w_up[l], ffn1_w_down[l],
                  fg, final_norm=False)

        proj = _in_proj(xf, mix_norm[l].reshape(1, d), w_in[l].astype(jnp.bfloat16), col_scale)
        proj = proj.reshape(b, s, PROJ_W)
        ya = _swa(proj, slopes, swa_sinks[l])
        yb = _sb(proj)
        xf = _out_proj(xf, ya.reshape(b * s, SWA_Q), yb.reshape(b * s, SB_W),
                       swa_out_norm[l].reshape(1, SWA_Q), sb_out_norm[l].reshape(1, SB_W),
                       w_out[l].astype(jnp.bfloat16))

        xf = _ffn(xf, ffn2_norm[l].reshape(1, d), ffn2_w_gate[l], ffn2_w_up[l], ffn2_w_down[l],
                  fg, final_norm=(l == depth - 1))
    return xf.reshape(b, s, d)
```

```python
import functools
import math

import jax
import jax.numpy as jnp
from jax import lax
from jax.experimental import pallas as pl
from jax.experimental.pallas import tpu as pltpu

D_MODEL = 2048
HEAD_DIM = 64
N_SWA_HEADS = 16
N_SWA_KV = 4
N_SB_HEADS = 16
WINDOW = 128
BLOCK = 128
D_FF = 5504
EPS = 1e-6
LOG2E = math.log2(math.e)

LANES = 128
FFN_TM = 1024
FFN_TF = 512
FFN_HEAD_TF = 256
PROJ_TM = 1024
PROJ_TN = 1536
OUT_TM = 512
SB_TK = 128
VMEM_LIMIT = 56 * 1024 * 1024
FFN_MAIN_VMEM_LIMIT = 62 * 1024 * 1024

SWA_Q = N_SWA_HEADS * HEAD_DIM
SB_W = N_SB_HEADS * HEAD_DIM
COL_KA = SWA_Q
COL_VA = COL_KA + 2 * N_SWA_KV * HEAD_DIM
COL_QB = COL_VA + 2 * N_SWA_KV * HEAD_DIM
COL_KB = COL_QB + SB_W
COL_VB = COL_KB + SB_W
PROJ_W = COL_VB + SB_W
IN_W = SWA_Q + 2 * N_SWA_KV * HEAD_DIM + 3 * SB_W
assert PROJ_TN == SWA_Q + 2 * N_SWA_KV * HEAD_DIM and IN_W % PROJ_TN == 0

_NT = (((1,), (1,)), ((), ()))


def _rms(x, g):
    return x * lax.rsqrt(jnp.mean(x * x, axis=-1, keepdims=True) + EPS) * g


def _ffn_kernel(*refs, tf, final_norm, cast_weights):
    if cast_weights:
        (x_ref, g_ref, wg_ref, wu_ref, wd_ref, fg_ref,
         o_ref, wg_out, wu_out, wd_out, h_ref) = refs
    else:
        x_ref, g_ref, wg_ref, wu_ref, wd_ref, fg_ref, _, o_ref, h_ref = refs
    j = pl.program_id(1)
    last = pl.cdiv(D_FF, tf) - 1
    tail = D_FF - last * tf

    def half_swiglu(h, width):
        wg, wu, wd = wg_ref[:, :width], wu_ref[:, :width], wd_ref[:width, :]
        if cast_weights:
            wg, wu, wd = (w.astype(jnp.bfloat16) for w in (wg, wu, wd))
            wg_out[:, :width] = wg
            wu_out[:, :width] = wu
            wd_out[:width, :] = wd
        gate = jnp.dot(h, wg, preferred_element_type=jnp.float32)
        up = jnp.dot(h, wu, preferred_element_type=jnp.float32)
        act = (gate * jax.nn.sigmoid(gate) * up * 0.5).astype(jnp.bfloat16)
        return jnp.dot(act, wd, preferred_element_type=jnp.float32)

    @pl.when(j == 0)
    def _():
        x = x_ref[...]
        h = _rms(x, g_ref[...]).astype(jnp.bfloat16)
        h_ref[...] = h
        o_ref[...] = x + half_swiglu(h, tf)

    @pl.when(jnp.logical_and(j > 0, j < last))
    def _():
        o_ref[...] += half_swiglu(h_ref[...], tf)

    @pl.when(j == last)
    def _():
        y = o_ref[...] + half_swiglu(h_ref[...], tail)
        o_ref[...] = _rms(y, fg_ref[...]) if final_norm else y


def _ffn_main_kernel(x_ref, g_ref, wg_hbm, wu_hbm, wd_hbm, fg_ref, _, o_ref,
                     h_ref, wg_buf, wu_buf, wd_buf, sem, *, final_norm):
    i = pl.program_id(0)
    n_chunks = pl.cdiv(D_FF, FFN_TF)
    tail = D_FF - (n_chunks - 1) * FFN_TF

    def copies(k, width):
        slot = (k + i) % 2
        wraps = isinstance(k, int) and k == n_chunks
        off = 0 if wraps else pl.multiple_of(k * FFN_TF, FFN_TF)
        return (
            pltpu.make_async_copy(wg_hbm.at[:, pl.ds(off, width)],
                                  wg_buf.at[slot, :, pl.ds(0, width)], sem.at[slot, 0]),
            pltpu.make_async_copy(wu_hbm.at[:, pl.ds(off, width)],
                                  wu_buf.at[slot, :, pl.ds(0, width)], sem.at[slot, 1]),
            pltpu.make_async_copy(wd_hbm.at[pl.ds(off, width), :],
                                  wd_buf.at[slot, pl.ds(0, width), :], sem.at[slot, 2]),
        )

    def start(k, width=FFN_TF):
        for c in copies(k, width):
            c.start()

    def wait(k, width=FFN_TF):
        for c in copies(k, width):
            c.wait()

    def half_swiglu(h, k, width=FFN_TF):
        slot = (k + i) % 2
        wg, wu, wd = wg_buf[slot], wu_buf[slot], wd_buf[slot]
        gate = jnp.dot(h, wg[:, :width], preferred_element_type=jnp.float32)
        up = jnp.dot(h, wu[:, :width], preferred_element_type=jnp.float32)
        act = (gate * jax.nn.sigmoid(gate) * up * 0.5).astype(jnp.bfloat16)
        return jnp.dot(act, wd[:width, :], preferred_element_type=jnp.float32)

    @pl.when(i == 0)
    def _():
        start(0)

    start(1)
    wait(0)
    x = x_ref[...]
    h = _rms(x, g_ref[...]).astype(jnp.bfloat16)
    h_ref[...] = h
    o_ref[...] = x + half_swiglu(h, 0)

    def middle(k, carry):
        start(k + 1)
        wait(k)
        o_ref[...] += half_swiglu(h_ref[...], k)
        return carry

    lax.fori_loop(1, n_chunks - 2, middle, 0)

    k = n_chunks - 2
    start(k + 1, tail)
    wait(k)
    o_ref[...] += half_swiglu(h_ref[...], k)

    @pl.when(i + 1 < pl.num_programs(0))
    def _():
        start(n_chunks)
    wait(n_chunks - 1, tail)
    y = o_ref[...] + half_swiglu(h_ref[...], n_chunks - 1, tail)
    o_ref[...] = _rms(y, fg_ref[...]) if final_norm else y


def _ffn(x, g, w_gate, w_up, w_down, fg, final_norm):
    n = x.shape[0]
    suffix = "_final" if final_norm else ""
    params = pltpu.CompilerParams(dimension_semantics=("arbitrary", "arbitrary"),
                                  vmem_limit_bytes=VMEM_LIMIT)
    vec = pl.BlockSpec((1, D_MODEL), lambda i, j: (0, 0))

    def weight_specs(tf):
        return [pl.BlockSpec((D_MODEL, tf), lambda i, j: (0, j)),
                pl.BlockSpec((D_MODEL, tf), lambda i, j: (0, j)),
                pl.BlockSpec((tf, D_MODEL), lambda i, j: (j, 0))]

    out_shape = jax.ShapeDtypeStruct((n, D_MODEL), jnp.float32)
    head_out, wg, wu, wd = pl.pallas_call(
        functools.partial(_ffn_kernel, tf=FFN_HEAD_TF, final_norm=final_norm, cast_weights=True),
        grid=(1, pl.cdiv(D_FF, FFN_HEAD_TF)),
        in_specs=[pl.BlockSpec((FFN_TM, D_MODEL), lambda i, j: (0, 0), pipeline_mode=pl.Buffered(1)),
                  vec, *weight_specs(FFN_HEAD_TF), vec],
        out_specs=[pl.BlockSpec((FFN_TM, D_MODEL), lambda i, j: (0, 0)),
                   *weight_specs(FFN_HEAD_TF)],
        out_shape=[out_shape,
                   jax.ShapeDtypeStruct(w_gate.shape, jnp.bfloat16),
                   jax.ShapeDtypeStruct(w_up.shape, jnp.bfloat16),
                   jax.ShapeDtypeStruct(w_down.shape, jnp.bfloat16)],
        scratch_shapes=[pltpu.VMEM((FFN_TM, D_MODEL), jnp.bfloat16)],
        compiler_params=params,
        name="ffn_head" + suffix,
    )(x, g, w_gate, w_up, w_down, fg)

    hbm = pl.BlockSpec(memory_space=pl.ANY)
    vec1 = pl.BlockSpec((1, D_MODEL), lambda i: (0, 0))
    return pl.pallas_call(
        functools.partial(_ffn_main_kernel, final_norm=final_norm),
        grid=(n // FFN_TM - 1,),
        in_specs=[pl.BlockSpec((FFN_TM, D_MODEL), lambda i: (i + 1, 0)),
                  vec1, hbm, hbm, hbm, vec1, hbm],
        out_specs=pl.BlockSpec((FFN_TM, D_MODEL), lambda i: (i + 1, 0)),
        out_shape=out_shape,
        input_output_aliases={6: 0},
        scratch_shapes=[pltpu.VMEM((FFN_TM, D_MODEL), jnp.bfloat16),
                        pltpu.VMEM((2, D_MODEL, FFN_TF), jnp.bfloat16),
                        pltpu.VMEM((2, D_MODEL, FFN_TF), jnp.bfloat16),
                        pltpu.VMEM((2, FFN_TF, D_MODEL), jnp.bfloat16),
                        pltpu.SemaphoreType.DMA((2, 3))],
        compiler_params=pltpu.CompilerParams(dimension_semantics=("arbitrary",),
                                             vmem_limit_bytes=FFN_MAIN_VMEM_LIMIT),
        name="ffn" + suffix,
    )(x, g, wg, wu, wd, fg, head_out)


def _in_proj_kernel(x_ref, g_ref, w_ref, s_ref, o_ref, h_ref):
    def project(h):
        acc = jnp.dot(h, w_ref[...], preferred_element_type=jnp.float32)
        return acc * s_ref[...]

    @pl.when(pl.program_id(1) == 0)
    def _():
        h = _rms(x_ref[...], g_ref[...]).astype(jnp.bfloat16)
        h_ref[...] = h
        p = project(h)
        o_ref[:, :SWA_Q] = p[:, :SWA_Q].astype(jnp.bfloat16)
        for g in range(2 * N_SWA_KV):
            head = p[:, SWA_Q + g * HEAD_DIM:SWA_Q + (g + 1) * HEAD_DIM]
            o_ref[:, COL_KA + g * LANES:COL_KA + (g + 1) * LANES] = jnp.concatenate(
                [head, head], axis=1).astype(jnp.bfloat16)

    for jb in range(1, IN_W // PROJ_TN):
        @pl.when(pl.program_id(1) == jb)
        def _(jb=jb):
            start = jb * PROJ_TN + PROJ_W - IN_W
            o_ref[:, start:start + PROJ_TN] = project(h_ref[...]).astype(jnp.bfloat16)


def _in_proj(x, g, w, col_scale):
    n = x.shape[0]
    grid = (n // PROJ_TM, IN_W // PROJ_TN)
    return pl.pallas_call(
        _in_proj_kernel,
        grid=grid,
        in_specs=[
            pl.BlockSpec((PROJ_TM, D_MODEL), lambda i, j: (i, 0)),
            pl.BlockSpec((1, D_MODEL), lambda i, j: (0, 0)),
            pl.BlockSpec((D_MODEL, PROJ_TN), lambda i, j: (0, j)),
            pl.BlockSpec((1, PROJ_TN), lambda i, j: (0, j)),
        ],
        out_specs=pl.BlockSpec((PROJ_TM, PROJ_W), lambda i, j: (i, 0)),
        out_shape=jax.ShapeDtypeStruct((n, PROJ_W), jnp.bfloat16),
        scratch_shapes=[pltpu.VMEM((PROJ_TM, D_MODEL), jnp.bfloat16)],
        compiler_params=pltpu.CompilerParams(
            dimension_semantics=("arbitrary", "arbitrary"),
            vmem_limit_bytes=VMEM_LIMIT),
        name="in_proj",
    )(x, g, w, col_scale)


def _swa_kernel(slopes_ref, sinks_ref, q_ref, k_ref, v_ref, o_ref, bias_ref):
    g = pl.program_id(1)
    n_heads = N_SWA_HEADS // N_SWA_KV
    nb = q_ref.shape[1] // BLOCK
    row = lax.broadcasted_iota(jnp.int32, (BLOCK, BLOCK), 0)
    col = lax.broadcasted_iota(jnp.int32, (BLOCK, BLOCK), 1)
    low_half = col < HEAD_DIM
    dist_p = (row + BLOCK - col).astype(jnp.float32)
    dist_c = (row - col).astype(jnp.float32)
    for jh in range(n_heads):
        slope2 = slopes_ref[g * n_heads + jh] * LOG2E
        bias_ref[jh, :, :BLOCK] = jnp.where(col > row, -slope2 * dist_p, -jnp.inf)
        bias_ref[jh, :, BLOCK:] = jnp.where(col <= row, -slope2 * dist_c, -jnp.inf)
    ones = jnp.ones((BLOCK, LANES), jnp.bfloat16)
    prev_cols = lax.broadcasted_iota(jnp.int32, (BLOCK, 2 * BLOCK), 1) < BLOCK

    for qb in range(nb):
        q0 = qb * BLOCK
        p0 = max(qb - 1, 0) * BLOCK
        kk = jnp.concatenate([k_ref[0, p0:p0 + BLOCK, :],
                              k_ref[0, q0:q0 + BLOCK, :]], axis=0)
        vv = jnp.concatenate(
            [jnp.concatenate([v_ref[0, p0:p0 + BLOCK, :], ones], axis=1),
             jnp.concatenate([v_ref[0, q0:q0 + BLOCK, :], ones], axis=1)], axis=0)
        q = q_ref[0, q0:q0 + BLOCK, :]
        outs = []
        for jh in range(n_heads):
            sink2 = sinks_ref[g * n_heads + jh] * LOG2E
            qpair = q[:, (jh // 2) * LANES:(jh // 2 + 1) * LANES]
            keep = low_half if jh % 2 == 0 else jnp.logical_not(low_half)
            qh = jnp.where(keep, qpair, jnp.zeros_like(qpair))
            bias = bias_ref[jh]
            if qb == 0:
                bias = jnp.where(prev_cols, -jnp.inf, bias)
            s = lax.dot_general(qh, kk, _NT, preferred_element_type=jnp.float32) + bias
            m = jnp.maximum(jnp.max(s, axis=-1, keepdims=True), sink2)
            e = jnp.exp2(s - m).astype(jnp.bfloat16)
            pv = jnp.dot(e, vv, preferred_element_type=jnp.float32)
            outs.append(pv[:, :LANES] / (pv[:, LANES:] + jnp.exp2(sink2 - m)))
        o_ref[0, q0:q0 + BLOCK, :] = jnp.concatenate(
            [jnp.where(low_half, outs[0], outs[1]), jnp.where(low_half, outs[2], outs[3])],
            axis=1)


def _swa(proj, slopes, sinks):
    b, s, _ = proj.shape
    kblk, vblk = COL_KA // LANES, COL_VA // LANES
    smem = pl.BlockSpec(memory_space=pltpu.SMEM)
    return pl.pallas_call(
        _swa_kernel,
        grid=(b, N_SWA_KV),
        in_specs=[
            smem, smem,
            pl.BlockSpec((1, s, 2 * LANES), lambda i, g: (i, 0, g)),
            pl.BlockSpec((1, s, LANES), lambda i, g: (i, 0, kblk + g)),
            pl.BlockSpec((1, s, LANES), lambda i, g: (i, 0, vblk + g)),
        ],
        out_specs=pl.BlockSpec((1, s, 2 * LANES), lambda i, g: (i, 0, g)),
        out_shape=jax.ShapeDtypeStruct((b, s, SWA_Q), jnp.float32),
        scratch_shapes=[pltpu.VMEM((N_SWA_HEADS // N_SWA_KV, BLOCK, 2 * BLOCK), jnp.float32)],
        compiler_params=pltpu.CompilerParams(
            dimension_semantics=("arbitrary", "arbitrary")),
        name="swa",
    )(slopes, sinks, proj, proj, proj)


def _sb_kernel(q_ref, k_ref, v_ref, o_ref, carry_ref, acc_ref):
    s_len = q_ref.shape[1]
    r = lax.broadcasted_iota(jnp.int32, (2 * SB_TK, 2 * SB_TK), 0)
    c = lax.broadcasted_iota(jnp.int32, (2 * SB_TK, 2 * SB_TK), 1)
    tri = jnp.where(jnp.logical_and(r > c, (r < SB_TK) == (c < SB_TK)), 1.0, 0.0
                    ).astype(jnp.bfloat16)
    low_half = lax.broadcasted_iota(jnp.int32, (SB_TK, LANES), 1) < HEAD_DIM
    strict_lower = (lax.broadcasted_iota(jnp.int32, (SB_TK, SB_TK), 1)
                    < lax.broadcasted_iota(jnp.int32, (SB_TK, SB_TK), 0))

    carry_ref[...] = jnp.zeros_like(carry_ref)
    acc_ref[...] = jnp.zeros_like(acc_ref)

    def rows(kk, vv, r0, r1):
        def mask_top(x):
            top = jnp.where(strict_lower, x[:SB_TK], 0.0)
            return top if r1 - r0 == SB_TK else jnp.concatenate([top, x[SB_TK:]], axis=0)

        u2 = lax.dot_general(q_ref[0, r0:r1, :], kk, _NT,
                             preferred_element_type=jnp.float32)
        sps = []
        for hd in range(2):
            u = u2[:, hd * SB_TK:(hd + 1) * SB_TK]
            sp = jnp.maximum(u, 0.0) + jnp.log(1.0 + jnp.exp2(-jnp.abs(u))) * LOG2E
            sps.append(mask_top(sp))
        later = jnp.dot(jnp.concatenate([sp.astype(jnp.bfloat16) for sp in sps], axis=1), tri,
                        preferred_element_type=jnp.float32)
        a_heads = []
        for hd in range(2):
            u = u2[:, hd * SB_TK:(hd + 1) * SB_TK]
            s_incl = later[:, hd * SB_TK:(hd + 1) * SB_TK] + sps[hd]
            total = carry_ref[hd, r0:r1, :] + s_incl
            a = mask_top(jnp.exp2(u - total))
            carry_ref[hd, r0:r1, :] = jnp.broadcast_to(total[:, 0:1], total.shape)
            a_heads.append(a.astype(jnp.bfloat16))
        acc_ref[r0:r1, :] += jnp.dot(jnp.concatenate(a_heads, axis=1), vv,
                                     preferred_element_type=jnp.float32)

    for d in range(s_len // SB_TK - 1, -1, -1):
        k0 = d * SB_TK
        kblk = k_ref[0, k0:k0 + SB_TK, :]
        vblk = v_ref[0, k0:k0 + SB_TK, :]
        zero = jnp.zeros_like(kblk)
        kk = jnp.concatenate([jnp.where(low_half, kblk, zero),
                              jnp.where(low_half, zero, kblk)], axis=0)
        vv = jnp.concatenate([jnp.where(low_half, vblk, zero),
                              jnp.where(low_half, zero, vblk)], axis=0)
        rows(kk, vv, k0, s_len)
    o_ref[0] = acc_ref[...]


def _sb(proj):
    b, s, _ = proj.shape
    npair = N_SB_HEADS // 2
    qblk, kblk, vblk = COL_QB // LANES, COL_KB // LANES, COL_VB // LANES
    return pl.pallas_call(
        _sb_kernel,
        grid=(b, npair),
        in_specs=[
            pl.BlockSpec((1, s, LANES), lambda i, p: (i, 0, qblk + p)),
            pl.BlockSpec((1, s, LANES), lambda i, p: (i, 0, kblk + p)),
            pl.BlockSpec((1, s, LANES), lambda i, p: (i, 0, vblk + p)),
        ],
        out_specs=pl.BlockSpec((1, s, LANES), lambda i, p: (i, 0, p)),
        out_shape=jax.ShapeDtypeStruct((b, s, SB_W), jnp.float32),
        scratch_shapes=[pltpu.VMEM((2, s, SB_TK), jnp.float32),
                        pltpu.VMEM((s, LANES), jnp.float32)],
        compiler_params=pltpu.CompilerParams(
            dimension_semantics=("arbitrary", "arbitrary")),
        name="sb",
    )(proj, proj, proj)


def _out_proj_kernel(x_ref, ya_ref, yb_ref, ga_ref, gb_ref, w_ref, o_ref):
    y = jnp.concatenate([_rms(ya_ref[...], ga_ref[...]), _rms(yb_ref[...], gb_ref[...])],
                        axis=1).astype(jnp.bfloat16)
    o_ref[...] = x_ref[...] + jnp.dot(y, w_ref[...], preferred_element_type=jnp.float32)


def _out_proj(x, ya, yb, ga, gb, w):
    n = x.shape[0]
    return pl.pallas_call(
        _out_proj_kernel,
        grid=(n // OUT_TM,),
        in_specs=[
            pl.BlockSpec((OUT_TM, D_MODEL), lambda i: (i, 0)),
            pl.BlockSpec((OUT_TM, SWA_Q), lambda i: (i, 0)),
            pl.BlockSpec((OUT_TM, SB_W), lambda i: (i, 0)),
            pl.BlockSpec((1, SWA_Q), lambda i: (0, 0)),
            pl.BlockSpec((1, SB_W), lambda i: (0, 0)),
            pl.BlockSpec((SWA_Q + SB_W, D_MODEL), lambda i: (0, 0)),
        ],
        out_specs=pl.BlockSpec((OUT_TM, D_MODEL), lambda i: (i, 0)),
        out_shape=jax.ShapeDtypeStruct((n, D_MODEL), jnp.float32),
        compiler_params=pltpu.CompilerParams(
            dimension_semantics=("arbitrary",),
            vmem_limit_bytes=VMEM_LIMIT),
        name="out_proj",
    )(x, ya, yb, ga, gb, w)


def _proj_col_scale():
    qb0 = SWA_Q + 2 * N_SWA_KV * HEAD_DIM
    s = jnp.ones((IN_W,), jnp.float32)
    s = s.at[:SWA_Q].set(HEAD_DIM ** -0.5 * LOG2E)
    s = s.at[qb0:qb0 + SB_W].set(HEAD_DIM ** -0.5 * LOG2E)
    return s.reshape(1, IN_W)


def kernel(x, ffn1_norm, ffn1_w_gate, ffn1_w_up, ffn1_w_down, mix_norm, w_in, swa_sinks,
           swa_out_norm, sb_out_norm, w_out, ffn2_norm, ffn2_w_gate, ffn2_w_up, ffn2_w_down,
           final_norm):
    b, s, d = x.shape
    depth = ffn1_norm.shape[0]
    xf = x.reshape(b * s, d)
    i = jnp.arange(1, N_SWA_HEADS + 1, dtype=jnp.float32)
    slopes = jnp.exp2(-8.0 * i / N_SWA_HEADS)
    fg = final_norm.reshape(1, d)
    col_scale = _proj_col_scale()
    for l in range(depth):
        xf = _ffn(xf, ffn1_norm[l].reshape(1, d), ffn1_w_gate[l], ffn1_w_up[l], ffn1_w_down[l],
                  fg, final_norm=False)

        proj = _in_proj(xf, mix_norm[l].reshape(1, d), w_in[l].astype(jnp.bfloat16), col_scale)
        proj = proj.reshape(b, s, PROJ_W)
        ya = _swa(proj, slopes, swa_sinks[l])
        yb = _sb(proj)
        xf = _out_proj(xf, ya.reshape(b * s, SWA_Q), yb.reshape(b * s, SB_W),
                       swa_out_norm[l].reshape(1, SWA_Q), sb_out_norm[l].reshape(1, SB_W),
                       w_out[l].astype(jnp.bfloat16))

        xf = _ffn(xf, ffn2_norm[l].reshape(1, d), ffn2_w_gate[l], ffn2_w_up[l], ffn2_w_down[l],
                  fg, final_norm=(l == depth - 1))
    return xf.reshape(b, s, d)
```
